```python
import math
import jax, jax.numpy as jnp
from jax import lax
import numpy as np

D_MODEL = 1024
BATCH = 16
SEQ = 256
DEPTH = 2
DEC_BATCH = 8
DEC_SEQ = 2048
PAST_LEN = 256

GRID_W = 64
HEAD_DIM = 64
H_RET = 8
H_Q = 8
H_KV = 2
N_REP = H_Q // H_KV
CHUNK = 128
Q_BLOCK = 128
CM_GROUPS = 4
RET_W = H_RET * HEAD_DIM
ATT_Q_W = H_Q * HEAD_DIM
ATT_KV_W = H_KV * HEAD_DIM
CM_W = 512
CM_GROUP_CH = CM_W // CM_GROUPS
D_FF = 4 * D_MODEL
ROPE_BASE = 10000.0
EPS = 1e-6
IN_SIZES = (RET_W, RET_W, RET_W, RET_W, ATT_Q_W, ATT_KV_W, ATT_KV_W, CM_W, CM_W, D_MODEL, D_MODEL, D_MODEL)
IN_W = 4 * RET_W + ATT_Q_W + 2 * ATT_KV_W + 2 * CM_W + 3 * D_MODEL

kernel_name = 'hybrid_diffusion_prefix_context_step'


def rms_norm(x, g):
    xf = x.astype(jnp.float32)
    y = xf * lax.rsqrt(jnp.mean(xf * xf, axis=-1, keepdims=True) + EPS)
    return (y * g.astype(jnp.float32)).astype(x.dtype)


def head_group_norm(o, g):
    B, L, H, d = o.shape
    mu = jnp.mean(o, axis=-1, keepdims=True)
    var = jnp.mean(jnp.square(o - mu), axis=-1, keepdims=True)
    y = (o - mu) * lax.rsqrt(var + EPS)
    return y.reshape(B, L, H * d) * g.astype(jnp.float32)


def split_proj(z):
    idx, acc = [], 0
    for s in IN_SIZES[:-1]:
        acc += s
        idx.append(acc)
    return jnp.split(z, idx, axis=-1)


def modulation(cond, w_mod, b_mod):
    m = jax.nn.silu(cond) @ w_mod + b_mod
    return [t[:, None, :] for t in jnp.split(m, 6, axis=-1)]


def axial_rope_tables(length):
    rows = length // GRID_W
    row = jnp.repeat(jnp.arange(rows, dtype=jnp.float32), GRID_W)
    col = jnp.tile(jnp.arange(GRID_W, dtype=jnp.float32), rows)
    half = HEAD_DIM // 2
    inv = 1.0 / (ROPE_BASE ** (jnp.arange(0, half, 2, dtype=jnp.float32) / half))
    ang = jnp.concatenate([row[:, None] * inv[None, :], col[:, None] * inv[None, :]], axis=-1)
    return jnp.cos(ang), jnp.sin(ang)


def apply_rope(x, cos, sin):
    xf = x.astype(jnp.float32)
    x1, x2 = xf[..., 0::2], xf[..., 1::2]
    c = cos[None, :, None, :]
    s = sin[None, :, None, :]
    out = jnp.stack([x1 * c - x2 * s, x1 * s + x2 * c], axis=-1).reshape(x.shape)
    return out.astype(x.dtype)


def retention_scan(q, k, v, log_decay, s0):
    B, L, H, d = q.shape
    nc = L // CHUNK
    qc = q.astype(jnp.float32).reshape(B, nc, CHUNK, H, d)
    kc = k.astype(jnp.float32).reshape(B, nc, CHUNK, H, d)
    vc = v.astype(jnp.float32).reshape(B, nc, CHUNK, H, d)
    ld = log_decay.astype(jnp.float32)
    pos = jnp.arange(CHUNK, dtype=jnp.float32)
    rel = pos[:, None] - pos[None, :]
    dmask = jnp.where(rel[None] >= 0, jnp.exp(ld[:, None, None] * jnp.maximum(rel, 0.0)[None]), 0.0)
    scores = jnp.einsum('bnihd,bnjhd->bnhij', qc, kc) * dmask[None, None]
    inner = jnp.einsum('bnhij,bnjhe->bnihe', scores, vc)
    zeta = jnp.exp(ld[None, :] * (CHUNK - 1.0 - pos)[:, None])
    upd = jnp.einsum('bnjhd,jh,bnjhe->nbhde', kc, zeta, vc)
    chunk_decay = jnp.exp(ld * CHUNK)[None, :, None, None]

    def step(s, u):
        return chunk_decay * s + u, s

    s_final, s_prev = lax.scan(step, s0.astype(jnp.float32), upd)
    xi = jnp.exp(ld[None, :] * (pos + 1.0)[:, None])
    cross = jnp.einsum('bnihd,ih,nbhde->bnihe', qc, xi, s_prev)
    return (inner + cross).reshape(B, L, H, d), s_final


def bidir_retention(q, k, v, decay_param, s0):
    ld = -jax.nn.softplus(decay_param.astype(jnp.float32))
    o_f, s_f = retention_scan(q, k, v, ld[0], s0[:, 0])
    o_b, s_b = retention_scan(q[:, ::-1], k[:, ::-1], v[:, ::-1], ld[1], s0[:, 1])
    return o_f + o_b[:, ::-1], jnp.stack([s_f, s_b], axis=1)


def block_attention(q, k, v):
    B, L, _, d = q.shape
    nb = L // Q_BLOCK
    qb = q.reshape(B, nb, Q_BLOCK, H_KV, N_REP, d).transpose(1, 0, 2, 3, 4, 5)
    kf = k.astype(jnp.float32)
    vf = v.astype(jnp.float32)
    scale = d ** -0.5

    def one_block(qi):
        s = jnp.einsum('bqgrd,bkgd->bgrqk', qi.astype(jnp.float32), kf) * scale
        p = jax.nn.softmax(s, axis=-1)
        return jnp.einsum('bgrqk,bkgd->bqgrd', p, vf)

    o = lax.map(one_block, qb)
    return o.transpose(1, 0, 2, 3, 4, 5).reshape(B, L, H_Q * d).astype(q.dtype)


def chunk_mlp(u, v, g_norm, ws, bs):
    B, L, _ = u.shape
    u = jax.nn.gelu(u)
    v = rms_norm(jax.nn.gelu(v), g_norm)
    vc = v.reshape(B, L // CHUNK, CHUNK, CM_GROUPS, CM_GROUP_CH)
    mixed = jnp.einsum('gpq,bnqgc->bnpgc', ws, vc) + bs.T[None, None, :, :, None]
    return u * mixed.reshape(B, L, CM_W)


def token_mixers(h, lp, is_latent, cos, sin, ctx_k, ctx_v, s0):
    B, L, _ = h.shape
    rq, rk, rv, rg, aq, ak, av, cu, cv, gr, ga, gc = split_proj(h @ lp['w_in'])

    def heads(t, n):
        return t.reshape(B, L, n, HEAD_DIM)

    rq, rk, rv = heads(rq, H_RET), heads(rk, H_RET) * (HEAD_DIM ** -0.5), heads(rv, H_RET)
    aq = rms_norm(heads(aq, H_Q), lp['g_q'])
    ak = rms_norm(heads(ak, H_KV), lp['g_k'])
    av = heads(av, H_KV)
    if is_latent:
        rq, rk = apply_rope(rq, cos, sin), apply_rope(rk, cos, sin)
        q_att = apply_rope(aq, cos, sin)
        keys = jnp.concatenate([ctx_k.astype(h.dtype), apply_rope(ak, cos, sin)], axis=1)
        vals = jnp.concatenate([ctx_v.astype(h.dtype), av], axis=1)
        state0 = s0
    else:
        q_att, keys, vals = aq, ak, av
        state0 = jnp.zeros((B, 2, H_RET, HEAD_DIM, HEAD_DIM), jnp.float32)
    ret_o, s_fin = bidir_retention(rq, rk, rv, lp['ret_decay'], state0)
    ret_o = (head_group_norm(ret_o, lp['ret_gn']) * jax.nn.silu(rg.astype(jnp.float32))).astype(h.dtype)
    att_o = block_attention(q_att, keys, vals)
    cm_o = chunk_mlp(cu, cv, lp['cm_norm'], lp['cm_ws'], lp['cm_bs'])
    merged = (jax.nn.sigmoid(gr) * (ret_o @ lp['w_br_ret'])
              + jax.nn.sigmoid(ga) * (att_o @ lp['w_br_att'])
              + jax.nn.sigmoid(gc) * (cm_o @ lp['w_br_cm']))
    return merged @ lp['w_out'], ak, av, s_fin


def trunk_layer(x, cond, lp, is_latent, cos, sin, ctx_k, ctx_v, s0):
    sh1, sc1, g1, sh2, sc2, g2 = modulation(cond, lp['w_mod'], lp['b_mod'])
    h = rms_norm(x, lp['g_norm1']) * (1.0 + sc1) + sh1
    y, k_ctx, v_ctx, s_ctx = token_mixers(h, lp, is_latent, cos, sin, ctx_k, ctx_v, s0)
    x = x + g1 * y
    h = rms_norm(x, lp['g_norm2']) * (1.0 + sc2) + sh2
    f = jnp.square(jax.nn.relu(h @ lp['w_ff1'])) @ lp['w_ff2']
    x = x + g2 * f
    return x, k_ctx, v_ctx, s_ctx


def setup_inputs(seed: int = 0) -> dict:
    key = jax.random.key(seed)
    ks = jax.random.split(key, 26)
    f32 = jnp.float32

    def nrm(k, shape, scale=1.0):
        return jax.random.normal(k, shape, f32) * scale

    gam = 1.0 - 2.0 ** (-5.0 - np.arange(H_RET, dtype=np.float32))
    decay_base = np.log(np.expm1(-np.log(gam))).astype(np.float32)
    return {
        'x_prompt': nrm(ks[0], (BATCH, SEQ, D_MODEL)),
        'x_sample': nrm(ks[1], (DEC_BATCH, DEC_SEQ, D_MODEL)),
        'cache_k': nrm(ks[2], (DEC_BATCH, DEPTH, PAST_LEN, H_KV, HEAD_DIM)),
        'cache_v': nrm(ks[3], (DEC_BATCH, DEPTH, PAST_LEN, H_KV, HEAD_DIM)),
        'state_ret': nrm(ks[4], (DEC_BATCH, DEPTH, 2, H_RET, HEAD_DIM, HEAD_DIM), 0.5),
        'c': nrm(ks[5], (DEC_BATCH, D_MODEL)),
        'c_ctx': nrm(ks[6], (D_MODEL,)),
        'w_mod': nrm(ks[7], (DEPTH, D_MODEL, 6 * D_MODEL), 0.5 * D_MODEL ** -0.5),
        'b_mod': nrm(ks[8], (DEPTH, 6 * D_MODEL), 0.02),
        'g_norm1': 1.0 + nrm(ks[9], (DEPTH, D_MODEL), 0.02),
        'g_norm2': 1.0 + nrm(ks[10], (DEPTH, D_MODEL), 0.02),
        'w_in': nrm(ks[11], (DEPTH, D_MODEL, IN_W), D_MODEL ** -0.5),
        'g_q': 1.0 + nrm(ks[12], (DEPTH, HEAD_DIM), 0.02),
        'g_k': 1.0 + nrm(ks[13], (DEPTH, HEAD_DIM), 0.02),
        'ret_decay': jnp.asarray(decay_base)[None, None, :] + nrm(ks[14], (DEPTH, 2, H_RET), 0.05),
        'ret_gn': 1.0 + nrm(ks[15], (DEPTH, RET_W), 0.02),
        'cm_norm': 1.0 + nrm(ks[16], (DEPTH, CM_W), 0.02),
        'cm_ws': nrm(ks[17], (DEPTH, CM_GROUPS, CHUNK, CHUNK), CHUNK ** -0.5),
        'cm_bs': 1.0 + nrm(ks[18], (DEPTH, CM_GROUPS, CHUNK), 0.02),
        'w_br_ret': nrm(ks[19], (DEPTH, RET_W, D_MODEL), RET_W ** -0.5),
        'w_br_att': nrm(ks[20], (DEPTH, ATT_Q_W, D_MODEL), ATT_Q_W ** -0.5),
        'w_br_cm': nrm(ks[21], (DEPTH, CM_W, D_MODEL), CM_W ** -0.5),
        'w_out': nrm(ks[22], (DEPTH, D_MODEL, D_MODEL), D_MODEL ** -0.5),
        'w_ff1': nrm(ks[23], (DEPTH, D_MODEL, D_FF), D_MODEL ** -0.5),
        'w_ff2': nrm(ks[24], (DEPTH, D_FF, D_MODEL), D_FF ** -0.5),
        'g_final': 1.0 + nrm(ks[25], (D_MODEL,), 0.02),
    }


def reference(x_prompt, x_sample, cache_k, cache_v, state_ret, c, c_ctx,
              w_mod, b_mod, g_norm1, g_norm2, w_in, g_q, g_k, ret_decay, ret_gn,
              cm_norm, cm_ws, cm_bs, w_br_ret, w_br_att, w_br_cm, w_out, w_ff1, w_ff2, g_final):
    cond_ctx = jnp.broadcast_to(c_ctx[None, :], (x_prompt.shape[0], D_MODEL))
    cos, sin = axial_rope_tables(x_sample.shape[1])
    xp, xs = x_prompt, x_sample
    ks_list, vs_list, ss_list = [], [], []
    for l in range(DEPTH):
        lp = {
            'w_mod': w_mod[l], 'b_mod': b_mod[l], 'g_norm1': g_norm1[l], 'g_norm2': g_norm2[l],
            'w_in': w_in[l], 'g_q': g_q[l], 'g_k': g_k[l], 'ret_decay': ret_decay[l], 'ret_gn': ret_gn[l],
            'cm_norm': cm_norm[l], 'cm_ws': cm_ws[l], 'cm_bs': cm_bs[l],
            'w_br_ret': w_br_ret[l], 'w_br_att': w_br_att[l], 'w_br_cm': w_br_cm[l], 'w_out': w_out[l],
            'w_ff1': w_ff1[l], 'w_ff2': w_ff2[l],
        }
        xp, k_l, v_l, s_l = trunk_layer(xp, cond_ctx, lp, False, None, None, None, None, None)
        ks_list.append(k_l)
        vs_list.append(v_l)
        ss_list.append(s_l.astype(x_prompt.dtype))
        xs, _, _, _ = trunk_layer(xs, c, lp, True, cos, sin, cache_k[:, l], cache_v[:, l], state_ret[:, l])
    y_prompt = rms_norm(xp, g_final)
    y_sample = rms_norm(xs, g_final)
    new_cache_k = jnp.stack(ks_list, axis=1)
    new_cache_v = jnp.stack(vs_list, axis=1)
    new_state_ret = jnp.stack(ss_list, axis=1)
    return (y_prompt, y_sample, new_cache_k, new_cache_v, new_state_ret)
```

```python
import functools

import numpy as np
import jax
import jax.numpy as jnp
from jax import lax
from jax.experimental import pallas as pl
from jax.experimental.pallas import tpu as pltpu

D_MODEL = 1024
DEPTH = 2
SEQ = 256
DEC_SEQ = 2048
PAST_LEN = 256
GRID_W = 64
HEAD_DIM = 64
H_RET = 8
H_Q = 8
H_KV = 2
CHUNK = 128
CM_GROUPS = 4
RET_W = 512
ATT_Q_W = 512
ATT_KV_W = 128
CM_W = 512
D_FF = 4096
ROPE_BASE = 10000.0
EPS = 1e-6
IN_W = 6912
N_MOD_ROWS = 16

LANES = 128
VMEM_LIMIT = 48 * 1024 * 1024

F32 = jnp.float32
BF16 = jnp.bfloat16

COL_GR, COL_GA, COL_GC = 0, 1024, 2048
COL_RQ, COL_RK, COL_RV, COL_RG = 3072, 3584, 4096, 4608
COL_AQ, COL_CU, COL_CV = 5120, 5632, 6144
COL_AK, COL_AV = 6656, 6784
Q_HEAD_ORDER = (0, 4, 1, 5, 2, 6, 3, 7)


def _cparams(sem):
    return pltpu.CompilerParams(dimension_semantics=sem, vmem_limit_bytes=VMEM_LIMIT)


def _dot(a, b):
    return jnp.dot(a, b, preferred_element_type=F32)


def _dot_nt(a, b):
    return lax.dot_general(a, b, (((1,), (1,)), ((), ())), preferred_element_type=F32)


def _dot_tn(a, b):
    return lax.dot_general(a, b, (((0,), (0,)), ((), ())), preferred_element_type=F32)


def _lo_mask(shape):
    return lax.broadcasted_iota(jnp.int32, shape, len(shape) - 1) < HEAD_DIM


def _seg_mean(x, lo):
    s_lo = jnp.sum(jnp.where(lo, x, 0.0), axis=-1, keepdims=True)
    s_hi = jnp.sum(jnp.where(lo, 0.0, x), axis=-1, keepdims=True)
    return jnp.where(lo, s_lo, s_hi) * (1.0 / HEAD_DIM)


def _rope(x, cos, sin_signed):
    nxt = pltpu.roll(x, LANES - 1, 1)
    prv = pltpu.roll(x, 1, 1)
    even = (lax.broadcasted_iota(jnp.int32, x.shape, 1) & 1) == 0
    return x * cos + jnp.where(even, nxt, prv) * sin_signed


def _silu(x):
    return x * jax.nn.sigmoid(x)


def _norm_mod(x, g, sc, sh):
    ms = jnp.mean(x * x, axis=-1, keepdims=True)
    y = x * lax.rsqrt(ms + EPS) * g
    return y * (1.0 + sc) + sh


def _mod_kernel(c_ref, w_ref, b_ref, o_ref):
    s = _silu(c_ref[...])
    o_ref[0] = jnp.dot(s, w_ref[0], preferred_element_type=F32,
                       precision=lax.Precision.HIGHEST) + b_ref[0]


def _modulation(cond, w_mod, b_mod):
    tn = 1536
    n = 6 * D_MODEL
    return pl.pallas_call(
        _mod_kernel,
        grid=(DEPTH, n // tn),
        in_specs=[
            pl.BlockSpec((N_MOD_ROWS, D_MODEL), lambda l, j: (0, 0)),
            pl.BlockSpec((1, D_MODEL, tn), lambda l, j: (l, 0, j)),
            pl.BlockSpec((1, 1, tn), lambda l, j: (l, 0, j)),
        ],
        out_specs=pl.BlockSpec((1, N_MOD_ROWS, tn), lambda l, j: (l, 0, j)),
        out_shape=jax.ShapeDtypeStruct((DEPTH, N_MOD_ROWS, n), F32),
        compiler_params=_cparams(("arbitrary", "arbitrary")),
        name="modulation",
    )(cond, w_mod, b_mod.reshape(DEPTH, 1, n))


def _mod_row_fn(tm, seq, latent):
    if latent:
        return lambda i: 1 + (i * tm) // seq
    return lambda i: 0


def _proj_kernel(x_ref, g_ref, sc_ref, sh_ref, w_ref, o_ref, h_ref):
    @pl.when(pl.program_id(1) == 0)
    def _():
        h_ref[...] = _norm_mod(x_ref[...], g_ref[...], sc_ref[0], sh_ref[0]).astype(BF16)

    o_ref[...] = _dot(h_ref[...], w_ref[...])


def _in_proj(x, g, sc, sh, w, seq, latent):
    t = x.shape[0]
    tm, tn = 512, 2304
    mrow = _mod_row_fn(tm, seq, latent)
    return pl.pallas_call(
        _proj_kernel,
        grid=(t // tm, IN_W // tn),
        in_specs=[
            pl.BlockSpec((tm, D_MODEL), lambda i, j: (i, 0)),
            pl.BlockSpec((1, D_MODEL), lambda i, j: (0, 0)),
            pl.BlockSpec((1, 1, D_MODEL), lambda i, j: (mrow(i), 0, 0)),
            pl.BlockSpec((1, 1, D_MODEL), lambda i, j: (mrow(i), 0, 0)),
            pl.BlockSpec((D_MODEL, tn), lambda i, j: (0, j)),
        ],
        out_specs=pl.BlockSpec((tm, tn), lambda i, j: (i, j)),
        out_shape=jax.ShapeDtypeStruct((t, IN_W), F32),
        scratch_shapes=[pltpu.VMEM((tm, D_MODEL), BF16)],
        compiler_params=_cparams(("parallel", "arbitrary")),
        name="in_proj",
    )(x, g, sc, sh, w)


def _ret_kernel(*refs, nc, latent):
    if latent:
        (dec_ref, q_ref, k_ref, v_ref, rg_ref, gn_ref, cos_ref, sin_ref, s0_ref,
         o_ref, cst_ref, uu_ref) = refs
    else:
        (dec_ref, q_ref, k_ref, v_ref, rg_ref, gn_ref,
         o_ref, sfin_ref, cst_ref, uu_ref) = refs
    hp = pl.program_id(1)
    sq = (CHUNK, LANES)
    lane = lax.broadcasted_iota(jnp.int32, sq, 1)
    row = lax.broadcasted_iota(jnp.int32, sq, 0)
    lo = lane < HEAD_DIM
    blockdiag = (row < HEAD_DIM) == lo
    rowf = row.astype(F32)
    rel = (row - lane).astype(F32)

    def log_decay(direction, head):
        d = jnp.full(sq, dec_ref[direction, head], F32)
        return -(jnp.maximum(d, 0.0) + jnp.log1p(jnp.exp(-jnp.abs(d))))

    ldf0, ldf1 = log_decay(0, 2 * hp), log_decay(0, 2 * hp + 1)
    ldb0, ldb1 = log_decay(1, 2 * hp), log_decay(1, 2 * hp + 1)
    ldf = jnp.where(lo, ldf0, ldf1)
    ldb = jnp.where(lo, ldb0, ldb1)

    def decay_mask(lf, lb):
        fwd = jnp.exp(lf * jnp.maximum(rel, 0.0))
        bwd = jnp.exp(lb * jnp.maximum(-rel, 0.0))
        return jnp.where(rel > 0, fwd, jnp.where(rel < 0, bwd, 2.0))

    cst_ref[0] = decay_mask(ldf0, ldb0)
    cst_ref[1] = decay_mask(ldf1, ldb1)
    cst_ref[2] = jnp.exp(ldf * (CHUNK - 1.0 - rowf))
    cst_ref[3] = jnp.exp(ldb * rowf)
    cst_ref[4] = jnp.exp(ldf * (rowf + 1.0))
    cst_ref[5] = jnp.exp(ldb * (CHUNK - rowf))
    gf = jnp.exp(ldf * float(CHUNK))
    gb = jnp.exp(ldb * float(CHUNK))

    def rows_of(c):
        return pl.ds(pl.multiple_of(c * CHUNK, CHUNK), CHUNK)

    def load_qk(ref, c, scale):
        x = ref[rows_of(c), :]
        if scale != 1.0:
            x = x * scale
        if latent:
            x = _rope(x, cos_ref[rows_of(c), :], sin_ref[rows_of(c), :])
        return x

    def phase_a(c, carry):
        k = load_qk(k_ref, c, HEAD_DIM ** -0.5)
        v = v_ref[rows_of(c), :].astype(BF16)
        kz = jnp.concatenate([(k * cst_ref[2]).astype(BF16), (k * cst_ref[3]).astype(BF16)], axis=1)
        uu_ref[c] = _dot_tn(kz, v)
        return carry

    lax.fori_loop(0, nc, phase_a, 0)

    if latent:
        zeros = jnp.zeros((HEAD_DIM, HEAD_DIM), F32)

        def embed(direction):
            top = jnp.concatenate([s0_ref[0, 0, direction, 0], zeros], axis=1)
            bot = jnp.concatenate([zeros, s0_ref[0, 0, direction, 1]], axis=1)
            return jnp.concatenate([top, bot], axis=0)

        sf0, sb0 = embed(0), embed(1)
    else:
        sf0 = sb0 = jnp.zeros(sq, F32)

    def scan_f(c, s):
        u = uu_ref[c, 0:CHUNK, :]
        uu_ref[c, 0:CHUNK, :] = s
        return gf * s + jnp.where(blockdiag, u, 0.0)

    def scan_b(i, s):
        c = nc - 1 - i
        u = uu_ref[c, CHUNK:2 * CHUNK, :]
        uu_ref[c, CHUNK:2 * CHUNK, :] = s
        return gb * s + jnp.where(blockdiag, u, 0.0)

    sf = lax.fori_loop(0, nc, scan_f, sf0)
    sb = lax.fori_loop(0, nc, scan_b, sb0)
    if not latent:
        for d, s in ((0, sf), (1, sb)):
            sfin_ref[0, d, 0] = s[0:HEAD_DIM, 0:HEAD_DIM]
            sfin_ref[0, d, 1] = s[HEAD_DIM:, HEAD_DIM:]

    def phase_c(c, carry):
        q = load_qk(q_ref, c, 1.0)
        k = load_qk(k_ref, c, HEAD_DIM ** -0.5)
        v = v_ref[rows_of(c), :].astype(BF16)
        qb = q.astype(BF16)
        k0 = jnp.where(lo, k, 0.0).astype(BF16)
        k1 = jnp.where(lo, 0.0, k).astype(BF16)
        p0 = (_dot_nt(qb, k0) * cst_ref[0]).astype(BF16)
        p1 = (_dot_nt(qb, k1) * cst_ref[1]).astype(BF16)
        inner = jnp.where(lo, _dot(p0, v), _dot(p1, v))
        qx = jnp.concatenate([(q * cst_ref[4]).astype(BF16), (q * cst_ref[5]).astype(BF16)], axis=1)
        o = inner + _dot(qx, uu_ref[c].astype(BF16))
        mu = _seg_mean(o, lo)
        d = o - mu
        var = _seg_mean(d * d, lo)
        y = d * lax.rsqrt(var + EPS) * gn_ref[...]
        o_ref[rows_of(c), :] = (y * _silu(rg_ref[rows_of(c), :])).astype(BF16)
        return carry

    lax.fori_loop(0, nc, phase_c, 0)


def _retention(z, dec, gn, seq, latent, cos=None, sin=None, s0=None, layer=None):
    t = z.shape[0]
    b = t // seq
    nc = seq // CHUNK
    npair = H_RET // 2

    def zcol(base):
        return pl.BlockSpec((seq, LANES), lambda bi, hp: (bi, base // LANES + hp))

    in_specs = [
        pl.BlockSpec(memory_space=pltpu.SMEM),
        zcol(COL_RQ), zcol(COL_RK), zcol(COL_RV), zcol(COL_RG),
        pl.BlockSpec((1, LANES), lambda bi, hp: (0, hp)),
    ]
    args = [dec, z, z, z, z, gn.reshape(1, RET_W)]
    out_shape = [jax.ShapeDtypeStruct((t, RET_W), BF16)]
    out_specs = [pl.BlockSpec((seq, LANES), lambda bi, hp: (bi, hp))]
    if latent:
        in_specs += [
            pl.BlockSpec((seq, LANES), lambda bi, hp: (0, 0)),
            pl.BlockSpec((seq, LANES), lambda bi, hp: (0, 0)),
            pl.BlockSpec((1, 1, 2, 2, HEAD_DIM, HEAD_DIM), lambda bi, hp: (bi, layer, 0, hp, 0, 0)),
        ]
        args += [cos, sin, s0]
    else:
        out_shape.append(jax.ShapeDtypeStruct((b, 2, H_RET, HEAD_DIM, HEAD_DIM), F32))
        out_specs.append(pl.BlockSpec((1, 2, 2, HEAD_DIM, HEAD_DIM), lambda bi, hp: (bi, 0, hp, 0, 0)))
    res = pl.pallas_call(
        functools.partial(_ret_kernel, nc=nc, latent=latent),
        grid=(b, npair),
        in_specs=in_specs,
        out_specs=out_specs,
        out_shape=out_shape,
        scratch_shapes=[pltpu.VMEM((6, CHUNK, LANES), F32), pltpu.VMEM((nc, 2 * CHUNK, LANES), F32)],
        compiler_params=_cparams(("parallel", "arbitrary")),
        name="retention_latent" if latent else "retention_ctx",
    )(*args)
    return res if not latent else res[0]


def _att_kernel(*refs, tq, seq, latent):
    if latent:
        (q_ref, ak_ref, av_ref, ck_ref, cv_ref, gq_ref, gk_ref, cosq_ref, sinq_ref, cosk_ref, sink_ref,
         o_ref, k0_ref, k1_ref, vv_ref) = refs
        off = PAST_LEN
    else:
        (q_ref, ak_ref, av_ref, gq_ref, gk_ref,
         o_ref, knew_ref, vnew_ref, k0_ref, k1_ref, vv_ref) = refs
        off = 0

    @pl.when(pl.program_id(1) == 0)
    def _():
        lo = _lo_mask((seq, LANES))
        ak = ak_ref[...]
        kn = ak * lax.rsqrt(_seg_mean(ak * ak, lo) + EPS) * gk_ref[...]
        av = av_ref[...]
        if latent:
            kn = _rope(kn, cosk_ref[...], sink_ref[...])
            lo_c = _lo_mask((PAST_LEN, LANES))
            ck = ck_ref[0, 0]
            k0_ref[0:off, :] = jnp.where(lo_c, ck, 0.0).astype(BF16)
            k1_ref[0:off, :] = jnp.where(lo_c, 0.0, ck).astype(BF16)
            vv_ref[0:off, :] = cv_ref[0, 0].astype(BF16)
        else:
            knew_ref[...] = kn
            vnew_ref[...] = av
        k0_ref[off:off + seq, :] = jnp.where(lo, kn, 0.0).astype(BF16)
        k1_ref[off:off + seq, :] = jnp.where(lo, 0.0, kn).astype(BF16)
        vv_ref[off:off + seq, :] = av.astype(BF16)

    lo_q = _lo_mask((tq, LANES))
    nrep = ATT_Q_W // LANES
    chunks = []
    for r in range(nrep):
        x = q_ref[:, r * LANES:(r + 1) * LANES]
        xn = x * lax.rsqrt(_seg_mean(x * x, lo_q) + EPS) * gq_ref[...]
        if latent:
            xn = _rope(xn, cosq_ref[...], sinq_ref[...])
        chunks.append((xn * (HEAD_DIM ** -0.5)).astype(BF16))
    qs = jnp.concatenate(chunks, axis=0)
    v = vv_ref[...]
    outs = []
    for kref in (k0_ref, k1_ref):
        s = _dot_nt(qs, kref[...])
        m = jnp.max(s, axis=-1, keepdims=True)
        p = jnp.exp(s - m)
        l = jnp.sum(p, axis=-1, keepdims=True)
        outs.append(_dot(p.astype(BF16), v) / l)
    o = jnp.where(_lo_mask((nrep * tq, LANES)), outs[0], outs[1])
    for r in range(nrep):
        o_ref[:, r * LANES:(r + 1) * LANES] = o[r * tq:(r + 1) * tq, :].astype(BF16)


def _attention(z, gq, gk, seq, latent, cos=None, sin=None, cache_k=None, cache_v=None, layer=None):
    t = z.shape[0]
    b = t // seq
    tq = 128 if latent else seq
    nq = seq // tq
    sk = seq + (PAST_LEN if latent else 0)
    kv_spec = lambda base: pl.BlockSpec((seq, LANES), lambda bi, qi: (bi, base // LANES))
    vec_spec = pl.BlockSpec((1, LANES), lambda bi, qi: (0, 0))
    in_specs = [
        pl.BlockSpec((tq, ATT_Q_W), lambda bi, qi: (bi * nq + qi, COL_AQ // ATT_Q_W)),
        kv_spec(COL_AK), kv_spec(COL_AV),
    ]
    args = [z, z, z]
    if latent:
        cache_spec = pl.BlockSpec((1, 1, PAST_LEN, LANES), lambda bi, qi: (bi, layer, 0, 0))
        in_specs += [cache_spec, cache_spec]
        args += [cache_k, cache_v]
    in_specs += [vec_spec, vec_spec]
    args += [gq, gk]
    out_shape = [jax.ShapeDtypeStruct((t, ATT_Q_W), BF16)]
    out_specs = [pl.BlockSpec((tq, ATT_Q_W), lambda bi, qi: (bi * nq + qi, 0))]
    if latent:
        in_specs += [
            pl.BlockSpec((tq, LANES), lambda bi, qi: (qi, 0)),
            pl.BlockSpec((tq, LANES), lambda bi, qi: (qi, 0)),
            pl.BlockSpec((seq, LANES), lambda bi, qi: (0, 0)),
            pl.BlockSpec((seq, LANES), lambda bi, qi: (0, 0)),
        ]
        args += [cos, sin, cos, sin]
    else:
        new_spec = pl.BlockSpec((seq, LANES), lambda bi, qi: (bi, 0))
        out_shape += [jax.ShapeDtypeStruct((t, LANES), F32)] * 2
        out_specs += [new_spec, new_spec]
    res = pl.pallas_call(
        functools.partial(_att_kernel, tq=tq, seq=seq, latent=latent),
        grid=(b, nq),
        in_specs=in_specs,
        out_specs=out_specs,
        out_shape=out_shape,
        scratch_shapes=[pltpu.VMEM((sk, LANES), BF16)] * 3,
        compiler_params=_cparams(("parallel", "arbitrary")),
        name="attention_latent" if latent else "attention_ctx",
    )(*args)
    return res if not latent else res[0]


def _merge_kernel(x_ref, gr_ref, ga_ref, gc_ref, cu_ref, cv_ref, ro_ref, ao_ref,
                  wr_ref, wa_ref, wc_ref, wo_ref, cmn_ref, ws_ref, bs_ref, g1_ref, o_ref, *, tm):
    u = jax.nn.gelu(cu_ref[...], approximate=True)
    v = jax.nn.gelu(cv_ref[...], approximate=True)
    vn = v * lax.rsqrt(jnp.mean(v * v, axis=-1, keepdims=True) + EPS) * cmn_ref[...]
    vb = vn.astype(BF16)
    rows = []
    for n in range(tm // CHUNK):
        cols = []
        for g in range(CM_GROUPS):
            vg = vb[n * CHUNK:(n + 1) * CHUNK, g * LANES:(g + 1) * LANES]
            cols.append(_dot(ws_ref[g], vg) + bs_ref[:, g:g + 1])
        rows.append(jnp.concatenate(cols, axis=1))
    cm = (u * jnp.concatenate(rows, axis=0)).astype(BF16)
    merged = (jax.nn.sigmoid(gr_ref[...]) * _dot(ro_ref[...], wr_ref[...])
              + jax.nn.sigmoid(ga_ref[...]) * _dot(ao_ref[...], wa_ref[...])
              + jax.nn.sigmoid(gc_ref[...]) * _dot(cm, wc_ref[...]))
    y = _dot(merged.astype(BF16), wo_ref[...])
    o_ref[...] = x_ref[...] + g1_ref[0] * y


def _merge(x, z, ret_o, att_o, wr, wa, wc, wo, cmn, ws, bs_t, g1, seq, latent):
    t = x.shape[0]
    tm = 256
    mrow = _mod_row_fn(tm, seq, latent)
    row_spec = lambda w, cb: pl.BlockSpec((tm, w), lambda i: (i, cb))
    full = lambda shape: pl.BlockSpec(shape, lambda i: (0,) * len(shape))
    return pl.pallas_call(
        functools.partial(_merge_kernel, tm=tm),
        grid=(t // tm,),
        in_specs=[
            row_spec(D_MODEL, 0),
            row_spec(D_MODEL, COL_GR // D_MODEL), row_spec(D_MODEL, COL_GA // D_MODEL),
            row_spec(D_MODEL, COL_GC // D_MODEL),
            row_spec(CM_W, COL_CU // CM_W), row_spec(CM_W, COL_CV // CM_W),
            row_spec(RET_W, 0), row_spec(ATT_Q_W, 0),
            full((RET_W, D_MODEL)), full((ATT_Q_W, D_MODEL)), full((CM_W, D_MODEL)),
            full((D_MODEL, D_MODEL)),
            full((1, CM_W)), full((CM_GROUPS, CHUNK, CHUNK)), full((CHUNK, CM_GROUPS)),
            pl.BlockSpec((1, 1, D_MODEL), lambda i: (mrow(i), 0, 0)),
        ],
        out_specs=row_spec(D_MODEL, 0),
        out_shape=jax.ShapeDtypeStruct((t, D_MODEL), F32),
        compiler_params=_cparams(("parallel",)),
        name="merge",
    )(x, z, z, z, z, z, ret_o, att_o, wr, wa, wc, wo, cmn, ws, bs_t, g1)


def _ffn_kernel(x_ref, g_ref, sc_ref, sh_ref, w1_ref, w2_ref, g2_ref, gf_ref, o_ref, h_ref, acc_ref, *, final):
    k = pl.program_id(1)

    @pl.when(k == 0)
    def _():
        h_ref[...] = _norm_mod(x_ref[...], g_ref[...], sc_ref[0], sh_ref[0]).astype(BF16)
        acc_ref[...] = jnp.zeros_like(acc_ref)

    a = jnp.square(jnp.maximum(_dot(h_ref[...], w1_ref[...]), 0.0))
    acc_ref[...] += _dot(a.astype(BF16), w2_ref[...])

    @pl.when(k == pl.num_programs(1) - 1)
    def _():
        xn = x_ref[...] + g2_ref[0] * acc_ref[...]
        if final:
            xn = xn * lax.rsqrt(jnp.mean(xn * xn, axis=-1, keepdims=True) + EPS) * gf_ref[...]
        o_ref[...] = xn


def _ffn(x, g, sc, sh, w1, w2, g2, gfin, seq, latent, final):
    t = x.shape[0]
    tm, tf = 512, 512
    mrow = _mod_row_fn(tm, seq, latent)
    mod_spec = pl.BlockSpec((1, 1, D_MODEL), lambda i, k: (mrow(i), 0, 0))
    vec_spec = pl.BlockSpec((1, D_MODEL), lambda i, k: (0, 0))
    return pl.pallas_call(
        functools.partial(_ffn_kernel, final=final),
        grid=(t // tm, D_FF // tf),
        in_specs=[
            pl.BlockSpec((tm, D_MODEL), lambda i, k: (i, 0)),
            vec_spec, mod_spec, mod_spec,
            pl.BlockSpec((D_MODEL, tf), lambda i, k: (0, k)),
            pl.BlockSpec((tf, D_MODEL), lambda i, k: (k, 0)),
            mod_spec, vec_spec,
        ],
        out_specs=pl.BlockSpec((tm, D_MODEL), lambda i, k: (i, 0)),
        out_shape=jax.ShapeDtypeStruct((t, D_MODEL), F32),
        scratch_shapes=[pltpu.VMEM((tm, D_MODEL), BF16), pltpu.VMEM((tm, D_MODEL), F32)],
        compiler_params=_cparams(("parallel", "arbitrary")),
        name="ffn_final" if final else "ffn",
    )(x, g, sc, sh, w1, w2, g2, gfin)


def _rope_tables(length):
    rows = length // GRID_W
    row = jnp.repeat(jnp.arange(rows, dtype=F32), GRID_W)
    col = jnp.tile(jnp.arange(GRID_W, dtype=F32), rows)
    half = HEAD_DIM // 2
    inv = 1.0 / (ROPE_BASE ** (jnp.arange(0, half, 2, dtype=F32) / half))
    ang = jnp.concatenate([row[:, None] * inv[None, :], col[:, None] * inv[None, :]], axis=-1)
    cos, sin = jnp.cos(ang), jnp.sin(ang)
    cos2 = jnp.repeat(cos, 2, axis=-1)
    sin2 = jnp.stack([-sin, sin], axis=-1).reshape(length, HEAD_DIM)
    return jnp.tile(cos2, (1, 2)), jnp.tile(sin2, (1, 2))


def _permute_w_in(w_in):
    o = np.cumsum((0, 512, 512, 512, 512, 512, 128, 128, 512, 512, 1024, 1024, 1024))
    seg = lambda i: w_in[:, :, int(o[i]):int(o[i + 1])]
    rq, rk, rv, rg, aq, ak, av, cu, cv, gr, ga, gc = (seg(i) for i in range(12))
    aq = aq.reshape(DEPTH, D_MODEL, H_Q, HEAD_DIM)[:, :, np.array(Q_HEAD_ORDER)].reshape(DEPTH, D_MODEL, ATT_Q_W)
    return jnp.concatenate([gr, ga, gc, rq, rk, rv, rg, aq, cu, cv, ak, av], axis=-1).astype(BF16)


def kernel(x_prompt, x_sample, cache_k, cache_v, state_ret, c, c_ctx, w_mod, b_mod, g_norm1, g_norm2, w_in, g_q, g_k, ret_decay, ret_gn, cm_norm, cm_ws, cm_bs, w_br_ret, w_br_att, w_br_cm, w_out, w_ff1, w_ff2, g_final):
    batch, dec_batch = x_prompt.shape[0], x_sample.shape[0]

    cond = jnp.concatenate(
        [c_ctx[None, :], c, jnp.zeros((N_MOD_ROWS - 1 - dec_batch, D_MODEL), F32)], axis=0)
    mod = _modulation(cond, w_mod, b_mod)
    mod = mod.reshape(DEPTH, N_MOD_ROWS, 6, D_MODEL).transpose(0, 2, 1, 3).reshape(DEPTH, 6, N_MOD_ROWS, 1, D_MODEL)

    w_in_p = _permute_w_in(w_in)
    w_br_ret_b = w_br_ret.astype(BF16)
    w_br_att_b = w_br_att.reshape(DEPTH, H_Q, HEAD_DIM, D_MODEL)[:, np.array(Q_HEAD_ORDER)].reshape(
        DEPTH, ATT_Q_W, D_MODEL).astype(BF16)
    w_br_cm_b = w_br_cm.astype(BF16)
    w_out_b = w_out.astype(BF16)
    w_ff1_b = w_ff1.astype(BF16)
    w_ff2_b = w_ff2.astype(BF16)
    cm_ws_b = cm_ws.astype(BF16)
    cm_bs_t = jnp.swapaxes(cm_bs, 1, 2)
    g_q2 = jnp.tile(g_q, (1, 2)).reshape(DEPTH, 1, LANES)
    g_k2 = jnp.tile(g_k, (1, 2)).reshape(DEPTH, 1, LANES)
    cos, sin = _rope_tables(DEC_SEQ)
    cache_k2 = cache_k.reshape(dec_batch, DEPTH, PAST_LEN, LANES)
    cache_v2 = cache_v.reshape(dec_batch, DEPTH, PAST_LEN, LANES)
    g_fin = g_final.reshape(1, D_MODEL)

    xp = x_prompt.reshape(batch * SEQ, D_MODEL)
    xs = x_sample.reshape(dec_batch * DEC_SEQ, D_MODEL)
    ks_list, vs_list, ss_list = [], [], []
    for l in range(DEPTH):
        sh1, sc1, g1, sh2, sc2, g2 = (mod[l, i] for i in range(6))
        gn1 = g_norm1[l].reshape(1, D_MODEL)
        gn2 = g_norm2[l].reshape(1, D_MODEL)
        final = l == DEPTH - 1
        for latent in (False, True):
            x = xs if latent else xp
            seq = DEC_SEQ if latent else SEQ
            z = _in_proj(x, gn1, sc1, sh1, w_in_p[l], seq, latent)
            if latent:
                ret_o = _retention(z, ret_decay[l], ret_gn[l], seq, True, cos, sin, state_ret, l)
                att_o = _attention(z, g_q2[l], g_k2[l], seq, True, cos, sin, cache_k2, cache_v2, l)
            else:
                ret_o, s_fin = _retention(z, ret_decay[l], ret_gn[l], seq, False)
                att_o, k_new, v_new = _attention(z, g_q2[l], g_k2[l], seq, False)
                ks_list.append(k_new.reshape(batch, SEQ, H_KV, HEAD_DIM))
                vs_list.append(v_new.reshape(batch, SEQ, H_KV, HEAD_DIM))
                ss_list.append(s_fin)
            x = _merge(x, z, ret_o, att_o, w_br_ret_b[l], w_br_att_b[l], w_br_cm_b[l], w_out_b[l],
                       cm_norm[l].reshape(1, CM_W), cm_ws_b[l], cm_bs_t[l], g1, seq, latent)
            x = _ffn(x, gn2, sc2, sh2, w_ff1_b[l], w_ff2_b[l], g2, g_fin, seq, latent, final)
            if latent:
                xs = x
            else:
                xp = x
    y_prompt = xp.reshape(batch, SEQ, D_MODEL)
    y_sample = xs.reshape(dec_batch, DEC_SEQ, D_MODEL)
    return (y_prompt, y_sample, jnp.stack(ks_list, axis=1), jnp.stack(vs_list, axis=1),
            jnp.stack(ss_list, axis=1))
```

```python
import functools

import numpy as np
import jax
import jax.numpy as jnp
from jax import lax
from jax.experimental import pallas as pl
from jax.experimental.pallas import tpu as pltpu

D_MODEL = 1024
DEPTH = 2
SEQ = 256
DEC_SEQ = 2048
PAST_LEN = 256
GRID_W = 64
HEAD_DIM = 64
H_RET = 8
H_Q = 8
H_KV = 2
CHUNK = 128
CM_GROUPS = 4
RET_W = 512
ATT_Q_W = 512
ATT_KV_W = 128
CM_W = 512
D_FF = 4096
ROPE_BASE = 10000.0
EPS = 1e-6
IN_W = 6912
N_MOD_ROWS = 16

LANES = 128
MXU_N = 256
RET_UNROLL = 4
VMEM_LIMIT = 48 * 1024 * 1024

F32 = jnp.float32
BF16 = jnp.bfloat16

COL_GR, COL_GA, COL_GC = 0, 1024, 2048
COL_RQ, COL_RK, COL_RV, COL_RG = 3072, 3584, 4096, 4608
COL_AQ, COL_CU, COL_CV = 5120, 5632, 6144
COL_AK, COL_AV = 6656, 6784
Q_HEAD_ORDER = (0, 4, 1, 5, 2, 6, 3, 7)


def _cparams(sem):
    return pltpu.CompilerParams(dimension_semantics=sem, vmem_limit_bytes=VMEM_LIMIT)


def _dot(a, b):
    return jnp.dot(a, b, preferred_element_type=F32)


def _dot_nt(a, b):
    return lax.dot_general(a, b, (((1,), (1,)), ((), ())), preferred_element_type=F32)


def _dot_tn(a, b):
    return lax.dot_general(a, b, (((0,), (0,)), ((), ())), preferred_element_type=F32)


def _lo_mask(shape):
    return lax.broadcasted_iota(jnp.int32, shape, len(shape) - 1) < HEAD_DIM


def _seg_mean(x, lo):
    s_lo = jnp.sum(jnp.where(lo, x, 0.0), axis=-1, keepdims=True)
    s_hi = jnp.sum(jnp.where(lo, 0.0, x), axis=-1, keepdims=True)
    return jnp.where(lo, s_lo, s_hi) * (1.0 / HEAD_DIM)


def _rope(x, cos, sin_signed):
    nxt = pltpu.roll(x, LANES - 1, 1)
    prv = pltpu.roll(x, 1, 1)
    even = (lax.broadcasted_iota(jnp.int32, x.shape, 1) & 1) == 0
    return x * cos + jnp.where(even, nxt, prv) * sin_signed


def _silu(x):
    return x * jax.nn.sigmoid(x)


def _norm_mod(x, g, sc, sh):
    ms = jnp.mean(x * x, axis=-1, keepdims=True)
    y = x * lax.rsqrt(ms + EPS) * g
    return y * (1.0 + sc) + sh


def _mod_kernel(c_ref, w_ref, b_ref, o_ref):
    s = _silu(c_ref[...])
    o_ref[0] = jnp.dot(s, w_ref[0], preferred_element_type=F32,
                       precision=lax.Precision.HIGHEST) + b_ref[0]


def _modulation(cond, w_mod, b_mod):
    tn = 1536
    n = 6 * D_MODEL
    return pl.pallas_call(
        _mod_kernel,
        grid=(DEPTH, n // tn),
        in_specs=[
            pl.BlockSpec((N_MOD_ROWS, D_MODEL), lambda l, j: (0, 0)),
            pl.BlockSpec((1, D_MODEL, tn), lambda l, j: (l, 0, j)),
            pl.BlockSpec((1, 1, tn), lambda l, j: (l, 0, j)),
        ],
        out_specs=pl.BlockSpec((1, N_MOD_ROWS, tn), lambda l, j: (l, 0, j)),
        out_shape=jax.ShapeDtypeStruct((DEPTH, N_MOD_ROWS, n), F32),
        compiler_params=_cparams(("arbitrary", "arbitrary")),
        name="modulation",
    )(cond, w_mod, b_mod.reshape(DEPTH, 1, n))


def _mod_row_fn(tm, seq, latent):
    if latent:
        return lambda i: 1 + (i * tm) // seq
    return lambda i: 0


def _proj_kernel(x_ref, g_ref, sc_ref, sh_ref, w_ref, o_ref, h_ref):
    @pl.when(pl.program_id(1) == 0)
    def _():
        h_ref[...] = _norm_mod(x_ref[...], g_ref[...], sc_ref[0], sh_ref[0]).astype(BF16)

    o_ref[...] = _dot(h_ref[...], w_ref[...]).astype(BF16)


def _in_proj(x, g, sc, sh, w, seq, latent):
    t = x.shape[0]
    tm, tn = 1024, 2304
    mrow = _mod_row_fn(tm, seq, latent)
    return pl.pallas_call(
        _proj_kernel,
        grid=(t // tm, IN_W // tn),
        in_specs=[
            pl.BlockSpec((tm, D_MODEL), lambda i, j: (i, 0)),
            pl.BlockSpec((1, D_MODEL), lambda i, j: (0, 0)),
            pl.BlockSpec((1, 1, D_MODEL), lambda i, j: (mrow(i), 0, 0)),
            pl.BlockSpec((1, 1, D_MODEL), lambda i, j: (mrow(i), 0, 0)),
            pl.BlockSpec((D_MODEL, tn), lambda i, j: (0, j)),
        ],
        out_specs=pl.BlockSpec((tm, tn), lambda i, j: (i, j)),
        out_shape=jax.ShapeDtypeStruct((t, IN_W), BF16),
        scratch_shapes=[pltpu.VMEM((tm, D_MODEL), BF16)],
        compiler_params=_cparams(("parallel", "arbitrary")),
        name="in_proj",
    )(x, g, sc, sh, w)


def _ret_kernel(*refs, nc, latent):
    if latent:
        (dec_ref, q_ref, k_ref, v_ref, rg_ref, gn_ref, cos_ref, sin_ref, s0_ref,
         o_ref, cst_ref, uu_ref) = refs
    else:
        (dec_ref, q_ref, k_ref, v_ref, rg_ref, gn_ref,
         o_ref, sfin_ref, cst_ref, uu_ref) = refs
    hp = pl.program_id(1)
    sq = (CHUNK, LANES)
    lane = lax.broadcasted_iota(jnp.int32, sq, 1)
    row = lax.broadcasted_iota(jnp.int32, sq, 0)
    lo = lane < HEAD_DIM
    blockdiag = (row < HEAD_DIM) == lo
    rowf = row.astype(F32)
    rel = (row - lane).astype(F32)

    def log_decay(direction, head):
        d = jnp.full(sq, dec_ref[direction, head], F32)
        return -(jnp.maximum(d, 0.0) + jnp.log1p(jnp.exp(-jnp.abs(d))))

    ldf0, ldf1 = log_decay(0, 2 * hp), log_decay(0, 2 * hp + 1)
    ldb0, ldb1 = log_decay(1, 2 * hp), log_decay(1, 2 * hp + 1)
    ldf = jnp.where(lo, ldf0, ldf1)
    ldb = jnp.where(lo, ldb0, ldb1)

    def decay_mask(lf, lb):
        fwd = jnp.exp(lf * jnp.maximum(rel, 0.0))
        bwd = jnp.exp(lb * jnp.maximum(-rel, 0.0))
        return jnp.where(rel > 0, fwd, jnp.where(rel < 0, bwd, 2.0))

    cst_ref[0] = decay_mask(ldf0, ldb0)
    cst_ref[1] = decay_mask(ldf1, ldb1)
    cst_ref[2] = jnp.exp(ldf * (CHUNK - 1.0 - rowf))
    cst_ref[3] = jnp.exp(ldb * rowf)
    cst_ref[4] = jnp.exp(ldf * (rowf + 1.0))
    cst_ref[5] = jnp.exp(ldb * (CHUNK - rowf))
    gf = jnp.exp(ldf * float(CHUNK))
    gb = jnp.exp(ldb * float(CHUNK))

    def rows_of(c):
        return pl.ds(pl.multiple_of(c * CHUNK, CHUNK), CHUNK)

    def load_qk(ref, c, scale):
        x = ref[rows_of(c), :].astype(F32)
        if scale != 1.0:
            x = x * scale
        if latent:
            x = _rope(x, cos_ref[rows_of(c), :], sin_ref[rows_of(c), :])
        return x

    def phase_a(c, carry):
        k = load_qk(k_ref, c, HEAD_DIM ** -0.5)
        v = v_ref[rows_of(c), :]
        kz = jnp.concatenate([(k * cst_ref[2]).astype(BF16), (k * cst_ref[3]).astype(BF16)], axis=1)
        uu_ref[c] = _dot_tn(kz, v)
        return carry

    unroll = min(nc, RET_UNROLL)
    lax.fori_loop(0, nc, phase_a, 0, unroll=unroll)

    if latent:
        zeros = jnp.zeros((HEAD_DIM, HEAD_DIM), F32)

        def embed(direction):
            top = jnp.concatenate([s0_ref[0, 0, direction, 0], zeros], axis=1)
            bot = jnp.concatenate([zeros, s0_ref[0, 0, direction, 1]], axis=1)
            return jnp.concatenate([top, bot], axis=0)

        sf0, sb0 = embed(0), embed(1)
    else:
        sf0 = sb0 = jnp.zeros(sq, F32)

    def scan_f(c, s):
        u = uu_ref[c, 0:CHUNK, :]
        uu_ref[c, 0:CHUNK, :] = s
        return gf * s + jnp.where(blockdiag, u, 0.0)

    def scan_b(i, s):
        c = nc - 1 - i
        u = uu_ref[c, CHUNK:2 * CHUNK, :]
        uu_ref[c, CHUNK:2 * CHUNK, :] = s
        return gb * s + jnp.where(blockdiag, u, 0.0)

    sf = lax.fori_loop(0, nc, scan_f, sf0, unroll=True)
    sb = lax.fori_loop(0, nc, scan_b, sb0, unroll=True)
    if not latent:
        for d, s in ((0, sf), (1, sb)):
            sfin_ref[0, d, 0] = s[0:HEAD_DIM, 0:HEAD_DIM]
            sfin_ref[0, d, 1] = s[HEAD_DIM:, HEAD_DIM:]

    def phase_c(c, carry):
        q = load_qk(q_ref, c, 1.0)
        k = load_qk(k_ref, c, HEAD_DIM ** -0.5)
        v = v_ref[rows_of(c), :]
        qb = q.astype(BF16)
        k0 = jnp.where(lo, k, 0.0).astype(BF16)
        k1 = jnp.where(lo, 0.0, k).astype(BF16)
        p0 = (_dot_nt(qb, k0) * cst_ref[0]).astype(BF16)
        p1 = (_dot_nt(qb, k1) * cst_ref[1]).astype(BF16)
        inner = jnp.where(lo, _dot(p0, v), _dot(p1, v))
        qx = jnp.concatenate([(q * cst_ref[4]).astype(BF16), (q * cst_ref[5]).astype(BF16)], axis=1)
        o = inner + _dot(qx, uu_ref[c].astype(BF16))
        mu = _seg_mean(o, lo)
        d = o - mu
        var = _seg_mean(d * d, lo)
        y = d * lax.rsqrt(var + EPS) * gn_ref[...]
        o_ref[rows_of(c), :] = (y * _silu(rg_ref[rows_of(c), :].astype(F32))).astype(BF16)
        return carry

    lax.fori_loop(0, nc, phase_c, 0, unroll=unroll)


def _retention(z, dec, gn, seq, latent, cos=None, sin=None, s0=None, layer=None):
    t = z.shape[0]
    b = t // seq
    nc = seq // CHUNK
    npair = H_RET // 2

    def zcol(base):
        return pl.BlockSpec((seq, LANES), lambda bi, hp: (bi, base // LANES + hp))

    in_specs = [
        pl.BlockSpec(memory_space=pltpu.SMEM),
        zcol(COL_RQ), zcol(COL_RK), zcol(COL_RV), zcol(COL_RG),
        pl.BlockSpec((1, LANES), lambda bi, hp: (0, hp)),
    ]
    args = [dec, z, z, z, z, gn.reshape(1, RET_W)]
    out_shape = [jax.ShapeDtypeStruct((t, RET_W), BF16)]
    out_specs = [pl.BlockSpec((seq, LANES), lambda bi, hp: (bi, hp))]
    if latent:
        in_specs += [
            pl.BlockSpec((seq, LANES), lambda bi, hp: (0, 0)),
            pl.BlockSpec((seq, LANES), lambda bi, hp: (0, 0)),
            pl.BlockSpec((1, 1, 2, 2, HEAD_DIM, HEAD_DIM), lambda bi, hp: (bi, layer, 0, hp, 0, 0)),
        ]
        args += [cos, sin, s0]
    else:
        out_shape.append(jax.ShapeDtypeStruct((b, 2, H_RET, HEAD_DIM, HEAD_DIM), F32))
        out_specs.append(pl.BlockSpec((1, 2, 2, HEAD_DIM, HEAD_DIM), lambda bi, hp: (bi, 0, hp, 0, 0)))
    res = pl.pallas_call(
        functools.partial(_ret_kernel, nc=nc, latent=latent),
        grid=(b, npair),
        in_specs=in_specs,
        out_specs=out_specs,
        out_shape=out_shape,
        scratch_shapes=[pltpu.VMEM((6, CHUNK, LANES), F32), pltpu.VMEM((nc, 2 * CHUNK, LANES), F32)],
        compiler_params=_cparams(("parallel", "arbitrary")),
        name="retention_latent" if latent else "retention_ctx",
    )(*args)
    return res if not latent else res[0]


def _att_kernel(*refs, tq, seq, latent):
    if latent:
        (q_ref, ak_ref, av_ref, ck_ref, cv_ref, gq_ref, gk_ref, cosq_ref, sinq_ref, cosk_ref, sink_ref,
         o_ref, k0_ref, k1_ref, vv_ref, s_ref, p_ref) = refs
        off = PAST_LEN
    else:
        (q_ref, ak_ref, av_ref, gq_ref, gk_ref,
         o_ref, knew_ref, vnew_ref, k0_ref, k1_ref, vv_ref, s_ref, p_ref) = refs
        off = 0

    @pl.when(pl.program_id(1) == 0)
    def _():
        lo = _lo_mask((seq, LANES))
        ak = ak_ref[...].astype(F32)
        kn = ak * lax.rsqrt(_seg_mean(ak * ak, lo) + EPS) * gk_ref[...]
        av = av_ref[...]
        if latent:
            kn = _rope(kn, cosk_ref[...], sink_ref[...])
            lo_c = _lo_mask((PAST_LEN, LANES))
            ck = ck_ref[0, 0]
            k0_ref[0:off, :] = jnp.where(lo_c, ck, 0.0).astype(BF16)
            k1_ref[0:off, :] = jnp.where(lo_c, 0.0, ck).astype(BF16)
            vv_ref[0:off, :] = cv_ref[0, 0].astype(BF16)
        else:
            knew_ref[...] = kn
            vnew_ref[...] = av.astype(F32)
        k0_ref[off:off + seq, :] = jnp.where(lo, kn, 0.0).astype(BF16)
        k1_ref[off:off + seq, :] = jnp.where(lo, 0.0, kn).astype(BF16)
        vv_ref[off:off + seq, :] = av

    sk = off + seq
    lo = _lo_mask((CHUNK, LANES))
    unit = 0
    for sub in range(tq // CHUNK):
        rows = slice(sub * CHUNK, (sub + 1) * CHUNK)
        for r in range(ATT_Q_W // LANES):
            x = q_ref[rows, r * LANES:(r + 1) * LANES].astype(F32)
            xn = x * lax.rsqrt(_seg_mean(x * x, lo) + EPS) * gq_ref[...]
            if latent:
                xn = _rope(xn, cosq_ref[rows, :], sinq_ref[rows, :])
            qb = (xn * (HEAD_DIM ** -0.5)).astype(BF16)
            outs = []
            for kref in (k0_ref, k1_ref):
                sbuf, pbuf = s_ref.at[unit % 2], p_ref.at[unit % 2]
                unit += 1
                m_l = None
                for kt in range(sk // MXU_N):
                    cols = slice(kt * MXU_N, (kt + 1) * MXU_N)
                    s_t = _dot_nt(qb, kref[cols, :])
                    sbuf[:, cols] = s_t
                    t = jnp.maximum(s_t[:, :LANES], s_t[:, LANES:])
                    m_l = t if m_l is None else jnp.maximum(m_l, t)
                m = jnp.broadcast_to(jnp.max(m_l, axis=-1, keepdims=True), (CHUNK, LANES))
                l_l = jnp.zeros((CHUNK, LANES), F32)
                for kt in range(sk // LANES):
                    cols = slice(kt * LANES, (kt + 1) * LANES)
                    p_t = jnp.exp(sbuf[:, cols] - m)
                    l_l = l_l + p_t
                    pbuf[:, cols] = p_t.astype(BF16)
                l = jnp.sum(l_l, axis=-1, keepdims=True)
                outs.append(_dot(pbuf[...], vv_ref[...]) / l)
            o_ref[rows, r * LANES:(r + 1) * LANES] = jnp.where(lo, outs[0], outs[1]).astype(BF16)


def _attention(z, gq, gk, seq, latent, cos=None, sin=None, cache_k=None, cache_v=None, layer=None):
    t = z.shape[0]
    b = t // seq
    tq = 128 if latent else seq
    nq = seq // tq
    sk = seq + (PAST_LEN if latent else 0)
    kv_spec = lambda base: pl.BlockSpec((seq, LANES), lambda bi, qi: (bi, base // LANES))
    vec_spec = pl.BlockSpec((1, LANES), lambda bi, qi: (0, 0))
    in_specs = [
        pl.BlockSpec((tq, ATT_Q_W), lambda bi, qi: (bi * nq + qi, COL_AQ // ATT_Q_W)),
        kv_spec(COL_AK), kv_spec(COL_AV),
    ]
    args = [z, z, z]
    if latent:
        cache_spec = pl.BlockSpec((1, 1, PAST_LEN, LANES), lambda bi, qi: (bi, layer, 0, 0))
        in_specs += [cache_spec, cache_spec]
        args += [cache_k, cache_v]
    in_specs += [vec_spec, vec_spec]
    args += [gq, gk]
    out_shape = [jax.ShapeDtypeStruct((t, ATT_Q_W), BF16)]
    out_specs = [pl.BlockSpec((tq, ATT_Q_W), lambda bi, qi: (bi * nq + qi, 0))]
    if latent:
        in_specs += [
            pl.BlockSpec((tq, LANES), lambda bi, qi: (qi, 0)),
            pl.BlockSpec((tq, LANES), lambda bi, qi: (qi, 0)),
            pl.BlockSpec((seq, LANES), lambda bi, qi: (0, 0)),
            pl.BlockSpec((seq, LANES), lambda bi, qi: (0, 0)),
        ]
        args += [cos, sin, cos, sin]
    else:
        new_spec = pl.BlockSpec((seq, LANES), lambda bi, qi: (bi, 0))
        out_shape += [jax.ShapeDtypeStruct((t, LANES), F32)] * 2
        out_specs += [new_spec, new_spec]
    res = pl.pallas_call(
        functools.partial(_att_kernel, tq=tq, seq=seq, latent=latent),
        grid=(b, nq),
        in_specs=in_specs,
        out_specs=out_specs,
        out_shape=out_shape,
        scratch_shapes=[pltpu.VMEM((sk, LANES), BF16)] * 3 + [
            pltpu.VMEM((2, CHUNK, sk), F32), pltpu.VMEM((2, CHUNK, sk), BF16)],
        compiler_params=_cparams(("parallel", "arbitrary")),
        name="attention_latent" if latent else "attention_ctx",
    )(*args)
    return res if not latent else res[0]


def _merge_kernel(x_ref, gr_ref, ga_ref, gc_ref, cu_ref, cv_ref, ro_ref, ao_ref,
                  wr_ref, wa_ref, wc_ref, wo_ref, cmn_ref, ws_ref, bs_ref, g1_ref, o_ref, *, tm):
    u = jax.nn.gelu(cu_ref[...].astype(F32), approximate=True)
    v = jax.nn.gelu(cv_ref[...].astype(F32), approximate=True)
    vn = v * lax.rsqrt(jnp.mean(v * v, axis=-1, keepdims=True) + EPS) * cmn_ref[...]
    vb = vn.astype(BF16)
    rows = []
    for n in range(tm // CHUNK):
        cols = []
        for g in range(CM_GROUPS):
            vg = vb[n * CHUNK:(n + 1) * CHUNK, g * LANES:(g + 1) * LANES]
            cols.append(_dot(ws_ref[g], vg) + bs_ref[:, g:g + 1])
        rows.append(jnp.concatenate(cols, axis=1))
    cm = (u * jnp.concatenate(rows, axis=0)).astype(BF16)
    merged = (jax.nn.sigmoid(gr_ref[...].astype(F32)) * _dot(ro_ref[...], wr_ref[...])
              + jax.nn.sigmoid(ga_ref[...].astype(F32)) * _dot(ao_ref[...], wa_ref[...])
              + jax.nn.sigmoid(gc_ref[...].astype(F32)) * _dot(cm, wc_ref[...]))
    y = _dot(merged.astype(BF16), wo_ref[...])
    o_ref[...] = x_ref[...] + g1_ref[0] * y


def _merge(x, z, ret_o, att_o, wr, wa, wc, wo, cmn, ws, bs_t, g1, seq, latent):
    t = x.shape[0]
    tm = 256
    mrow = _mod_row_fn(tm, seq, latent)
    row_spec = lambda w, cb: pl.BlockSpec((tm, w), lambda i: (i, cb))
    full = lambda shape: pl.BlockSpec(shape, lambda i: (0,) * len(shape))
    return pl.pallas_call(
        functools.partial(_merge_kernel, tm=tm),
        grid=(t // tm,),
        in_specs=[
            row_spec(D_MODEL, 0),
            row_spec(D_MODEL, COL_GR // D_MODEL), row_spec(D_MODEL, COL_GA // D_MODEL),
            row_spec(D_MODEL, COL_GC // D_MODEL),
            row_spec(CM_W, COL_CU // CM_W), row_spec(CM_W, COL_CV // CM_W),
            row_spec(RET_W, 0), row_spec(ATT_Q_W, 0),
            full((RET_W, D_MODEL)), full((ATT_Q_W, D_MODEL)), full((CM_W, D_MODEL)),
            full((D_MODEL, D_MODEL)),
            full((1, CM_W)), full((CM_GROUPS, CHUNK, CHUNK)), full((CHUNK, CM_GROUPS)),
            pl.BlockSpec((1, 1, D_MODEL), lambda i: (mrow(i), 0, 0)),
        ],
        out_specs=row_spec(D_MODEL, 0),
        out_shape=jax.ShapeDtypeStruct((t, D_MODEL), F32),
        compiler_params=_cparams(("parallel",)),
        name="merge",
    )(x, z, z, z, z, z, ret_o, att_o, wr, wa, wc, wo, cmn, ws, bs_t, g1)


def _ffn_kernel(x_ref, g_ref, sc_ref, sh_ref, w1_ref, w2_ref, g2_ref, gf_ref, o_ref, h_ref, acc_ref, *, final):
    k = pl.program_id(1)

    @pl.when(k == 0)
    def _():
        h_ref[...] = _norm_mod(x_ref[...], g_ref[...], sc_ref[0], sh_ref[0]).astype(BF16)
        acc_ref[...] = jnp.zeros_like(acc_ref)

    a = jnp.square(jnp.maximum(_dot(h_ref[...], w1_ref[...]), 0.0))
    acc_ref[...] += _dot(a.astype(BF16), w2_ref[...])

    @pl.when(k == pl.num_programs(1) - 1)
    def _():
        xn = x_ref[...] + g2_ref[0] * acc_ref[...]
        if final:
            xn = xn * lax.rsqrt(jnp.mean(xn * xn, axis=-1, keepdims=True) + EPS) * gf_ref[...]
        o_ref[...] = xn


def _ffn(x, g, sc, sh, w1, w2, g2, gfin, seq, latent, final):
    t = x.shape[0]
    tm, tf = 1024, 1024
    mrow = _mod_row_fn(tm, seq, latent)
    mod_spec = pl.BlockSpec((1, 1, D_MODEL), lambda i, k: (mrow(i), 0, 0))
    vec_spec = pl.BlockSpec((1, D_MODEL), lambda i, k: (0, 0))
    return pl.pallas_call(
        functools.partial(_ffn_kernel, final=final),
        grid=(t // tm, D_FF // tf),
        in_specs=[
            pl.BlockSpec((tm, D_MODEL), lambda i, k: (i, 0)),
            vec_spec, mod_spec, mod_spec,
            pl.BlockSpec((D_MODEL, tf), lambda i, k: (0, k)),
            pl.BlockSpec((tf, D_MODEL), lambda i, k: (k, 0)),
            mod_spec, vec_spec,
        ],
        out_specs=pl.BlockSpec((tm, D_MODEL), lambda i, k: (i, 0)),
        out_shape=jax.ShapeDtypeStruct((t, D_MODEL), F32),
        scratch_shapes=[pltpu.VMEM((tm, D_MODEL), BF16), pltpu.VMEM((tm, D_MODEL), F32)],
        compiler_params=_cparams(("parallel", "arbitrary")),
        name="ffn_final" if final else "ffn",
    )(x, g, sc, sh, w1, w2, g2, gfin)


def _rope_tables(length):
    rows = length // GRID_W
    row = jnp.repeat(jnp.arange(rows, dtype=F32), GRID_W)
    col = jnp.tile(jnp.arange(GRID_W, dtype=F32), rows)
    half = HEAD_DIM // 2
    inv = 1.0 / (ROPE_BASE ** (jnp.arange(0, half, 2, dtype=F32) / half))
    ang = jnp.concatenate([row[:, None] * inv[None, :], col[:, None] * inv[None, :]], axis=-1)
    cos, sin = jnp.cos(ang), jnp.sin(ang)
    cos2 = jnp.repeat(cos, 2, axis=-1)
    sin2 = jnp.stack([-sin, sin], axis=-1).reshape(length, HEAD_DIM)
    return jnp.tile(cos2, (1, 2)), jnp.tile(sin2, (1, 2))


def _permute_w_in(w_in):
    o = np.cumsum((0, 512, 512, 512, 512, 512, 128, 128, 512, 512, 1024, 1024, 1024))
    seg = lambda i: w_in[:, :, int(o[i]):int(o[i + 1])]
    rq, rk, rv, rg, aq, ak, av, cu, cv, gr, ga, gc = (seg(i) for i in range(12))
    aq = aq.reshape(DEPTH, D_MODEL, H_Q, HEAD_DIM)[:, :, np.array(Q_HEAD_ORDER)].reshape(DEPTH, D_MODEL, ATT_Q_W)
    return jnp.concatenate([gr, ga, gc, rq, rk, rv, rg, aq, cu, cv, ak, av], axis=-1).astype(BF16)


def kernel(x_prompt, x_sample, cache_k, cache_v, state_ret, c, c_ctx, w_mod, b_mod, g_norm1, g_norm2, w_in, g_q, g_k, ret_decay, ret_gn, cm_norm, cm_ws, cm_bs, w_br_ret, w_br_att, w_br_cm, w_out, w_ff1, w_ff2, g_final):
    batch, dec_batch = x_prompt.shape[0], x_sample.shape[0]

    cond = jnp.concatenate(
        [c_ctx[None, :], c, jnp.zeros((N_MOD_ROWS - 1 - dec_batch, D_MODEL), F32)], axis=0)
    mod = _modulation(cond, w_mod, b_mod)
    mod = mod.reshape(DEPTH, N_MOD_ROWS, 6, D_MODEL).transpose(0, 2, 1, 3).reshape(DEPTH, 6, N_MOD_ROWS, 1, D_MODEL)

    w_in_p = _permute_w_in(w_in)
    w_br_ret_b = w_br_ret.astype(BF16)
    w_br_att_b = w_br_att.reshape(DEPTH, H_Q, HEAD_DIM, D_MODEL)[:, np.array(Q_HEAD_ORDER)].reshape(
        DEPTH, ATT_Q_W, D_MODEL).astype(BF16)
    w_br_cm_b = w_br_cm.astype(BF16)
    w_out_b = w_out.astype(BF16)
    w_ff1_b = w_ff1.astype(BF16)
    w_ff2_b = w_ff2.astype(BF16)
    cm_ws_b = cm_ws.astype(BF16)
    cm_bs_t = jnp.swapaxes(cm_bs, 1, 2)
    g_q2 = jnp.tile(g_q, (1, 2)).reshape(DEPTH, 1, LANES)
    g_k2 = jnp.tile(g_k, (1, 2)).reshape(DEPTH, 1, LANES)
    cos, sin = _rope_tables(DEC_SEQ)
    cache_k2 = cache_k.reshape(dec_batch, DEPTH, PAST_LEN, LANES)
    cache_v2 = cache_v.reshape(dec_batch, DEPTH, PAST_LEN, LANES)
    g_fin = g_final.reshape(1, D_MODEL)

    xp = x_prompt.reshape(batch * SEQ, D_MODEL)
    xs = x_sample.reshape(dec_batch * DEC_SEQ, D_MODEL)
    ks_list, vs_list, ss_list = [], [], []
    for l in range(DEPTH):
        sh1, sc1, g1, sh2, sc2, g2 = (mod[l, i] for i in range(6))
        gn1 = g_norm1[l].reshape(1, D_MODEL)
        gn2 = g_norm2[l].reshape(1, D_MODEL)
        final = l == DEPTH - 1
        for latent in (False, True):
            x = xs if latent else xp
            seq = DEC_SEQ if latent else SEQ
            z = _in_proj(x, gn1, sc1, sh1, w_in_p[l], seq, latent)
            if latent:
                ret_o = _retention(z, ret_decay[l], ret_gn[l], seq, True, cos, sin, state_ret, l)
                att_o = _attention(z, g_q2[l], g_k2[l], seq, True, cos, sin, cache_k2, cache_v2, l)
            else:
                ret_o, s_fin = _retention(z, ret_decay[l], ret_gn[l], seq, False)
                att_o, k_new, v_new = _attention(z, g_q2[l], g_k2[l], seq, False)
                ks_list.append(k_new.reshape(batch, SEQ, H_KV, HEAD_DIM))
                vs_list.append(v_new.reshape(batch, SEQ, H_KV, HEAD_DIM))
                ss_list.append(s_fin)
            x = _merge(x, z, ret_o, att_o, w_br_ret_b[l], w_br_att_b[l], w_br_cm_b[l], w_out_b[l],
                       cm_norm[l].reshape(1, CM_W), cm_ws_b[l], cm_bs_t[l], g1, seq, latent)
            x = _ffn(x, gn2, sc2, sh2, w_ff1_b[l], w_ff2_b[l], g2, g_fin, seq, latent, final)
            if latent:
                xs = x
            else:
                xp = x
    y_prompt = xp.reshape(batch, SEQ, D_MODEL)
    y_sample = xs.reshape(dec_batch, DEC_SEQ, D_MODEL)
    return (y_prompt, y_sample, jnp.stack(ks_list, axis=1), jnp.stack(vs_list, axis=1),
            jnp.stack(ss_list, axis=1))
```

```python
import functools

import numpy as np
import jax
import jax.numpy as jnp
from jax import lax
from jax.experimental import pallas as pl
from jax.experimental.pallas import tpu as pltpu

D_MODEL = 1024
DEPTH = 2
SEQ = 256
DEC_SEQ = 2048
PAST_LEN = 256
GRID_W = 64
HEAD_DIM = 64
H_RET = 8
H_Q = 8
H_KV = 2
CHUNK = 128
CM_GROUPS = 4
RET_W = 512
ATT_Q_W = 512
ATT_KV_W = 128
CM_W = 512
D_FF = 4096
ROPE_BASE = 10000.0
EPS = 1e-6
IN_W = 6912
N_MOD_ROWS = 16

LANES = 128
MXU_N = 256
RET_UNROLL = 4
ATT_LAG = 2
ATT_BUFS = ATT_LAG + 1
VMEM_LIMIT = 48 * 1024 * 1024

F32 = jnp.float32
BF16 = jnp.bfloat16

COL_GR, COL_GA, COL_GC = 0, 1024, 2048
COL_RQ, COL_RK, COL_RV, COL_RG = 3072, 3584, 4096, 4608
COL_AQ, COL_CU, COL_CV = 5120, 5632, 6144
COL_AK, COL_AV = 6656, 6784
Q_HEAD_ORDER = (0, 4, 1, 5, 2, 6, 3, 7)


def _cparams(sem):
    return pltpu.CompilerParams(dimension_semantics=sem, vmem_limit_bytes=VMEM_LIMIT)


def _dot(a, b):
    return jnp.dot(a, b, preferred_element_type=F32)


def _dot_nt(a, b):
    return lax.dot_general(a, b, (((1,), (1,)), ((), ())), preferred_element_type=F32)


def _dot_tn(a, b):
    return lax.dot_general(a, b, (((0,), (0,)), ((), ())), preferred_element_type=F32)


def _lo_mask(shape):
    return lax.broadcasted_iota(jnp.int32, shape, len(shape) - 1) < HEAD_DIM


def _seg_mean(x, lo):
    s_lo = jnp.sum(jnp.where(lo, x, 0.0), axis=-1, keepdims=True)
    s_hi = jnp.sum(jnp.where(lo, 0.0, x), axis=-1, keepdims=True)
    return jnp.where(lo, s_lo, s_hi) * (1.0 / HEAD_DIM)


def _rope(x, cos, sin_signed):
    nxt = pltpu.roll(x, LANES - 1, 1)
    prv = pltpu.roll(x, 1, 1)
    even = (lax.broadcasted_iota(jnp.int32, x.shape, 1) & 1) == 0
    return x * cos + jnp.where(even, nxt, prv) * sin_signed


def _silu(x):
    return x * jax.nn.sigmoid(x)


def _norm_mod(x, g, sc, sh):
    ms = jnp.mean(x * x, axis=-1, keepdims=True)
    y = x * lax.rsqrt(ms + EPS) * g
    return y * (1.0 + sc) + sh


def _mod_kernel(c_ref, w_ref, b_ref, o_ref):
    s = _silu(c_ref[...])
    o_ref[0] = jnp.dot(s, w_ref[0], preferred_element_type=F32,
                       precision=lax.Precision.HIGHEST) + b_ref[0]


def _modulation(cond, w_mod, b_mod):
    tn = 1536
    n = 6 * D_MODEL
    return pl.pallas_call(
        _mod_kernel,
        grid=(DEPTH, n // tn),
        in_specs=[
            pl.BlockSpec((N_MOD_ROWS, D_MODEL), lambda l, j: (0, 0)),
            pl.BlockSpec((1, D_MODEL, tn), lambda l, j: (l, 0, j)),
            pl.BlockSpec((1, 1, tn), lambda l, j: (l, 0, j)),
        ],
        out_specs=pl.BlockSpec((1, N_MOD_ROWS, tn), lambda l, j: (l, 0, j)),
        out_shape=jax.ShapeDtypeStruct((DEPTH, N_MOD_ROWS, n), F32),
        compiler_params=_cparams(("arbitrary", "arbitrary")),
        name="modulation",
    )(cond, w_mod, b_mod.reshape(DEPTH, 1, n))


def _mod_row_fn(tm, seq, latent):
    if latent:
        return lambda i: 1 + (i * tm) // seq
    return lambda i: 0


def _proj_kernel(x_ref, g_ref, sc_ref, sh_ref, w_ref, o_ref, h_ref):
    @pl.when(pl.program_id(1) == 0)
    def _():
        h_ref[...] = _norm_mod(x_ref[...], g_ref[...], sc_ref[0], sh_ref[0]).astype(BF16)

    o_ref[...] = _dot(h_ref[...], w_ref[...]).astype(BF16)


def _in_proj(x, g, sc, sh, w, seq, latent):
    t = x.shape[0]
    tm, tn = 1024, 2304
    mrow = _mod_row_fn(tm, seq, latent)
    return pl.pallas_call(
        _proj_kernel,
        grid=(t // tm, IN_W // tn),
        in_specs=[
            pl.BlockSpec((tm, D_MODEL), lambda i, j: (i, 0)),
            pl.BlockSpec((1, D_MODEL), lambda i, j: (0, 0)),
            pl.BlockSpec((1, 1, D_MODEL), lambda i, j: (mrow(i), 0, 0)),
            pl.BlockSpec((1, 1, D_MODEL), lambda i, j: (mrow(i), 0, 0)),
            pl.BlockSpec((D_MODEL, tn), lambda i, j: (0, j)),
        ],
        out_specs=pl.BlockSpec((tm, tn), lambda i, j: (i, j)),
        out_shape=jax.ShapeDtypeStruct((t, IN_W), BF16),
        scratch_shapes=[pltpu.VMEM((tm, D_MODEL), BF16)],
        compiler_params=_cparams(("parallel", "arbitrary")),
        name="in_proj",
    )(x, g, sc, sh, w)


def _ret_kernel(*refs, nc, latent):
    if latent:
        (dec_ref, q_ref, k_ref, v_ref, rg_ref, gn_ref, cos_ref, sin_ref, s0_ref,
         o_ref, cst_ref, uu_ref) = refs
    else:
        (dec_ref, q_ref, k_ref, v_ref, rg_ref, gn_ref,
         o_ref, sfin_ref, cst_ref, uu_ref) = refs
    hp = pl.program_id(1)
    sq = (CHUNK, LANES)
    lane = lax.broadcasted_iota(jnp.int32, sq, 1)
    row = lax.broadcasted_iota(jnp.int32, sq, 0)
    lo = lane < HEAD_DIM
    blockdiag = (row < HEAD_DIM) == lo
    rowf = row.astype(F32)
    rel = (row - lane).astype(F32)

    def log_decay(direction, head):
        d = jnp.full(sq, dec_ref[direction, head], F32)
        return -(jnp.maximum(d, 0.0) + jnp.log1p(jnp.exp(-jnp.abs(d))))

    ldf0, ldf1 = log_decay(0, 2 * hp), log_decay(0, 2 * hp + 1)
    ldb0, ldb1 = log_decay(1, 2 * hp), log_decay(1, 2 * hp + 1)
    ldf = jnp.where(lo, ldf0, ldf1)
    ldb = jnp.where(lo, ldb0, ldb1)

    def decay_mask(lf, lb):
        fwd = jnp.exp(lf * jnp.maximum(rel, 0.0))
        bwd = jnp.exp(lb * jnp.maximum(-rel, 0.0))
        return jnp.where(rel > 0, fwd, jnp.where(rel < 0, bwd, 2.0))

    cst_ref[0] = decay_mask(ldf0, ldb0)
    cst_ref[1] = decay_mask(ldf1, ldb1)
    cst_ref[2] = jnp.exp(ldf * (CHUNK - 1.0 - rowf))
    cst_ref[3] = jnp.exp(ldb * rowf)
    cst_ref[4] = jnp.exp(ldf * (rowf + 1.0))
    cst_ref[5] = jnp.exp(ldb * (CHUNK - rowf))
    gf = jnp.exp(ldf * float(CHUNK))
    gb = jnp.exp(ldb * float(CHUNK))

    def rows_of(c):
        return pl.ds(pl.multiple_of(c * CHUNK, CHUNK), CHUNK)

    def load_qk(ref, c, scale):
        x = ref[rows_of(c), :].astype(F32)
        if scale != 1.0:
            x = x * scale
        if latent:
            x = _rope(x, cos_ref[rows_of(c), :], sin_ref[rows_of(c), :])
        return x

    def phase_a(c, carry):
        k = load_qk(k_ref, c, HEAD_DIM ** -0.5)
        v = v_ref[rows_of(c), :]
        kz = jnp.concatenate([(k * cst_ref[2]).astype(BF16), (k * cst_ref[3]).astype(BF16)], axis=1)
        uu_ref[c] = _dot_tn(kz, v)
        return carry

    unroll = min(nc, RET_UNROLL)
    lax.fori_loop(0, nc, phase_a, 0, unroll=unroll)

    if latent:
        zeros = jnp.zeros((HEAD_DIM, HEAD_DIM), F32)

        def embed(direction):
            top = jnp.concatenate([s0_ref[0, 0, direction, 0], zeros], axis=1)
            bot = jnp.concatenate([zeros, s0_ref[0, 0, direction, 1]], axis=1)
            return jnp.concatenate([top, bot], axis=0)

        sf0, sb0 = embed(0), embed(1)
    else:
        sf0 = sb0 = jnp.zeros(sq, F32)

    def scan_f(c, s):
        u = uu_ref[c, 0:CHUNK, :]
        uu_ref[c, 0:CHUNK, :] = s
        return gf * s + jnp.where(blockdiag, u, 0.0)

    def scan_b(i, s):
        c = nc - 1 - i
        u = uu_ref[c, CHUNK:2 * CHUNK, :]
        uu_ref[c, CHUNK:2 * CHUNK, :] = s
        return gb * s + jnp.where(blockdiag, u, 0.0)

    sf = lax.fori_loop(0, nc, scan_f, sf0, unroll=True)
    sb = lax.fori_loop(0, nc, scan_b, sb0, unroll=True)
    if not latent:
        for d, s in ((0, sf), (1, sb)):
            sfin_ref[0, d, 0] = s[0:HEAD_DIM, 0:HEAD_DIM]
            sfin_ref[0, d, 1] = s[HEAD_DIM:, HEAD_DIM:]

    def phase_c(c, carry):
        q = load_qk(q_ref, c, 1.0)
        k = load_qk(k_ref, c, HEAD_DIM ** -0.5)
        v = v_ref[rows_of(c), :]
        qb = q.astype(BF16)
        k0 = jnp.where(lo, k, 0.0).astype(BF16)
        k1 = jnp.where(lo, 0.0, k).astype(BF16)
        p0 = (_dot_nt(qb, k0) * cst_ref[0]).astype(BF16)
        p1 = (_dot_nt(qb, k1) * cst_ref[1]).astype(BF16)
        inner = jnp.where(lo, _dot(p0, v), _dot(p1, v))
        qx = jnp.concatenate([(q * cst_ref[4]).astype(BF16), (q * cst_ref[5]).astype(BF16)], axis=1)
        o = inner + _dot(qx, uu_ref[c].astype(BF16))
        mu = _seg_mean(o, lo)
        d = o - mu
        var = _seg_mean(d * d, lo)
        y = d * lax.rsqrt(var + EPS) * gn_ref[...]
        o_ref[rows_of(c), :] = (y * _silu(rg_ref[rows_of(c), :].astype(F32))).astype(BF16)
        return carry

    lax.fori_loop(0, nc, phase_c, 0, unroll=unroll)


def _retention(z, dec, gn, seq, latent, cos=None, sin=None, s0=None, layer=None):
    t = z.shape[0]
    b = t // seq
    nc = seq // CHUNK
    npair = H_RET // 2

    def zcol(base):
        return pl.BlockSpec((seq, LANES), lambda bi, hp: (bi, base // LANES + hp))

    in_specs = [
        pl.BlockSpec(memory_space=pltpu.SMEM),
        zcol(COL_RQ), zcol(COL_RK), zcol(COL_RV), zcol(COL_RG),
        pl.BlockSpec((1, LANES), lambda bi, hp: (0, hp)),
    ]
    args = [dec, z, z, z, z, gn.reshape(1, RET_W)]
    out_shape = [jax.ShapeDtypeStruct((t, RET_W), BF16)]
    out_specs = [pl.BlockSpec((seq, LANES), lambda bi, hp: (bi, hp))]
    if latent:
        in_specs += [
            pl.BlockSpec((seq, LANES), lambda bi, hp: (0, 0)),
            pl.BlockSpec((seq, LANES), lambda bi, hp: (0, 0)),
            pl.BlockSpec((1, 1, 2, 2, HEAD_DIM, HEAD_DIM), lambda bi, hp: (bi, layer, 0, hp, 0, 0)),
        ]
        args += [cos, sin, s0]
    else:
        out_shape.append(jax.ShapeDtypeStruct((b, 2, H_RET, HEAD_DIM, HEAD_DIM), F32))
        out_specs.append(pl.BlockSpec((1, 2, 2, HEAD_DIM, HEAD_DIM), lambda bi, hp: (bi, 0, hp, 0, 0)))
    res = pl.pallas_call(
        functools.partial(_ret_kernel, nc=nc, latent=latent),
        grid=(b, npair),
        in_specs=in_specs,
        out_specs=out_specs,
        out_shape=out_shape,
        scratch_shapes=[pltpu.VMEM((6, CHUNK, LANES), F32), pltpu.VMEM((nc, 2 * CHUNK, LANES), F32)],
        compiler_params=_cparams(("parallel", "arbitrary")),
        name="retention_latent" if latent else "retention_ctx",
    )(*args)
    return res if not latent else res[0]


def _att_kernel(*refs, tq, seq, latent):
    if latent:
        (q_ref, ak_ref, av_ref, ck_ref, cv_ref, gq_ref, gk_ref, cosq_ref, sinq_ref, cosk_ref, sink_ref,
         o_ref, k0_ref, k1_ref, vv_ref, *sp_refs) = refs
        off = PAST_LEN
    else:
        (q_ref, ak_ref, av_ref, gq_ref, gk_ref,
         o_ref, knew_ref, vnew_ref, k0_ref, k1_ref, vv_ref, *sp_refs) = refs
        off = 0

    @pl.when(pl.program_id(1) == 0)
    def _():
        lo = _lo_mask((seq, LANES))
        ak = ak_ref[...].astype(F32)
        kn = ak * lax.rsqrt(_seg_mean(ak * ak, lo) + EPS) * gk_ref[...]
        av = av_ref[...]
        if latent:
            kn = _rope(kn, cosk_ref[...], sink_ref[...])
            lo_c = _lo_mask((PAST_LEN, LANES))
            ck = ck_ref[0, 0]
            k0_ref[:, 0:off] = jnp.where(lo_c, ck, 0.0).T.astype(BF16)
            k1_ref[:, 0:off] = jnp.where(lo_c, 0.0, ck).T.astype(BF16)
            vv_ref[0:off, :] = cv_ref[0, 0].astype(BF16)
        else:
            knew_ref[...] = kn
            vnew_ref[...] = av.astype(F32)
        k0_ref[:, off:off + seq] = jnp.where(lo, kn, 0.0).T.astype(BF16)
        k1_ref[:, off:off + seq] = jnp.where(lo, 0.0, kn).T.astype(BF16)
        vv_ref[off:off + seq, :] = av

    sk = off + seq
    nkt = sk // MXU_N
    lo = _lo_mask((CHUNK, LANES))
    krefs = (k0_ref, k1_ref)
    units = [(sub, r, g) for sub in range(tq // CHUNK) for r in range(ATT_Q_W // LANES) for g in range(H_KV)]
    qb, m_lane, m_row, l_lane, o_acc, done = {}, {}, {}, {}, {}, {}

    def prep(sub, r):
        rows = slice(sub * CHUNK, (sub + 1) * CHUNK)
        x = q_ref[rows, r * LANES:(r + 1) * LANES].astype(F32)
        xn = x * lax.rsqrt(_seg_mean(x * x, lo) + EPS) * gq_ref[...]
        if latent:
            xn = _rope(xn, cosq_ref[rows, :], sinq_ref[rows, :])
        qb[(sub, r)] = (xn * (HEAD_DIM ** -0.5)).astype(BF16)

    def score_tile(i, kt):
        sub, r, g = units[i]
        cols = slice(kt * MXU_N, (kt + 1) * MXU_N)
        s_t = _dot(qb[(sub, r)], krefs[g][:, cols])
        sp_refs[i % ATT_BUFS][:, cols] = s_t
        t = jnp.maximum(s_t[:, :LANES], s_t[:, LANES:])
        m_lane[i] = t if kt == 0 else jnp.maximum(m_lane[i], t)

    def soft_tile(i, kt):
        cols = slice(kt * MXU_N, (kt + 1) * MXU_N)
        sbuf = sp_refs[i % ATT_BUFS]
        p0 = jnp.exp(sbuf[:, kt * MXU_N:kt * MXU_N + LANES] - m_row[i])
        p1 = jnp.exp(sbuf[:, kt * MXU_N + LANES:(kt + 1) * MXU_N] - m_row[i])
        pv = _dot(jnp.concatenate([p0, p1], axis=1).astype(BF16), vv_ref[cols, :])
        if kt == 0:
            l_lane[i], o_acc[i] = p0 + p1, pv
        else:
            l_lane[i], o_acc[i] = l_lane[i] + (p0 + p1), o_acc[i] + pv

    def finish(i):
        sub, r, g = units[i]
        done[(sub, r, g)] = o_acc.pop(i) / jnp.sum(l_lane.pop(i), axis=-1, keepdims=True)
        if g == H_KV - 1:
            rows = slice(sub * CHUNK, (sub + 1) * CHUNK)
            o = jnp.where(lo, done.pop((sub, r, 0)), done.pop((sub, r, 1)))
            o_ref[rows, r * LANES:(r + 1) * LANES] = o.astype(BF16)

    prep(units[0][0], units[0][1])
    for s in range(len(units) + ATT_LAG):
        nxt = s + H_KV
        if s % H_KV == 0 and nxt < len(units):
            prep(units[nxt][0], units[nxt][1])
        for kt in range(nkt):
            if s < len(units):
                score_tile(s, kt)
            if s >= ATT_LAG:
                soft_tile(s - ATT_LAG, kt)
        if s < len(units):
            m_row[s] = jnp.broadcast_to(jnp.max(m_lane.pop(s), axis=-1, keepdims=True), (CHUNK, LANES))
        if s >= ATT_LAG:
            finish(s - ATT_LAG)


def _attention(z, gq, gk, seq, latent, cos=None, sin=None, cache_k=None, cache_v=None, layer=None):
    t = z.shape[0]
    b = t // seq
    tq = 128 if latent else seq
    nq = seq // tq
    sk = seq + (PAST_LEN if latent else 0)
    kv_spec = lambda base: pl.BlockSpec((seq, LANES), lambda bi, qi: (bi, base // LANES))
    vec_spec = pl.BlockSpec((1, LANES), lambda bi, qi: (0, 0))
    in_specs = [
        pl.BlockSpec((tq, ATT_Q_W), lambda bi, qi: (bi * nq + qi, COL_AQ // ATT_Q_W)),
        kv_spec(COL_AK), kv_spec(COL_AV),
    ]
    args = [z, z, z]
    if latent:
        cache_spec = pl.BlockSpec((1, 1, PAST_LEN, LANES), lambda bi, qi: (bi, layer, 0, 0))
        in_specs += [cache_spec, cache_spec]
        args += [cache_k, cache_v]
    in_specs += [vec_spec, vec_spec]
    args += [gq, gk]
    out_shape = [jax.ShapeDtypeStruct((t, ATT_Q_W), BF16)]
    out_specs = [pl.BlockSpec((tq, ATT_Q_W), lambda bi, qi: (bi * nq + qi, 0))]
    if latent:
        in_specs += [
            pl.BlockSpec((tq, LANES), lambda bi, qi: (qi, 0)),
            pl.BlockSpec((tq, LANES), lambda bi, qi: (qi, 0)),
            pl.BlockSpec((seq, LANES), lambda bi, qi: (0, 0)),
            pl.BlockSpec((seq, LANES), lambda bi, qi: (0, 0)),
        ]
        args += [cos, sin, cos, sin]
    else:
        new_spec = pl.BlockSpec((seq, LANES), lambda bi, qi: (bi, 0))
        out_shape += [jax.ShapeDtypeStruct((t, LANES), F32)] * 2
        out_specs += [new_spec, new_spec]
    res = pl.pallas_call(
        functools.partial(_att_kernel, tq=tq, seq=seq, latent=latent),
        grid=(b, nq),
        in_specs=in_specs,
        out_specs=out_specs,
        out_shape=out_shape,
        scratch_shapes=([pltpu.VMEM((LANES, sk), BF16)] * 2 + [pltpu.VMEM((sk, LANES), BF16)]
                        + [pltpu.VMEM((CHUNK, sk), F32)] * ATT_BUFS),
        compiler_params=_cparams(("parallel", "arbitrary")),
        name="attention_latent" if latent else "attention_ctx",
    )(*args)
    return res if not latent else res[0]


def _merge_kernel(x_ref, gr_ref, ga_ref, gc_ref, cu_ref, cv_ref, ro_ref, ao_ref,
                  wr_ref, wa_ref, wc_ref, wo_ref, cmn_ref, ws_ref, bs_ref, g1_ref, o_ref, *, tm):
    u = jax.nn.gelu(cu_ref[...].astype(F32), approximate=True)
    v = jax.nn.gelu(cv_ref[...].astype(F32), approximate=True)
    vn = v * lax.rsqrt(jnp.mean(v * v, axis=-1, keepdims=True) + EPS) * cmn_ref[...]
    vb = vn.astype(BF16)
    rows = []
    for n in range(tm // CHUNK):
        cols = []
        for g in range(CM_GROUPS):
            vg = vb[n * CHUNK:(n + 1) * CHUNK, g * LANES:(g + 1) * LANES]
            cols.append(_dot(ws_ref[g], vg) + bs_ref[:, g:g + 1])
        rows.append(jnp.concatenate(cols, axis=1))
    cm = (u * jnp.concatenate(rows, axis=0)).astype(BF16)
    merged = (jax.nn.sigmoid(gr_ref[...].astype(F32)) * _dot(ro_ref[...], wr_ref[...])
              + jax.nn.sigmoid(ga_ref[...].astype(F32)) * _dot(ao_ref[...], wa_ref[...])
              + jax.nn.sigmoid(gc_ref[...].astype(F32)) * _dot(cm, wc_ref[...]))
    y = _dot(merged.astype(BF16), wo_ref[...])
    o_ref[...] = x_ref[...] + g1_ref[0] * y


def _merge(x, z, ret_o, att_o, wr, wa, wc, wo, cmn, ws, bs_t, g1, seq, latent):
    t = x.shape[0]
    tm = 256
    mrow = _mod_row_fn(tm, seq, latent)
    row_spec = lambda w, cb: pl.BlockSpec((tm, w), lambda i: (i, cb))
    full = lambda shape: pl.BlockSpec(shape, lambda i: (0,) * len(shape))
    return pl.pallas_call(
        functools.partial(_merge_kernel, tm=tm),
        grid=(t // tm,),
        in_specs=[
            row_spec(D_MODEL, 0),
            row_spec(D_MODEL, COL_GR // D_MODEL), row_spec(D_MODEL, COL_GA // D_MODEL),
            row_spec(D_MODEL, COL_GC // D_MODEL),
            row_spec(CM_W, COL_CU // CM_W), row_spec(CM_W, COL_CV // CM_W),
            row_spec(RET_W, 0), row_spec(ATT_Q_W, 0),
            full((RET_W, D_MODEL)), full((ATT_Q_W, D_MODEL)), full((CM_W, D_MODEL)),
            full((D_MODEL, D_MODEL)),
            full((1, CM_W)), full((CM_GROUPS, CHUNK, CHUNK)), full((CHUNK, CM_GROUPS)),
            pl.BlockSpec((1, 1, D_MODEL), lambda i: (mrow(i), 0, 0)),
        ],
        out_specs=row_spec(D_MODEL, 0),
        out_shape=jax.ShapeDtypeStruct((t, D_MODEL), F32),
        compiler_params=_cparams(("parallel",)),
        name="merge",
    )(x, z, z, z, z, z, ret_o, att_o, wr, wa, wc, wo, cmn, ws, bs_t, g1)


def _ffn_kernel(x_ref, g_ref, sc_ref, sh_ref, w1_ref, w2_ref, g2_ref, gf_ref, o_ref, h_ref, acc_ref, *, final):
    k = pl.program_id(1)

    @pl.when(k == 0)
    def _():
        h_ref[...] = _norm_mod(x_ref[...], g_ref[...], sc_ref[0], sh_ref[0]).astype(BF16)
        acc_ref[...] = jnp.zeros_like(acc_ref)

    a = jnp.square(jnp.maximum(_dot(h_ref[...], w1_ref[...]), 0.0))
    acc_ref[...] += _dot(a.astype(BF16), w2_ref[...])

    @pl.when(k == pl.num_programs(1) - 1)
    def _():
        xn = x_ref[...] + g2_ref[0] * acc_ref[...]
        if final:
            xn = xn * lax.rsqrt(jnp.mean(xn * xn, axis=-1, keepdims=True) + EPS) * gf_ref[...]
        o_ref[...] = xn


def _ffn(x, g, sc, sh, w1, w2, g2, gfin, seq, latent, final):
    t = x.shape[0]
    tm, tf = 1024, 1024
    mrow = _mod_row_fn(tm, seq, latent)
    mod_spec = pl.BlockSpec((1, 1, D_MODEL), lambda i, k: (mrow(i), 0, 0))
    vec_spec = pl.BlockSpec((1, D_MODEL), lambda i, k: (0, 0))
    return pl.pallas_call(
        functools.partial(_ffn_kernel, final=final),
        grid=(t // tm, D_FF // tf),
        in_specs=[
            pl.BlockSpec((tm, D_MODEL), lambda i, k: (i, 0)),
            vec_spec, mod_spec, mod_spec,
            pl.BlockSpec((D_MODEL, tf), lambda i, k: (0, k)),
            pl.BlockSpec((tf, D_MODEL), lambda i, k: (k, 0)),
            mod_spec, vec_spec,
        ],
        out_specs=pl.BlockSpec((tm, D_MODEL), lambda i, k: (i, 0)),
        out_shape=jax.ShapeDtypeStruct((t, D_MODEL), F32),
        scratch_shapes=[pltpu.VMEM((tm, D_MODEL), BF16), pltpu.VMEM((tm, D_MODEL), F32)],
        compiler_params=_cparams(("parallel", "arbitrary")),
        name="ffn_final" if final else "ffn",
    )(x, g, sc, sh, w1, w2, g2, gfin)


def _rope_tables(length):
    rows = length // GRID_W
    row = jnp.repeat(jnp.arange(rows, dtype=F32), GRID_W)
    col = jnp.tile(jnp.arange(GRID_W, dtype=F32), rows)
    half = HEAD_DIM // 2
    inv = 1.0 / (ROPE_BASE ** (jnp.arange(0, half, 2, dtype=F32) / half))
    ang = jnp.concatenate([row[:, None] * inv[None, :], col[:, None] * inv[None, :]], axis=-1)
    cos, sin = jnp.cos(ang), jnp.sin(ang)
    cos2 = jnp.repeat(cos, 2, axis=-1)
    sin2 = jnp.stack([-sin, sin], axis=-1).reshape(length, HEAD_DIM)
    return jnp.tile(cos2, (1, 2)), jnp.tile(sin2, (1, 2))


def _permute_w_in(w_in):
    o = np.cumsum((0, 512, 512, 512, 512, 512, 128, 128, 512, 512, 1024, 1024, 1024))
    seg = lambda i: w_in[:, :, int(o[i]):int(o[i + 1])]
    rq, rk, rv, rg, aq, ak, av, cu, cv, gr, ga, gc = (seg(i) for i in range(12))
    aq = aq.reshape(DEPTH, D_MODEL, H_Q, HEAD_DIM)[:, :, np.array(Q_HEAD_ORDER)].reshape(DEPTH, D_MODEL, ATT_Q_W)
    return jnp.concatenate([gr, ga, gc, rq, rk, rv, rg, aq, cu, cv, ak, av], axis=-1).astype(BF16)


def kernel(x_prompt, x_sample, cache_k, cache_v, state_ret, c, c_ctx, w_mod, b_mod, g_norm1, g_norm2, w_in, g_q, g_k, ret_decay, ret_gn, cm_norm, cm_ws, cm_bs, w_br_ret, w_br_att, w_br_cm, w_out, w_ff1, w_ff2, g_final):
    batch, dec_batch = x_prompt.shape[0], x_sample.shape[0]

    cond = jnp.concatenate(
        [c_ctx[None, :], c, jnp.zeros((N_MOD_ROWS - 1 - dec_batch, D_MODEL), F32)], axis=0)
    mod = _modulation(cond, w_mod, b_mod)
    mod = mod.reshape(DEPTH, N_MOD_ROWS, 6, D_MODEL).transpose(0, 2, 1, 3).reshape(DEPTH, 6, N_MOD_ROWS, 1, D_MODEL)

    w_in_p = _permute_w_in(w_in)
    w_br_ret_b = w_br_ret.astype(BF16)
    w_br_att_b = w_br_att.reshape(DEPTH, H_Q, HEAD_DIM, D_MODEL)[:, np.array(Q_HEAD_ORDER)].reshape(
        DEPTH, ATT_Q_W, D_MODEL).astype(BF16)
    w_br_cm_b = w_br_cm.astype(BF16)
    w_out_b = w_out.astype(BF16)
    w_ff1_b = w_ff1.astype(BF16)
    w_ff2_b = w_ff2.astype(BF16)
    cm_ws_b = cm_ws.astype(BF16)
    cm_bs_t = jnp.swapaxes(cm_bs, 1, 2)
    g_q2 = jnp.tile(g_q, (1, 2)).reshape(DEPTH, 1, LANES)
    g_k2 = jnp.tile(g_k, (1, 2)).reshape(DEPTH, 1, LANES)
    cos, sin = _rope_tables(DEC_SEQ)
    cache_k2 = cache_k.reshape(dec_batch, DEPTH, PAST_LEN, LANES)
    cache_v2 = cache_v.reshape(dec_batch, DEPTH, PAST_LEN, LANES)
    g_fin = g_final.reshape(1, D_MODEL)

    xp = x_prompt.reshape(batch * SEQ, D_MODEL)
    xs = x_sample.reshape(dec_batch * DEC_SEQ, D_MODEL)
    ks_list, vs_list, ss_list = [], [], []
    for l in range(DEPTH):
        sh1, sc1, g1, sh2, sc2, g2 = (mod[l, i] for i in range(6))
        gn1 = g_norm1[l].reshape(1, D_MODEL)
        gn2 = g_norm2[l].reshape(1, D_MODEL)
        final = l == DEPTH - 1
        for latent in (False, True):
            x = xs if latent else xp
            seq = DEC_SEQ if latent else SEQ
            z = _in_proj(x, gn1, sc1, sh1, w_in_p[l], seq, latent)
            if latent:
                ret_o = _retention(z, ret_decay[l], ret_gn[l], seq, True, cos, sin, state_ret, l)
                att_o = _attention(z, g_q2[l], g_k2[l], seq, True, cos, sin, cache_k2, cache_v2, l)
            else:
                ret_o, s_fin = _retention(z, ret_decay[l], ret_gn[l], seq, False)
                att_o, k_new, v_new = _attention(z, g_q2[l], g_k2[l], seq, False)
                ks_list.append(k_new.reshape(batch, SEQ, H_KV, HEAD_DIM))
                vs_list.append(v_new.reshape(batch, SEQ, H_KV, HEAD_DIM))
                ss_list.append(s_fin)
            x = _merge(x, z, ret_o, att_o, w_br_ret_b[l], w_br_att_b[l], w_br_cm_b[l], w_out_b[l],
                       cm_norm[l].reshape(1, CM_W), cm_ws_b[l], cm_bs_t[l], g1, seq, latent)
            x = _ffn(x, gn2, sc2, sh2, w_ff1_b[l], w_ff2_b[l], g2, g_fin, seq, latent, final)
            if latent:
                xs = x
            else:
                xp = x
    y_prompt = xp.reshape(batch, SEQ, D_MODEL)
    y_sample = xs.reshape(dec_batch, DEC_SEQ, D_MODEL)
    return (y_prompt, y_sample, jnp.stack(ks_list, axis=1), jnp.stack(vs_list, axis=1),
            jnp.stack(ss_list, axis=1))
```

```python
import functools

import numpy as np
import jax
import jax.numpy as jnp
from jax import lax
from jax.experimental import pallas as pl
from jax.experimental.pallas import tpu as pltpu

D_MODEL = 1024
DEPTH = 2
SEQ = 256
DEC_SEQ = 2048
PAST_LEN = 256
GRID_W = 64
HEAD_DIM = 64
H_RET = 8
H_Q = 8
H_KV = 2
CHUNK = 128
CM_GROUPS = 4
RET_W = 512
ATT_Q_W = 512
ATT_KV_W = 128
CM_W = 512
D_FF = 4096
ROPE_BASE = 10000.0
EPS = 1e-6
IN_W = 6912
N_MOD_ROWS = 16

LANES = 128
MXU_N = 256
RET_GROUP = 8
ATT_LAG = 2
ATT_BUFS = ATT_LAG + 1
VMEM_LIMIT = 48 * 1024 * 1024

F32 = jnp.float32
BF16 = jnp.bfloat16

COL_GR, COL_GA, COL_GC = 0, 1024, 2048
COL_RQ, COL_RK, COL_RV, COL_RG = 3072, 3584, 4096, 4608
COL_AQ, COL_CU, COL_CV = 5120, 5632, 6144
COL_AK, COL_AV = 6656, 6784
Q_HEAD_ORDER = (0, 4, 1, 5, 2, 6, 3, 7)


def _cparams(sem):
    return pltpu.CompilerParams(dimension_semantics=sem, vmem_limit_bytes=VMEM_LIMIT)


def _dot(a, b):
    return jnp.dot(a, b, preferred_element_type=F32)


def _dot_nt(a, b):
    return lax.dot_general(a, b, (((1,), (1,)), ((), ())), preferred_element_type=F32)


def _dot_tn(a, b):
    return lax.dot_general(a, b, (((0,), (0,)), ((), ())), preferred_element_type=F32)


def _lo_mask(shape):
    return lax.broadcasted_iota(jnp.int32, shape, len(shape) - 1) < HEAD_DIM


def _seg_mean(x, lo):
    s_lo = jnp.sum(jnp.where(lo, x, 0.0), axis=-1, keepdims=True)
    s_hi = jnp.sum(jnp.where(lo, 0.0, x), axis=-1, keepdims=True)
    return jnp.where(lo, s_lo, s_hi) * (1.0 / HEAD_DIM)


def _rope(x, cos, sin_signed):
    nxt = pltpu.roll(x, LANES - 1, 1)
    prv = pltpu.roll(x, 1, 1)
    even = (lax.broadcasted_iota(jnp.int32, x.shape, 1) & 1) == 0
    return x * cos + jnp.where(even, nxt, prv) * sin_signed


def _sigmoid(x):
    return 0.5 * jnp.tanh(0.5 * x) + 0.5


def _silu(x):
    return x * _sigmoid(x)


def _seg_mean_mxu(x, bmat, passes):
    hi = x.astype(BF16)
    out = _dot(hi, bmat)
    if passes == 2:
        out = out + _dot((x - hi.astype(F32)).astype(BF16), bmat)
    return out


def _norm_mod(x, g, sc, sh):
    ms = jnp.mean(x * x, axis=-1, keepdims=True)
    y = x * lax.rsqrt(ms + EPS) * g
    return y * (1.0 + sc) + sh


def _mod_kernel(c_ref, w_ref, b_ref, o_ref):
    s = _silu(c_ref[...])
    o_ref[0] = jnp.dot(s, w_ref[0], preferred_element_type=F32,
                       precision=lax.Precision.HIGHEST) + b_ref[0]


def _modulation(cond, w_mod, b_mod):
    tn = 1536
    n = 6 * D_MODEL
    return pl.pallas_call(
        _mod_kernel,
        grid=(DEPTH, n // tn),
        in_specs=[
            pl.BlockSpec((N_MOD_ROWS, D_MODEL), lambda l, j: (0, 0)),
            pl.BlockSpec((1, D_MODEL, tn), lambda l, j: (l, 0, j)),
            pl.BlockSpec((1, 1, tn), lambda l, j: (l, 0, j)),
        ],
        out_specs=pl.BlockSpec((1, N_MOD_ROWS, tn), lambda l, j: (l, 0, j)),
        out_shape=jax.ShapeDtypeStruct((DEPTH, N_MOD_ROWS, n), F32),
        compiler_params=_cparams(("arbitrary", "arbitrary")),
        name="modulation",
    )(cond, w_mod, b_mod.reshape(DEPTH, 1, n))


def _mod_row_fn(tm, seq, latent):
    if latent:
        return lambda i: 1 + (i * tm) // seq
    return lambda i: 0


def _proj_kernel(x_ref, g_ref, sc_ref, sh_ref, w_ref, o_ref, h_ref):
    @pl.when(pl.program_id(1) == 0)
    def _():
        h_ref[...] = _norm_mod(x_ref[...], g_ref[...], sc_ref[0], sh_ref[0]).astype(BF16)

    o_ref[...] = _dot(h_ref[...], w_ref[...]).astype(BF16)


def _in_proj(x, g, sc, sh, w, layer, seq, latent):
    t = x.shape[0]
    tm, tn = 1024, 2304
    mrow = _mod_row_fn(tm, seq, latent)
    return pl.pallas_call(
        _proj_kernel,
        grid=(t // tm, IN_W // tn),
        in_specs=[
            pl.BlockSpec((tm, D_MODEL), lambda i, j: (i, 0)),
            pl.BlockSpec((1, D_MODEL), lambda i, j: (0, 0)),
            pl.BlockSpec((1, 1, D_MODEL), lambda i, j: (mrow(i), 0, 0)),
            pl.BlockSpec((1, 1, D_MODEL), lambda i, j: (mrow(i), 0, 0)),
            pl.BlockSpec((None, D_MODEL, tn), lambda i, j: (layer, 0, j)),
        ],
        out_specs=pl.BlockSpec((tm, tn), lambda i, j: (i, j)),
        out_shape=jax.ShapeDtypeStruct((t, IN_W), BF16),
        scratch_shapes=[pltpu.VMEM((tm, D_MODEL), BF16)],
        compiler_params=_cparams(("parallel", "arbitrary")),
        name="in_proj",
    )(x, g, sc, sh, w)


def _ret_kernel(*refs, nc, latent):
    if latent:
        (dec_ref, q_ref, k_ref, v_ref, rg_ref, gn_ref, cos_ref, sin_ref, s0_ref,
         o_ref, cst_ref, uu_ref, kr_ref) = refs
    else:
        (dec_ref, q_ref, k_ref, v_ref, rg_ref, gn_ref,
         o_ref, sfin_ref, cst_ref, uu_ref) = refs
    hp = pl.program_id(1)
    sq = (CHUNK, LANES)
    lane = lax.broadcasted_iota(jnp.int32, sq, 1)
    row = lax.broadcasted_iota(jnp.int32, sq, 0)
    lo = lane < HEAD_DIM
    blockdiag = (row < HEAD_DIM) == lo
    bmat = jnp.where(blockdiag, 1.0 / HEAD_DIM, 0.0).astype(BF16)
    rowf = row.astype(F32)
    rel = (row - lane).astype(F32)

    def log_decay(direction, head):
        d = jnp.full(sq, dec_ref[direction, head], F32)
        return -(jnp.maximum(d, 0.0) + jnp.log1p(jnp.exp(-jnp.abs(d))))

    ldf0, ldf1 = log_decay(0, 2 * hp), log_decay(0, 2 * hp + 1)
    ldb0, ldb1 = log_decay(1, 2 * hp), log_decay(1, 2 * hp + 1)
    ldf = jnp.where(lo, ldf0, ldf1)
    ldb = jnp.where(lo, ldb0, ldb1)

    def decay_mask(lf, lb):
        fwd = jnp.exp(lf * jnp.maximum(rel, 0.0))
        bwd = jnp.exp(lb * jnp.maximum(-rel, 0.0))
        return jnp.where(rel > 0, fwd, jnp.where(rel < 0, bwd, 2.0))

    cst_ref[0] = decay_mask(ldf0, ldb0)
    cst_ref[1] = decay_mask(ldf1, ldb1)
    cst_ref[2] = jnp.exp(ldf * (CHUNK - 1.0 - rowf))
    cst_ref[3] = jnp.exp(ldb * rowf)
    cst_ref[4] = jnp.exp(ldf * (rowf + 1.0))
    cst_ref[5] = jnp.exp(ldb * (CHUNK - rowf))
    gf = jnp.exp(ldf * float(CHUNK))
    gb = jnp.exp(ldb * float(CHUNK))

    def rows_of(c):
        return pl.ds(pl.multiple_of(c * CHUNK, CHUNK), CHUNK)

    def load_qk(ref, c, scale):
        x = ref[rows_of(c), :].astype(F32)
        if scale != 1.0:
            x = x * scale
        if latent:
            x = _rope(x, cos_ref[rows_of(c), :], sin_ref[rows_of(c), :])
        return x

    group = min(nc, RET_GROUP)

    def phase_a(i, carry):
        cs = [i * group + g for g in range(group)]
        ks = [load_qk(k_ref, c, HEAD_DIM ** -0.5) for c in cs]
        if latent:
            for c, k in zip(cs, ks):
                kr_ref[rows_of(c), :] = k
        kzs = [jnp.concatenate([(k * cst_ref[2]).astype(BF16), (k * cst_ref[3]).astype(BF16)], axis=1)
               for k in ks]
        for c, kz in zip(cs, kzs):
            uu_ref[c] = _dot_tn(kz, v_ref[rows_of(c), :])
        return carry

    lax.fori_loop(0, nc // group, phase_a, 0)

    if latent:
        zeros = jnp.zeros((HEAD_DIM, HEAD_DIM), F32)

        def embed(direction):
            top = jnp.concatenate([s0_ref[0, 0, direction, 0], zeros], axis=1)
            bot = jnp.concatenate([zeros, s0_ref[0, 0, direction, 1]], axis=1)
            return jnp.concatenate([top, bot], axis=0)

        sf0, sb0 = embed(0), embed(1)
    else:
        sf0 = sb0 = jnp.zeros(sq, F32)

    def scan_f(c, s):
        u = uu_ref[c, 0:CHUNK, :]
        uu_ref[c, 0:CHUNK, :] = s
        return gf * s + jnp.where(blockdiag, u, 0.0)

    def scan_b(i, s):
        c = nc - 1 - i
        u = uu_ref[c, CHUNK:2 * CHUNK, :]
        uu_ref[c, CHUNK:2 * CHUNK, :] = s
        return gb * s + jnp.where(blockdiag, u, 0.0)

    sf = lax.fori_loop(0, nc, scan_f, sf0, unroll=True)
    sb = lax.fori_loop(0, nc, scan_b, sb0, unroll=True)
    if not latent:
        for d, s in ((0, sf), (1, sb)):
            sfin_ref[0, d, 0] = s[0:HEAD_DIM, 0:HEAD_DIM]
            sfin_ref[0, d, 1] = s[HEAD_DIM:, HEAD_DIM:]

    def phase_c(i, carry):
        cs = [i * group + g for g in range(group)]
        n = range(group)
        q = [load_qk(q_ref, c, 1.0) for c in cs]
        k = [kr_ref[rows_of(c), :] if latent else load_qk(k_ref, c, HEAD_DIM ** -0.5) for c in cs]
        v = [v_ref[rows_of(c), :] for c in cs]
        qb = [x.astype(BF16) for x in q]
        s0 = [_dot_nt(qb[g], jnp.where(lo, k[g], 0.0).astype(BF16)) for g in n]
        s1 = [_dot_nt(qb[g], jnp.where(lo, 0.0, k[g]).astype(BF16)) for g in n]
        qx = [jnp.concatenate([(x * cst_ref[4]).astype(BF16), (x * cst_ref[5]).astype(BF16)], axis=1) for x in q]
        cross = [_dot(qx[g], uu_ref[cs[g]].astype(BF16)) for g in n]
        i0 = [_dot((s0[g] * cst_ref[0]).astype(BF16), v[g]) for g in n]
        i1 = [_dot((s1[g] * cst_ref[1]).astype(BF16), v[g]) for g in n]
        o = [jnp.where(lo, i0[g], i1[g]) + cross[g] for g in n]
        mu = [_seg_mean_mxu(x, bmat, 2) for x in o]
        d = [o[g] - mu[g] for g in n]
        var = [_seg_mean_mxu(x * x, bmat, 1) for x in d]
        for g in n:
            y = d[g] * lax.rsqrt(var[g] + EPS) * gn_ref[...]
            o_ref[rows_of(cs[g]), :] = (y * _silu(rg_ref[rows_of(cs[g]), :].astype(F32))).astype(BF16)
        return carry

    lax.fori_loop(0, nc // group, phase_c, 0)


def _retention(z, dec, gn, seq, latent, cos=None, sin=None, s0=None, layer=None):
    t = z.shape[0]
    b = t // seq
    nc = seq // CHUNK
    npair = H_RET // 2

    def zcol(base):
        return pl.BlockSpec((seq, LANES), lambda bi, hp: (bi, base // LANES + hp))

    in_specs = [
        pl.BlockSpec(memory_space=pltpu.SMEM),
        zcol(COL_RQ), zcol(COL_RK), zcol(COL_RV), zcol(COL_RG),
        pl.BlockSpec((1, LANES), lambda bi, hp: (0, hp)),
    ]
    args = [dec, z, z, z, z, gn.reshape(1, RET_W)]
    out_shape = [jax.ShapeDtypeStruct((t, RET_W), BF16)]
    out_specs = [pl.BlockSpec((seq, LANES), lambda bi, hp: (bi, hp))]
    if latent:
        in_specs += [
            pl.BlockSpec((seq, LANES), lambda bi, hp: (0, 0)),
            pl.BlockSpec((seq, LANES), lambda bi, hp: (0, 0)),
            pl.BlockSpec((1, 1, 2, 2, HEAD_DIM, HEAD_DIM), lambda bi, hp: (bi, layer, 0, hp, 0, 0)),
        ]
        args += [cos, sin, s0]
    else:
        out_shape.append(jax.ShapeDtypeStruct((b, 2, H_RET, HEAD_DIM, HEAD_DIM), F32))
        out_specs.append(pl.BlockSpec((1, 2, 2, HEAD_DIM, HEAD_DIM), lambda bi, hp: (bi, 0, hp, 0, 0)))
    res = pl.pallas_call(
        functools.partial(_ret_kernel, nc=nc, latent=latent),
        grid=(b, npair),
        in_specs=in_specs,
        out_specs=out_specs,
        out_shape=out_shape,
        scratch_shapes=[pltpu.VMEM((6, CHUNK, LANES), F32), pltpu.VMEM((nc, 2 * CHUNK, LANES), F32)]
        + ([pltpu.VMEM((seq, LANES), F32)] if latent else []),
        compiler_params=_cparams(("parallel", "arbitrary")),
        name="retention_latent" if latent else "retention_ctx",
    )(*args)
    return res if not latent else res[0]


def _att_kernel(*refs, tq, seq, latent):
    if latent:
        (q_ref, ak_ref, av_ref, ck_ref, cv_ref, gq_ref, gk_ref, cosq_ref, sinq_ref, cosk_ref, sink_ref,
         o_ref, k0_ref, k1_ref, vv_ref, *sp_refs) = refs
        off = PAST_LEN
    else:
        (q_ref, ak_ref, av_ref, gq_ref, gk_ref,
         o_ref, knew_ref, vnew_ref, k0_ref, k1_ref, vv_ref, *sp_refs) = refs
        off = 0

    @pl.when(pl.program_id(1) == 0)
    def _():
        lo = _lo_mask((seq, LANES))
        ak = ak_ref[...].astype(F32)
        kn = ak * lax.rsqrt(_seg_mean(ak * ak, lo) + EPS) * gk_ref[...]
        av = av_ref[...]
        if latent:
            kn = _rope(kn, cosk_ref[...], sink_ref[...])
            lo_c = _lo_mask((PAST_LEN, LANES))
            ck = ck_ref[0, 0]
            k0_ref[:, 0:off] = jnp.where(lo_c, ck, 0.0).T.astype(BF16)
            k1_ref[:, 0:off] = jnp.where(lo_c, 0.0, ck).T.astype(BF16)
            vv_ref[0:off, :] = cv_ref[0, 0].astype(BF16)
        else:
            knew_ref[...] = kn
            vnew_ref[...] = av.astype(F32)
        k0_ref[:, off:off + seq] = jnp.where(lo, kn, 0.0).T.astype(BF16)
        k1_ref[:, off:off + seq] = jnp.where(lo, 0.0, kn).T.astype(BF16)
        vv_ref[off:off + seq, :] = av

    sk = off + seq
    nkt = sk // MXU_N
    lo = _lo_mask((CHUNK, LANES))
    krefs = (k0_ref, k1_ref)
    units = [(sub, r, g) for sub in range(tq // CHUNK) for r in range(ATT_Q_W // LANES) for g in range(H_KV)]
    qb, m_lane, m_row, l_lane, o_acc, done = {}, {}, {}, {}, {}, {}

    def prep(sub, r):
        rows = slice(sub * CHUNK, (sub + 1) * CHUNK)
        x = q_ref[rows, r * LANES:(r + 1) * LANES].astype(F32)
        xn = x * lax.rsqrt(_seg_mean(x * x, lo) + EPS) * gq_ref[...]
        if latent:
            xn = _rope(xn, cosq_ref[rows, :], sinq_ref[rows, :])
        qb[(sub, r)] = (xn * (HEAD_DIM ** -0.5)).astype(BF16)

    def score_tile(i, kt):
        sub, r, g = units[i]
        cols = slice(kt * MXU_N, (kt + 1) * MXU_N)
        s_t = _dot(qb[(sub, r)], krefs[g][:, cols])
        sp_refs[i % ATT_BUFS][:, cols] = s_t
        t = jnp.maximum(s_t[:, :LANES], s_t[:, LANES:])
        m_lane[i] = t if kt == 0 else jnp.maximum(m_lane[i], t)

    def soft_tile(i, kt):
        cols = slice(kt * MXU_N, (kt + 1) * MXU_N)
        sbuf = sp_refs[i % ATT_BUFS]
        p0 = jnp.exp(sbuf[:, kt * MXU_N:kt * MXU_N + LANES] - m_row[i])
        p1 = jnp.exp(sbuf[:, kt * MXU_N + LANES:(kt + 1) * MXU_N] - m_row[i])
        pv = _dot(jnp.concatenate([p0, p1], axis=1).astype(BF16), vv_ref[cols, :])
        if kt == 0:
            l_lane[i], o_acc[i] = p0 + p1, pv
        else:
            l_lane[i], o_acc[i] = l_lane[i] + (p0 + p1), o_acc[i] + pv

    def finish(i):
        sub, r, g = units[i]
        done[(sub, r, g)] = o_acc.pop(i) / jnp.sum(l_lane.pop(i), axis=-1, keepdims=True)
        if g == H_KV - 1:
            rows = slice(sub * CHUNK, (sub + 1) * CHUNK)
            o = jnp.where(lo, done.pop((sub, r, 0)), done.pop((sub, r, 1)))
            o_ref[rows, r * LANES:(r + 1) * LANES] = o.astype(BF16)

    prep(units[0][0], units[0][1])
    for s in range(len(units) + ATT_LAG):
        nxt = s + H_KV
        if s % H_KV == 0 and nxt < len(units):
            prep(units[nxt][0], units[nxt][1])
        for kt in range(nkt):
            if s < len(units):
                score_tile(s, kt)
            if s >= ATT_LAG:
                soft_tile(s - ATT_LAG, kt)
        if s < len(units):
            m_row[s] = jnp.broadcast_to(jnp.max(m_lane.pop(s), axis=-1, keepdims=True), (CHUNK, LANES))
        if s >= ATT_LAG:
            finish(s - ATT_LAG)


def _attention(z, gq, gk, seq, latent, cos=None, sin=None, cache_k=None, cache_v=None, layer=None):
    t = z.shape[0]
    b = t // seq
    tq = 128 if latent else seq
    nq = seq // tq
    sk = seq + (PAST_LEN if latent else 0)
    kv_spec = lambda base: pl.BlockSpec((seq, LANES), lambda bi, qi: (bi, base // LANES))
    vec_spec = pl.BlockSpec((1, LANES), lambda bi, qi: (0, 0))
    in_specs = [
        pl.BlockSpec((tq, ATT_Q_W), lambda bi, qi: (bi * nq + qi, COL_AQ // ATT_Q_W)),
        kv_spec(COL_AK), kv_spec(COL_AV),
    ]
    args = [z, z, z]
    if latent:
        cache_spec = pl.BlockSpec((1, 1, PAST_LEN, LANES), lambda bi, qi: (bi, layer, 0, 0))
        in_specs += [cache_spec, cache_spec]
        args += [cache_k, cache_v]
    in_specs += [vec_spec, vec_spec]
    args += [gq, gk]
    out_shape = [jax.ShapeDtypeStruct((t, ATT_Q_W), BF16)]
    out_specs = [pl.BlockSpec((tq, ATT_Q_W), lambda bi, qi: (bi * nq + qi, 0))]
    if latent:
        in_specs += [
            pl.BlockSpec((tq, LANES), lambda bi, qi: (qi, 0)),
            pl.BlockSpec((tq, LANES), lambda bi, qi: (qi, 0)),
            pl.BlockSpec((seq, LANES), lambda bi, qi: (0, 0)),
            pl.BlockSpec((seq, LANES), lambda bi, qi: (0, 0)),
        ]
        args += [cos, sin, cos, sin]
    else:
        new_spec = pl.BlockSpec((seq, LANES), lambda bi, qi: (bi, 0))
        out_shape += [jax.ShapeDtypeStruct((t, LANES), F32)] * 2
        out_specs += [new_spec, new_spec]
    res = pl.pallas_call(
        functools.partial(_att_kernel, tq=tq, seq=seq, latent=latent),
        grid=(b, nq),
        in_specs=in_specs,
        out_specs=out_specs,
        out_shape=out_shape,
        scratch_shapes=([pltpu.VMEM((LANES, sk), BF16)] * 2 + [pltpu.VMEM((sk, LANES), BF16)]
                        + [pltpu.VMEM((CHUNK, sk), F32)] * ATT_BUFS),
        compiler_params=_cparams(("parallel", "arbitrary")),
        name="attention_latent" if latent else "attention_ctx",
    )(*args)
    return res if not latent else res[0]


def _merge_kernel(x_ref, gr_ref, ga_ref, gc_ref, cu_ref, cv_ref, ro_ref, ao_ref,
                  wr_ref, wa_ref, wc_ref, wo_ref, cmn_ref, ws_ref, bs_ref, g1_ref, o_ref, *, tm):
    u = jax.nn.gelu(cu_ref[...].astype(F32), approximate=True)
    v = jax.nn.gelu(cv_ref[...].astype(F32), approximate=True)
    vn = v * lax.rsqrt(jnp.mean(v * v, axis=-1, keepdims=True) + EPS) * cmn_ref[...]
    vb = vn.astype(BF16)
    rows = []
    for n in range(tm // CHUNK):
        cols = []
        for g in range(CM_GROUPS):
            vg = vb[n * CHUNK:(n + 1) * CHUNK, g * LANES:(g + 1) * LANES]
            cols.append(_dot(ws_ref[g], vg) + bs_ref[:, g:g + 1])
        rows.append(jnp.concatenate(cols, axis=1))
    cm = (u * jnp.concatenate(rows, axis=0)).astype(BF16)
    merged = (_sigmoid(gr_ref[...].astype(F32)) * _dot(ro_ref[...], wr_ref[...])
              + _sigmoid(ga_ref[...].astype(F32)) * _dot(ao_ref[...], wa_ref[...])
              + _sigmoid(gc_ref[...].astype(F32)) * _dot(cm, wc_ref[...]))
    y = _dot(merged.astype(BF16), wo_ref[...])
    o_ref[...] = x_ref[...] + g1_ref[0] * y


def _merge(x, z, ret_o, att_o, wr, wa, wc, wo, cmn, ws, bs_t, g1, layer, seq, latent):
    t = x.shape[0]
    tm = 512
    mrow = _mod_row_fn(tm, seq, latent)
    row_spec = lambda w, cb: pl.BlockSpec((tm, w), lambda i: (i, cb))
    full = lambda shape: pl.BlockSpec(shape, lambda i: (0,) * len(shape))
    of_layer = lambda shape: pl.BlockSpec((None,) + shape, lambda i: (layer,) + (0,) * len(shape))
    return pl.pallas_call(
        functools.partial(_merge_kernel, tm=tm),
        grid=(t // tm,),
        in_specs=[
            row_spec(D_MODEL, 0),
            row_spec(D_MODEL, COL_GR // D_MODEL), row_spec(D_MODEL, COL_GA // D_MODEL),
            row_spec(D_MODEL, COL_GC // D_MODEL),
            row_spec(CM_W, COL_CU // CM_W), row_spec(CM_W, COL_CV // CM_W),
            row_spec(RET_W, 0), row_spec(ATT_Q_W, 0),
            of_layer((RET_W, D_MODEL)), of_layer((ATT_Q_W, D_MODEL)), of_layer((CM_W, D_MODEL)),
            of_layer((D_MODEL, D_MODEL)),
            full((1, CM_W)), of_layer((CM_GROUPS, CHUNK, CHUNK)), full((CHUNK, CM_GROUPS)),
            pl.BlockSpec((1, 1, D_MODEL), lambda i: (mrow(i), 0, 0)),
        ],
        out_specs=row_spec(D_MODEL, 0),
        out_shape=jax.ShapeDtypeStruct((t, D_MODEL), F32),
        compiler_params=_cparams(("parallel",)),
        name="merge",
    )(x, z, z, z, z, z, ret_o, att_o, wr, wa, wc, wo, cmn, ws, bs_t, g1)


def _ffn_kernel(x_ref, g_ref, sc_ref, sh_ref, w1_ref, w2_ref, g2_ref, gf_ref, o_ref, h_ref, acc_ref, *, final):
    k = pl.program_id(1)

    @pl.when(k == 0)
    def _():
        h_ref[...] = _norm_mod(x_ref[...], g_ref[...], sc_ref[0], sh_ref[0]).astype(BF16)
        acc_ref[...] = jnp.zeros_like(acc_ref)

    a = jnp.square(jnp.maximum(_dot(h_ref[...], w1_ref[...]), 0.0))
    acc_ref[...] += _dot(a.astype(BF16), w2_ref[...])

    @pl.when(k == pl.num_programs(1) - 1)
    def _():
        xn = x_ref[...] + g2_ref[0] * acc_ref[...]
        if final:
            xn = xn * lax.rsqrt(jnp.mean(xn * xn, axis=-1, keepdims=True) + EPS) * gf_ref[...]
        o_ref[...] = xn


def _ffn(x, g, sc, sh, w1, w2, g2, gfin, layer, seq, latent, final):
    t = x.shape[0]
    tm, tf = 1024, 1024
    mrow = _mod_row_fn(tm, seq, latent)
    mod_spec = pl.BlockSpec((1, 1, D_MODEL), lambda i, k: (mrow(i), 0, 0))
    vec_spec = pl.BlockSpec((1, D_MODEL), lambda i, k: (0, 0))
    return pl.pallas_call(
        functools.partial(_ffn_kernel, final=final),
        grid=(t // tm, D_FF // tf),
        in_specs=[
            pl.BlockSpec((tm, D_MODEL), lambda i, k: (i, 0)),
            vec_spec, mod_spec, mod_spec,
            pl.BlockSpec((None, D_MODEL, tf), lambda i, k: (layer, 0, k)),
            pl.BlockSpec((None, tf, D_MODEL), lambda i, k: (layer, k, 0)),
            mod_spec, vec_spec,
        ],
        out_specs=pl.BlockSpec((tm, D_MODEL), lambda i, k: (i, 0)),
        out_shape=jax.ShapeDtypeStruct((t, D_MODEL), F32),
        scratch_shapes=[pltpu.VMEM((tm, D_MODEL), BF16), pltpu.VMEM((tm, D_MODEL), F32)],
        compiler_params=_cparams(("parallel", "arbitrary")),
        name="ffn_final" if final else "ffn",
    )(x, g, sc, sh, w1, w2, g2, gfin)


def _rope_tables(length):
    rows = length // GRID_W
    row = jnp.repeat(jnp.arange(rows, dtype=F32), GRID_W)
    col = jnp.tile(jnp.arange(GRID_W, dtype=F32), rows)
    half = HEAD_DIM // 2
    inv = 1.0 / (ROPE_BASE ** (jnp.arange(0, half, 2, dtype=F32) / half))
    ang = jnp.concatenate([row[:, None] * inv[None, :], col[:, None] * inv[None, :]], axis=-1)
    cos, sin = jnp.cos(ang), jnp.sin(ang)
    cos2 = jnp.repeat(cos, 2, axis=-1)
    sin2 = jnp.stack([-sin, sin], axis=-1).reshape(length, HEAD_DIM)
    return jnp.tile(cos2, (1, 2)), jnp.tile(sin2, (1, 2))


def _permute_w_in(w_in):
    o = np.cumsum((0, 512, 512, 512, 512, 512, 128, 128, 512, 512, 1024, 1024, 1024))
    seg = lambda i: w_in[:, :, int(o[i]):int(o[i + 1])]
    rq, rk, rv, rg, aq, ak, av, cu, cv, gr, ga, gc = (seg(i) for i in range(12))
    aq = aq.reshape(DEPTH, D_MODEL, H_Q, HEAD_DIM)[:, :, np.array(Q_HEAD_ORDER)].reshape(DEPTH, D_MODEL, ATT_Q_W)
    return jnp.concatenate([gr, ga, gc, rq, rk, rv, rg, aq, cu, cv, ak, av], axis=-1).astype(BF16)


def kernel(x_prompt, x_sample, cache_k, cache_v, state_ret, c, c_ctx, w_mod, b_mod, g_norm1, g_norm2, w_in, g_q, g_k, ret_decay, ret_gn, cm_norm, cm_ws, cm_bs, w_br_ret, w_br_att, w_br_cm, w_out, w_ff1, w_ff2, g_final):
    batch, dec_batch = x_prompt.shape[0], x_sample.shape[0]

    cond = jnp.concatenate(
        [c_ctx[None, :], c, jnp.zeros((N_MOD_ROWS - 1 - dec_batch, D_MODEL), F32)], axis=0)
    mod = _modulation(cond, w_mod, b_mod)
    mod = mod.reshape(DEPTH, N_MOD_ROWS, 6, D_MODEL).transpose(0, 2, 1, 3).reshape(DEPTH, 6, N_MOD_ROWS, 1, D_MODEL)

    w_in_p = _permute_w_in(w_in)
    w_br_ret_b = w_br_ret.astype(BF16)
    w_br_att_b = w_br_att.reshape(DEPTH, H_Q, HEAD_DIM, D_MODEL)[:, np.array(Q_HEAD_ORDER)].reshape(
        DEPTH, ATT_Q_W, D_MODEL).astype(BF16)
    w_br_cm_b = w_br_cm.astype(BF16)
    w_out_b = w_out.astype(BF16)
    w_ff1_b = w_ff1.astype(BF16)
    w_ff2_b = w_ff2.astype(BF16)
    cm_ws_b = cm_ws.astype(BF16)
    cm_bs_t = jnp.swapaxes(cm_bs, 1, 2)
    g_q2 = jnp.tile(g_q, (1, 2)).reshape(DEPTH, 1, LANES)
    g_k2 = jnp.tile(g_k, (1, 2)).reshape(DEPTH, 1, LANES)
    cos, sin = _rope_tables(DEC_SEQ)
    cache_k2 = cache_k.reshape(dec_batch, DEPTH, PAST_LEN, LANES)
    cache_v2 = cache_v.reshape(dec_batch, DEPTH, PAST_LEN, LANES)
    g_fin = g_final.reshape(1, D_MODEL)

    xp = x_prompt.reshape(batch * SEQ, D_MODEL)
    xs = x_sample.reshape(dec_batch * DEC_SEQ, D_MODEL)
    ks_list, vs_list, ss_list = [], [], []
    for l in range(DEPTH):
        sh1, sc1, g1, sh2, sc2, g2 = (mod[l, i] for i in range(6))
        gn1 = g_norm1[l].reshape(1, D_MODEL)
        gn2 = g_norm2[l].reshape(1, D_MODEL)
        final = l == DEPTH - 1
        for latent in (False, True):
            x = xs if latent else xp
            seq = DEC_SEQ if latent else SEQ
            z = _in_proj(x, gn1, sc1, sh1, w_in_p, l, seq, latent)
            if latent:
                ret_o = _retention(z, ret_decay[l], ret_gn[l], seq, True, cos, sin, state_ret, l)
                att_o = _attention(z, g_q2[l], g_k2[l], seq, True, cos, sin, cache_k2, cache_v2, l)
            else:
                ret_o, s_fin = _retention(z, ret_decay[l], ret_gn[l], seq, False)
                att_o, k_new, v_new = _attention(z, g_q2[l], g_k2[l], seq, False)
                ks_list.append(k_new.reshape(batch, SEQ, H_KV, HEAD_DIM))
                vs_list.append(v_new.reshape(batch, SEQ, H_KV, HEAD_DIM))
                ss_list.append(s_fin)
            x = _merge(x, z, ret_o, att_o, w_br_ret_b, w_br_att_b, w_br_cm_b, w_out_b,
                       cm_norm[l].reshape(1, CM_W), cm_ws_b, cm_bs_t[l], g1, l, seq, latent)
            x = _ffn(x, gn2, sc2, sh2, w_ff1_b, w_ff2_b, g2, g_fin, l, seq, latent, final)
            if latent:
                xs = x
            else:
                xp = x
    y_prompt = xp.reshape(batch, SEQ, D_MODEL)
    y_sample = xs.reshape(dec_batch, DEC_SEQ, D_MODEL)
    return (y_prompt, y_sample, jnp.stack(ks_list, axis=1), jnp.stack(vs_list, axis=1),
            jnp.stack(ss_list, axis=1))
```

```python
import functools

import numpy as np
import jax
import jax.numpy as jnp
from jax import lax
from jax.experimental import pallas as pl
from jax.experimental.pallas import tpu as pltpu

D_MODEL = 1024
DEPTH = 2
SEQ = 256
DEC_SEQ = 2048
PAST_LEN = 256
GRID_W = 64
HEAD_DIM = 64
H_RET = 8
H_Q = 8
H_KV = 2
CHUNK = 128
CM_GROUPS = 4
RET_W = 512
ATT_Q_W = 512
ATT_KV_W = 128
CM_W = 512
D_FF = 4096
ROPE_BASE = 10000.0
EPS = 1e-6
IN_W = 6912
N_MOD_ROWS = 16

LANES = 128
MXU_N = 256
RET_GROUP = 8
ATT_LAG = 2
ATT_BUFS = ATT_LAG + 1
VMEM_LIMIT = 48 * 1024 * 1024

F32 = jnp.float32
BF16 = jnp.bfloat16

COL_GR, COL_GA, COL_GC = 0, 1024, 2048
COL_RQ, COL_RK, COL_RV, COL_RG = 3072, 3584, 4096, 4608
COL_AQ, COL_CU, COL_CV = 5120, 5632, 6144
COL_AK, COL_AV = 6656, 6784
Q_HEAD_ORDER = (0, 4, 1, 5, 2, 6, 3, 7)


def _cparams(sem):
    return pltpu.CompilerParams(dimension_semantics=sem, vmem_limit_bytes=VMEM_LIMIT)


def _dot(a, b):
    return jnp.dot(a, b, preferred_element_type=F32)


def _dot_nt(a, b):
    return lax.dot_general(a, b, (((1,), (1,)), ((), ())), preferred_element_type=F32)


def _dot_tn(a, b):
    return lax.dot_general(a, b, (((0,), (0,)), ((), ())), preferred_element_type=F32)


def _lo_mask(shape):
    return lax.broadcasted_iota(jnp.int32, shape, len(shape) - 1) < HEAD_DIM


def _seg_mean(x, lo):
    s_lo = jnp.sum(jnp.where(lo, x, 0.0), axis=-1, keepdims=True)
    s_hi = jnp.sum(jnp.where(lo, 0.0, x), axis=-1, keepdims=True)
    return jnp.where(lo, s_lo, s_hi) * (1.0 / HEAD_DIM)


def _rope(x, cos, sin_signed):
    nxt = pltpu.roll(x, LANES - 1, 1)
    prv = pltpu.roll(x, 1, 1)
    even = (lax.broadcasted_iota(jnp.int32, x.shape, 1) & 1) == 0
    return x * cos + jnp.where(even, nxt, prv) * sin_signed


def _sigmoid(x):
    return 0.5 * jnp.tanh(0.5 * x) + 0.5


def _silu(x):
    return x * _sigmoid(x)


def _seg_mean_mxu(x, bmat, passes):
    hi = x.astype(BF16)
    out = _dot(hi, bmat)
    if passes == 2:
        out = out + _dot((x - hi.astype(F32)).astype(BF16), bmat)
    return out


def _norm_mod(x, g, sc, sh):
    ms = jnp.mean(x * x, axis=-1, keepdims=True)
    y = x * lax.rsqrt(ms + EPS) * g
    return y * (1.0 + sc) + sh


def _mod_kernel(c_ref, w_ref, b_ref, o_ref):
    s = _silu(c_ref[...])
    o_ref[0] = jnp.dot(s, w_ref[0], preferred_element_type=F32,
                       precision=lax.Precision.HIGHEST) + b_ref[0]


def _modulation(cond, w_mod, b_mod):
    tn = 1536
    n = 6 * D_MODEL
    return pl.pallas_call(
        _mod_kernel,
        grid=(DEPTH, n // tn),
        in_specs=[
            pl.BlockSpec((N_MOD_ROWS, D_MODEL), lambda l, j: (0, 0)),
            pl.BlockSpec((1, D_MODEL, tn), lambda l, j: (l, 0, j)),
            pl.BlockSpec((1, 1, tn), lambda l, j: (l, 0, j)),
        ],
        out_specs=pl.BlockSpec((1, N_MOD_ROWS, tn), lambda l, j: (l, 0, j)),
        out_shape=jax.ShapeDtypeStruct((DEPTH, N_MOD_ROWS, n), F32),
        compiler_params=_cparams(("arbitrary", "arbitrary")),
        name="modulation",
    )(cond, w_mod, b_mod.reshape(DEPTH, 1, n))


def _mod_row_fn(tm, seq, latent):
    if latent:
        return lambda i: 1 + (i * tm) // seq
    return lambda i: 0


def _proj_kernel(x_ref, g_ref, sc_ref, sh_ref, w_ref, o_ref, h_ref):
    @pl.when(pl.program_id(1) == 0)
    def _():
        h_ref[...] = _norm_mod(x_ref[...], g_ref[...], sc_ref[0], sh_ref[0]).astype(BF16)

    o_ref[...] = _dot(h_ref[...], w_ref[...]).astype(BF16)


def _in_proj(x, g, sc, sh, w, layer, seq, latent):
    t = x.shape[0]
    tm, tn = 1024, 2304
    mrow = _mod_row_fn(tm, seq, latent)
    return pl.pallas_call(
        _proj_kernel,
        grid=(t // tm, IN_W // tn),
        in_specs=[
            pl.BlockSpec((tm, D_MODEL), lambda i, j: (i, 0)),
            pl.BlockSpec((1, D_MODEL), lambda i, j: (0, 0)),
            pl.BlockSpec((1, 1, D_MODEL), lambda i, j: (mrow(i), 0, 0)),
            pl.BlockSpec((1, 1, D_MODEL), lambda i, j: (mrow(i), 0, 0)),
            pl.BlockSpec((None, D_MODEL, tn), lambda i, j: (layer, 0, j)),
        ],
        out_specs=pl.BlockSpec((tm, tn), lambda i, j: (i, j)),
        out_shape=jax.ShapeDtypeStruct((t, IN_W), BF16),
        scratch_shapes=[pltpu.VMEM((tm, D_MODEL), BF16)],
        compiler_params=_cparams(("parallel", "arbitrary")),
        name="in_proj",
    )(x, g, sc, sh, w)


def _ret_kernel(*refs, nc, latent):
    if latent:
        (dec_ref, q_ref, k_ref, v_ref, rg_ref, gn_ref, cos_ref, sin_ref, s0_ref,
         o_ref, cst_ref, uu_ref, kr_ref) = refs
    else:
        (dec_ref, q_ref, k_ref, v_ref, rg_ref, gn_ref,
         o_ref, sfin_ref, cst_ref, uu_ref) = refs
    hp = pl.program_id(0)
    sq = (CHUNK, LANES)
    lane = lax.broadcasted_iota(jnp.int32, sq, 1)
    row = lax.broadcasted_iota(jnp.int32, sq, 0)
    lo = lane < HEAD_DIM
    blockdiag = (row < HEAD_DIM) == lo
    bmat = jnp.where(blockdiag, 1.0 / HEAD_DIM, 0.0).astype(BF16)
    @pl.when(pl.program_id(1) == 0)
    def _():
        rowf = row.astype(F32)
        rel = (row - lane).astype(F32)

        def log_decay(direction, head):
            d = jnp.full(sq, dec_ref[direction, head], F32)
            return -(jnp.maximum(d, 0.0) + jnp.log1p(jnp.exp(-jnp.abs(d))))

        ldf0, ldf1 = log_decay(0, 2 * hp), log_decay(0, 2 * hp + 1)
        ldb0, ldb1 = log_decay(1, 2 * hp), log_decay(1, 2 * hp + 1)
        ldf = jnp.where(lo, ldf0, ldf1)
        ldb = jnp.where(lo, ldb0, ldb1)

        def decay_mask(lf, lb):
            fwd = jnp.exp(lf * jnp.maximum(rel, 0.0))
            bwd = jnp.exp(lb * jnp.maximum(-rel, 0.0))
            return jnp.where(rel > 0, fwd, jnp.where(rel < 0, bwd, 2.0))

        cst_ref[0] = decay_mask(ldf0, ldb0)
        cst_ref[1] = decay_mask(ldf1, ldb1)
        cst_ref[2] = jnp.exp(ldf * (CHUNK - 1.0 - rowf))
        cst_ref[3] = jnp.exp(ldb * rowf)
        cst_ref[4] = jnp.exp(ldf * (rowf + 1.0))
        cst_ref[5] = jnp.exp(ldb * (CHUNK - rowf))
        cst_ref[6] = jnp.exp(ldf * float(CHUNK))
        cst_ref[7] = jnp.exp(ldb * float(CHUNK))

    def rows_of(c):
        return pl.ds(pl.multiple_of(c * CHUNK, CHUNK), CHUNK)

    def load_qk(ref, c, scale):
        x = ref[rows_of(c), :].astype(F32)
        if scale != 1.0:
            x = x * scale
        if latent:
            x = _rope(x, cos_ref[rows_of(c), :], sin_ref[rows_of(c), :])
        return x

    group = min(nc, RET_GROUP)

    def phase_a(i, carry):
        cs = [i * group + g for g in range(group)]
        ks = [load_qk(k_ref, c, HEAD_DIM ** -0.5) for c in cs]
        if latent:
            for c, k in zip(cs, ks):
                kr_ref[rows_of(c), :] = k
        kzs = [jnp.concatenate([(k * cst_ref[2]).astype(BF16), (k * cst_ref[3]).astype(BF16)], axis=1)
               for k in ks]
        for c, kz in zip(cs, kzs):
            uu_ref[c] = _dot_tn(kz, v_ref[rows_of(c), :])
        return carry

    lax.fori_loop(0, nc // group, phase_a, 0)

    if latent:
        zeros = jnp.zeros((HEAD_DIM, HEAD_DIM), F32)

        def embed(direction):
            top = jnp.concatenate([s0_ref[0, 0, direction, 0], zeros], axis=1)
            bot = jnp.concatenate([zeros, s0_ref[0, 0, direction, 1]], axis=1)
            return jnp.concatenate([top, bot], axis=0)

        sf0, sb0 = embed(0), embed(1)
    else:
        sf0 = sb0 = jnp.zeros(sq, F32)

    def scan_f(c, s):
        u = uu_ref[c, 0:CHUNK, :]
        uu_ref[c, 0:CHUNK, :] = s
        return cst_ref[6] * s + jnp.where(blockdiag, u, 0.0)

    def scan_b(i, s):
        c = nc - 1 - i
        u = uu_ref[c, CHUNK:2 * CHUNK, :]
        uu_ref[c, CHUNK:2 * CHUNK, :] = s
        return cst_ref[7] * s + jnp.where(blockdiag, u, 0.0)

    sf = lax.fori_loop(0, nc, scan_f, sf0, unroll=True)
    sb = lax.fori_loop(0, nc, scan_b, sb0, unroll=True)
    if not latent:
        for d, s in ((0, sf), (1, sb)):
            sfin_ref[0, d, 0] = s[0:HEAD_DIM, 0:HEAD_DIM]
            sfin_ref[0, d, 1] = s[HEAD_DIM:, HEAD_DIM:]

    def phase_c(i, carry):
        cs = [i * group + g for g in range(group)]
        n = range(group)
        q = [load_qk(q_ref, c, 1.0) for c in cs]
        k = [kr_ref[rows_of(c), :] if latent else load_qk(k_ref, c, HEAD_DIM ** -0.5) for c in cs]
        v = [v_ref[rows_of(c), :] for c in cs]
        qb = [x.astype(BF16) for x in q]
        s0 = [_dot_nt(qb[g], jnp.where(lo, k[g], 0.0).astype(BF16)) for g in n]
        s1 = [_dot_nt(qb[g], jnp.where(lo, 0.0, k[g]).astype(BF16)) for g in n]
        qx = [jnp.concatenate([(x * cst_ref[4]).astype(BF16), (x * cst_ref[5]).astype(BF16)], axis=1) for x in q]
        cross = [_dot(qx[g], uu_ref[cs[g]].astype(BF16)) for g in n]
        i0 = [_dot((s0[g] * cst_ref[0]).astype(BF16), v[g]) for g in n]
        i1 = [_dot((s1[g] * cst_ref[1]).astype(BF16), v[g]) for g in n]
        o = [jnp.where(lo, i0[g], i1[g]) + cross[g] for g in n]
        mu = [_seg_mean_mxu(x, bmat, 2) for x in o]
        d = [o[g] - mu[g] for g in n]
        var = [_seg_mean_mxu(x * x, bmat, 1) for x in d]
        for g in n:
            y = d[g] * lax.rsqrt(var[g] + EPS) * gn_ref[...]
            o_ref[rows_of(cs[g]), :] = (y * _silu(rg_ref[rows_of(cs[g]), :].astype(F32))).astype(BF16)
        return carry

    lax.fori_loop(0, nc // group, phase_c, 0)


def _retention(z, dec, gn, seq, latent, cos=None, sin=None, s0=None, layer=None):
    t = z.shape[0]
    b = t // seq
    nc = seq // CHUNK
    npair = H_RET // 2

    def zcol(base):
        return pl.BlockSpec((seq, LANES), lambda hp, bi: (bi, base // LANES + hp))

    in_specs = [
        pl.BlockSpec(memory_space=pltpu.SMEM),
        zcol(COL_RQ), zcol(COL_RK), zcol(COL_RV), zcol(COL_RG),
        pl.BlockSpec((1, LANES), lambda hp, bi: (0, hp)),
    ]
    args = [dec, z, z, z, z, gn.reshape(1, RET_W)]
    out_shape = [jax.ShapeDtypeStruct((t, RET_W), BF16)]
    out_specs = [pl.BlockSpec((seq, LANES), lambda hp, bi: (bi, hp))]
    if latent:
        in_specs += [
            pl.BlockSpec((seq, LANES), lambda hp, bi: (0, 0)),
            pl.BlockSpec((seq, LANES), lambda hp, bi: (0, 0)),
            pl.BlockSpec((1, 1, 2, 2, HEAD_DIM, HEAD_DIM), lambda hp, bi: (bi, layer, 0, hp, 0, 0)),
        ]
        args += [cos, sin, s0]
    else:
        out_shape.append(jax.ShapeDtypeStruct((b, 2, H_RET, HEAD_DIM, HEAD_DIM), F32))
        out_specs.append(pl.BlockSpec((1, 2, 2, HEAD_DIM, HEAD_DIM), lambda hp, bi: (bi, 0, hp, 0, 0)))
    res = pl.pallas_call(
        functools.partial(_ret_kernel, nc=nc, latent=latent),
        grid=(npair, b),
        in_specs=in_specs,
        out_specs=out_specs,
        out_shape=out_shape,
        scratch_shapes=[pltpu.VMEM((8, CHUNK, LANES), F32), pltpu.VMEM((nc, 2 * CHUNK, LANES), F32)]
        + ([pltpu.VMEM((seq, LANES), F32)] if latent else []),
        compiler_params=_cparams(("parallel", "arbitrary")),
        name="retention_latent" if latent else "retention_ctx",
    )(*args)
    return res if not latent else res[0]


def _att_kernel(*refs, tq, seq, latent):
    if latent:
        (q_ref, ak_ref, av_ref, ck_ref, cv_ref, gq_ref, gk_ref, cosq_ref, sinq_ref, cosk_ref, sink_ref,
         o_ref, k0_ref, k1_ref, vv_ref, *sp_refs) = refs
        off = PAST_LEN
    else:
        (q_ref, ak_ref, av_ref, gq_ref, gk_ref,
         o_ref, knew_ref, vnew_ref, k0_ref, k1_ref, vv_ref, *sp_refs) = refs
        off = 0

    @pl.when(pl.program_id(1) == 0)
    def _():
        lo = _lo_mask((seq, LANES))
        ak = ak_ref[...].astype(F32)
        kn = ak * lax.rsqrt(_seg_mean(ak * ak, lo) + EPS) * gk_ref[...]
        av = av_ref[...]
        if latent:
            kn = _rope(kn, cosk_ref[...], sink_ref[...])
            lo_c = _lo_mask((PAST_LEN, LANES))
            ck = ck_ref[0, 0]
            k0_ref[:, 0:off] = jnp.where(lo_c, ck, 0.0).T.astype(BF16)
            k1_ref[:, 0:off] = jnp.where(lo_c, 0.0, ck).T.astype(BF16)
            vv_ref[0:off, :] = cv_ref[0, 0].astype(BF16)
        else:
            knew_ref[...] = kn
            vnew_ref[...] = av.astype(F32)
        k0_ref[:, off:off + seq] = jnp.where(lo, kn, 0.0).T.astype(BF16)
        k1_ref[:, off:off + seq] = jnp.where(lo, 0.0, kn).T.astype(BF16)
        vv_ref[off:off + seq, :] = av

    sk = off + seq
    nkt = sk // MXU_N
    lo = _lo_mask((CHUNK, LANES))
    krefs = (k0_ref, k1_ref)
    units = [(sub, r, g) for sub in range(tq // CHUNK) for r in range(ATT_Q_W // LANES) for g in range(H_KV)]
    qb, m_lane, m_row, l_lane, o_acc, done = {}, {}, {}, {}, {}, {}

    def prep(sub, r):
        rows = slice(sub * CHUNK, (sub + 1) * CHUNK)
        x = q_ref[rows, r * LANES:(r + 1) * LANES].astype(F32)
        xn = x * lax.rsqrt(_seg_mean(x * x, lo) + EPS) * gq_ref[...]
        if latent:
            xn = _rope(xn, cosq_ref[rows, :], sinq_ref[rows, :])
        qb[(sub, r)] = (xn * (HEAD_DIM ** -0.5)).astype(BF16)

    def score_tile(i, kt):
        sub, r, g = units[i]
        cols = slice(kt * MXU_N, (kt + 1) * MXU_N)
        s_t = _dot(qb[(sub, r)], krefs[g][:, cols])
        sp_refs[i % ATT_BUFS][:, cols] = s_t
        t = jnp.maximum(s_t[:, :LANES], s_t[:, LANES:])
        m_lane[i] = t if kt == 0 else jnp.maximum(m_lane[i], t)

    def soft_tile(i, kt):
        cols = slice(kt * MXU_N, (kt + 1) * MXU_N)
        sbuf = sp_refs[i % ATT_BUFS]
        p0 = jnp.exp(sbuf[:, kt * MXU_N:kt * MXU_N + LANES] - m_row[i])
        p1 = jnp.exp(sbuf[:, kt * MXU_N + LANES:(kt + 1) * MXU_N] - m_row[i])
        pv = _dot(jnp.concatenate([p0, p1], axis=1).astype(BF16), vv_ref[cols, :])
        if kt == 0:
            l_lane[i], o_acc[i] = p0 + p1, pv
        else:
            l_lane[i], o_acc[i] = l_lane[i] + (p0 + p1), o_acc[i] + pv

    def finish(i):
        sub, r, g = units[i]
        done[(sub, r, g)] = o_acc.pop(i) / jnp.sum(l_lane.pop(i), axis=-1, keepdims=True)
        if g == H_KV - 1:
            rows = slice(sub * CHUNK, (sub + 1) * CHUNK)
            o = jnp.where(lo, done.pop((sub, r, 0)), done.pop((sub, r, 1)))
            o_ref[rows, r * LANES:(r + 1) * LANES] = o.astype(BF16)

    prep(units[0][0], units[0][1])
    for s in range(len(units) + ATT_LAG):
        nxt = s + H_KV
        if s % H_KV == 0 and nxt < len(units):
            prep(units[nxt][0], units[nxt][1])
        for kt in range(nkt):
            if s < len(units):
                score_tile(s, kt)
            if s >= ATT_LAG:
                soft_tile(s - ATT_LAG, kt)
        if s < len(units):
            m_row[s] = jnp.broadcast_to(jnp.max(m_lane.pop(s), axis=-1, keepdims=True), (CHUNK, LANES))
        if s >= ATT_LAG:
            finish(s - ATT_LAG)


def _attention(z, gq, gk, seq, latent, cos=None, sin=None, cache_k=None, cache_v=None, layer=None):
    t = z.shape[0]
    b = t // seq
    tq = 256
    nq = seq // tq
    sk = seq + (PAST_LEN if latent else 0)
    kv_spec = lambda base: pl.BlockSpec((seq, LANES), lambda bi, qi: (bi, base // LANES))
    vec_spec = pl.BlockSpec((1, LANES), lambda bi, qi: (0, 0))
    in_specs = [
        pl.BlockSpec((tq, ATT_Q_W), lambda bi, qi: (bi * nq + qi, COL_AQ // ATT_Q_W)),
        kv_spec(COL_AK), kv_spec(COL_AV),
    ]
    args = [z, z, z]
    if latent:
        cache_spec = pl.BlockSpec((1, 1, PAST_LEN, LANES), lambda bi, qi: (bi, layer, 0, 0))
        in_specs += [cache_spec, cache_spec]
        args += [cache_k, cache_v]
    in_specs += [vec_spec, vec_spec]
    args += [gq, gk]
    out_shape = [jax.ShapeDtypeStruct((t, ATT_Q_W), BF16)]
    out_specs = [pl.BlockSpec((tq, ATT_Q_W), lambda bi, qi: (bi * nq + qi, 0))]
    if latent:
        in_specs += [
            pl.BlockSpec((tq, LANES), lambda bi, qi: (qi, 0)),
            pl.BlockSpec((tq, LANES), lambda bi, qi: (qi, 0)),
            pl.BlockSpec((seq, LANES), lambda bi, qi: (0, 0)),
            pl.BlockSpec((seq, LANES), lambda bi, qi: (0, 0)),
        ]
        args += [cos, sin, cos, sin]
    else:
        new_spec = pl.BlockSpec((seq, LANES), lambda bi, qi: (bi, 0))
        out_shape += [jax.ShapeDtypeStruct((t, LANES), F32)] * 2
        out_specs += [new_spec, new_spec]
    res = pl.pallas_call(
        functools.partial(_att_kernel, tq=tq, seq=seq, latent=latent),
        grid=(b, nq),
        in_specs=in_specs,
        out_specs=out_specs,
        out_shape=out_shape,
        scratch_shapes=([pltpu.VMEM((LANES, sk), BF16)] * 2 + [pltpu.VMEM((sk, LANES), BF16)]
                        + [pltpu.VMEM((CHUNK, sk), F32)] * ATT_BUFS),
        compiler_params=_cparams(("parallel", "arbitrary")),
        name="attention_latent" if latent else "attention_ctx",
    )(*args)
    return res if not latent else res[0]


def _merge_kernel(x_ref, gr_ref, ga_ref, gc_ref, cu_ref, cv_ref, ro_ref, ao_ref,
                  wr_ref, wa_ref, wc_ref, wo_ref, cmn_ref, ws_ref, bs_ref, g1_ref,
                  gn2_ref, sc2_ref, sh2_ref, o_ref, h_ref, *, tm):
    u = jax.nn.gelu(cu_ref[...], approximate=True).astype(F32)
    v = jax.nn.gelu(cv_ref[...], approximate=True).astype(F32)
    vn = v * lax.rsqrt(jnp.mean(v * v, axis=-1, keepdims=True) + EPS) * cmn_ref[...]
    vb = vn.astype(BF16)
    rows = []
    for n in range(tm // CHUNK):
        cols = []
        for g in range(CM_GROUPS):
            vg = vb[n * CHUNK:(n + 1) * CHUNK, g * LANES:(g + 1) * LANES]
            cols.append(_dot(ws_ref[g], vg) + bs_ref[:, g:g + 1])
        rows.append(jnp.concatenate(cols, axis=1))
    cm = (u * jnp.concatenate(rows, axis=0)).astype(BF16)
    merged = (_sigmoid(gr_ref[...]).astype(F32) * _dot(ro_ref[...], wr_ref[...])
              + _sigmoid(ga_ref[...]).astype(F32) * _dot(ao_ref[...], wa_ref[...])
              + _sigmoid(gc_ref[...]).astype(F32) * _dot(cm, wc_ref[...]))
    y = _dot(merged.astype(BF16), wo_ref[...])
    xn = x_ref[...] + g1_ref[0] * y
    o_ref[...] = xn
    h_ref[...] = _norm_mod(xn, gn2_ref[...], sc2_ref[0], sh2_ref[0]).astype(BF16)


def _merge(x, z, ret_o, att_o, wr, wa, wc, wo, cmn, ws, bs_t, g1, gn2, sc2, sh2, layer, seq, latent):
    t = x.shape[0]
    tm = 512
    mrow = _mod_row_fn(tm, seq, latent)
    row_spec = lambda w, cb: pl.BlockSpec((tm, w), lambda i: (i, cb))
    full = lambda shape: pl.BlockSpec(shape, lambda i: (0,) * len(shape))
    of_layer = lambda shape: pl.BlockSpec((None,) + shape, lambda i: (layer,) + (0,) * len(shape))
    mod_spec = pl.BlockSpec((1, 1, D_MODEL), lambda i: (mrow(i), 0, 0))
    return pl.pallas_call(
        functools.partial(_merge_kernel, tm=tm),
        grid=(t // tm,),
        in_specs=[
            row_spec(D_MODEL, 0),
            row_spec(D_MODEL, COL_GR // D_MODEL), row_spec(D_MODEL, COL_GA // D_MODEL),
            row_spec(D_MODEL, COL_GC // D_MODEL),
            row_spec(CM_W, COL_CU // CM_W), row_spec(CM_W, COL_CV // CM_W),
            row_spec(RET_W, 0), row_spec(ATT_Q_W, 0),
            of_layer((RET_W, D_MODEL)), of_layer((ATT_Q_W, D_MODEL)), of_layer((CM_W, D_MODEL)),
            of_layer((D_MODEL, D_MODEL)),
            full((1, CM_W)), of_layer((CM_GROUPS, CHUNK, CHUNK)), full((CHUNK, CM_GROUPS)),
            mod_spec, full((1, D_MODEL)), mod_spec, mod_spec,
        ],
        out_specs=[row_spec(D_MODEL, 0), row_spec(D_MODEL, 0)],
        out_shape=[jax.ShapeDtypeStruct((t, D_MODEL), F32), jax.ShapeDtypeStruct((t, D_MODEL), BF16)],
        compiler_params=_cparams(("parallel",)),
        name="merge",
    )(x, z, z, z, z, z, ret_o, att_o, wr, wa, wc, wo, cmn, ws, bs_t, g1, gn2, sc2, sh2)


def _ffn_kernel(x_ref, h_ref, w1_ref, w2_ref, g2_ref, gf_ref, o_ref, acc_ref, *, final):
    k = pl.program_id(1)
    a = jnp.square(jnp.maximum(_dot(h_ref[...], w1_ref[...]), 0.0))
    upd = _dot(a.astype(BF16), w2_ref[...])

    @pl.when(k == 0)
    def _():
        acc_ref[...] = upd

    @pl.when(k > 0)
    def _():
        acc_ref[...] += upd

    @pl.when(k == pl.num_programs(1) - 1)
    def _():
        xn = x_ref[...] + g2_ref[0] * acc_ref[...]
        if final:
            xn = xn * lax.rsqrt(jnp.mean(xn * xn, axis=-1, keepdims=True) + EPS) * gf_ref[...]
        o_ref[...] = xn


def _ffn(x, h, w1, w2, g2, gfin, layer, seq, latent, final):
    t = x.shape[0]
    tm, tf = 1024, 1024
    mrow = _mod_row_fn(tm, seq, latent)
    return pl.pallas_call(
        functools.partial(_ffn_kernel, final=final),
        grid=(t // tm, D_FF // tf),
        in_specs=[
            pl.BlockSpec((tm, D_MODEL), lambda i, k: (i, 0)),
            pl.BlockSpec((tm, D_MODEL), lambda i, k: (i, 0)),
            pl.BlockSpec((None, D_MODEL, tf), lambda i, k: (layer, 0, k)),
            pl.BlockSpec((None, tf, D_MODEL), lambda i, k: (layer, k, 0)),
            pl.BlockSpec((1, 1, D_MODEL), lambda i, k: (mrow(i), 0, 0)),
            pl.BlockSpec((1, D_MODEL), lambda i, k: (0, 0)),
        ],
        out_specs=pl.BlockSpec((tm, D_MODEL), lambda i, k: (i, 0)),
        out_shape=jax.ShapeDtypeStruct((t, D_MODEL), F32),
        scratch_shapes=[pltpu.VMEM((tm, D_MODEL), F32)],
        compiler_params=_cparams(("parallel", "arbitrary")),
        name="ffn_final" if final else "ffn",
    )(x, h, w1, w2, g2, gfin)


def _rope_tables(length):
    rows = length // GRID_W
    row = jnp.repeat(jnp.arange(rows, dtype=F32), GRID_W)
    col = jnp.tile(jnp.arange(GRID_W, dtype=F32), rows)
    half = HEAD_DIM // 2
    inv = 1.0 / (ROPE_BASE ** (jnp.arange(0, half, 2, dtype=F32) / half))
    ang = jnp.concatenate([row[:, None] * inv[None, :], col[:, None] * inv[None, :]], axis=-1)
    cos, sin = jnp.cos(ang), jnp.sin(ang)
    cos2 = jnp.repeat(cos, 2, axis=-1)
    sin2 = jnp.stack([-sin, sin], axis=-1).reshape(length, HEAD_DIM)
    return jnp.tile(cos2, (1, 2)), jnp.tile(sin2, (1, 2))


def _permute_w_in(w_in):
    o = np.cumsum((0, 512, 512, 512, 512, 512, 128, 128, 512, 512, 1024, 1024, 1024))
    seg = lambda i: w_in[:, :, int(o[i]):int(o[i + 1])]
    rq, rk, rv, rg, aq, ak, av, cu, cv, gr, ga, gc = (seg(i) for i in range(12))
    aq = aq.reshape(DEPTH, D_MODEL, H_Q, HEAD_DIM)[:, :, np.array(Q_HEAD_ORDER)].reshape(DEPTH, D_MODEL, ATT_Q_W)
    return jnp.concatenate([gr, ga, gc, rq, rk, rv, rg, aq, cu, cv, ak, av], axis=-1).astype(BF16)


def kernel(x_prompt, x_sample, cache_k, cache_v, state_ret, c, c_ctx, w_mod, b_mod, g_norm1, g_norm2, w_in, g_q, g_k, ret_decay, ret_gn, cm_norm, cm_ws, cm_bs, w_br_ret, w_br_att, w_br_cm, w_out, w_ff1, w_ff2, g_final):
    batch, dec_batch = x_prompt.shape[0], x_sample.shape[0]

    cond = jnp.concatenate(
        [c_ctx[None, :], c, jnp.zeros((N_MOD_ROWS - 1 - dec_batch, D_MODEL), F32)], axis=0)
    mod = _modulation(cond, w_mod, b_mod)
    mod = mod.reshape(DEPTH, N_MOD_ROWS, 6, D_MODEL).transpose(0, 2, 1, 3).reshape(DEPTH, 6, N_MOD_ROWS, 1, D_MODEL)

    w_in_p = _permute_w_in(w_in)
    w_br_ret_b = w_br_ret.astype(BF16)
    w_br_att_b = w_br_att.reshape(DEPTH, H_Q, HEAD_DIM, D_MODEL)[:, np.array(Q_HEAD_ORDER)].reshape(
        DEPTH, ATT_Q_W, D_MODEL).astype(BF16)
    w_br_cm_b = w_br_cm.astype(BF16)
    w_out_b = w_out.astype(BF16)
    w_ff1_b = w_ff1.astype(BF16)
    w_ff2_b = w_ff2.astype(BF16)
    cm_ws_b = cm_ws.astype(BF16)
    cm_bs_t = jnp.swapaxes(cm_bs, 1, 2)
    g_q2 = jnp.tile(g_q, (1, 2)).reshape(DEPTH, 1, LANES)
    g_k2 = jnp.tile(g_k, (1, 2)).reshape(DEPTH, 1, LANES)
    cos, sin = _rope_tables(DEC_SEQ)
    cache_k2 = cache_k.reshape(dec_batch, DEPTH, PAST_LEN, LANES)
    cache_v2 = cache_v.reshape(dec_batch, DEPTH, PAST_LEN, LANES)
    g_fin = g_final.reshape(1, D_MODEL)

    xp = x_prompt.reshape(batch * SEQ, D_MODEL)
    xs = x_sample.reshape(dec_batch * DEC_SEQ, D_MODEL)
    ks_list, vs_list, ss_list = [], [], []
    for l in range(DEPTH):
        sh1, sc1, g1, sh2, sc2, g2 = (mod[l, i] for i in range(6))
        gn1 = g_norm1[l].reshape(1, D_MODEL)
        gn2 = g_norm2[l].reshape(1, D_MODEL)
        final = l == DEPTH - 1
        for latent in (False, True):
            x = xs if latent else xp
            seq = DEC_SEQ if latent else SEQ
            z = _in_proj(x, gn1, sc1, sh1, w_in_p, l, seq, latent)
            if latent:
                ret_o = _retention(z, ret_decay[l], ret_gn[l], seq, True, cos, sin, state_ret, l)
                att_o = _attention(z, g_q2[l], g_k2[l], seq, True, cos, sin, cache_k2, cache_v2, l)
            else:
                ret_o, s_fin = _retention(z, ret_decay[l], ret_gn[l], seq, False)
                att_o, k_new, v_new = _attention(z, g_q2[l], g_k2[l], seq, False)
                ks_list.append(k_new.reshape(batch, SEQ, H_KV, HEAD_DIM))
                vs_list.append(v_new.reshape(batch, SEQ, H_KV, HEAD_DIM))
                ss_list.append(s_fin)
            x, h2 = _merge(x, z, ret_o, att_o, w_br_ret_b, w_br_att_b, w_br_cm_b, w_out_b,
                           cm_norm[l].reshape(1, CM_W), cm_ws_b, cm_bs_t[l], g1, gn2, sc2, sh2, l, seq, latent)
            x = _ffn(x, h2, w_ff1_b, w_ff2_b, g2, g_fin, l, seq, latent, final)
            if latent:
                xs = x
            else:
                xp = x
    y_prompt = xp.reshape(batch, SEQ, D_MODEL)
    y_sample = xs.reshape(dec_batch, DEC_SEQ, D_MODEL)
    return (y_prompt, y_sample, jnp.stack(ks_list, axis=1), jnp.stack(vs_list, axis=1),
            jnp.stack(ss_list, axis=1))
```

```python
import functools

import numpy as np
import jax
import jax.numpy as jnp
from jax import lax
from jax.experimental import pallas as pl
from jax.experimental.pallas import tpu as pltpu

D_MODEL = 1024
DEPTH = 2
SEQ = 256
DEC_SEQ = 2048
PAST_LEN = 256
GRID_W = 64
HEAD_DIM = 64
H_RET = 8
H_Q = 8
H_KV = 2
CHUNK = 128
CM_GROUPS = 4
RET_W = 512
ATT_Q_W = 512
ATT_KV_W = 128
CM_W = 512
D_FF = 4096
ROPE_BASE = 10000.0
EPS = 1e-6
LOG2_E = 1.4426950408889634
IN_W = 6912
N_MOD_ROWS = 16
MOD_SH1, MOD_SC1, MOD_G1, MOD_SH2, MOD_SC2, MOD_G2 = range(6)

LANES = 128
MXU_N = 256
RET_GROUP = 8
ATT_LAG = 2
ATT_BUFS = ATT_LAG + 1
VMEM_LIMIT = 48 * 1024 * 1024

F32 = jnp.float32
BF16 = jnp.bfloat16

COL_GR, COL_GA, COL_GC = 0, 1024, 2048
COL_RQ, COL_RK, COL_RV, COL_RG = 3072, 3584, 4096, 4608
COL_AQ, COL_CU, COL_CV = 5120, 5632, 6144
COL_AK, COL_AV = 6656, 6784
Q_HEAD_ORDER = (0, 4, 1, 5, 2, 6, 3, 7)


def _cparams(sem):
    return pltpu.CompilerParams(dimension_semantics=sem, vmem_limit_bytes=VMEM_LIMIT)


def _dot(a, b):
    return jnp.dot(a, b, preferred_element_type=F32)


def _dot_nt(a, b):
    return lax.dot_general(a, b, (((1,), (1,)), ((), ())), preferred_element_type=F32)


def _dot_tn(a, b):
    return lax.dot_general(a, b, (((0,), (0,)), ((), ())), preferred_element_type=F32)


def _lo_mask(shape):
    return lax.broadcasted_iota(jnp.int32, shape, len(shape) - 1) < HEAD_DIM


def _seg_mean(x, lo):
    s_lo = jnp.sum(jnp.where(lo, x, 0.0), axis=-1, keepdims=True)
    s_hi = jnp.sum(jnp.where(lo, 0.0, x), axis=-1, keepdims=True)
    return jnp.where(lo, s_lo, s_hi) * (1.0 / HEAD_DIM)


def _rope(x, cos, sin_signed):
    nxt = pltpu.roll(x, LANES - 1, 1)
    prv = pltpu.roll(x, 1, 1)
    even = (lax.broadcasted_iota(jnp.int32, x.shape, 1) & 1) == 0
    return x * cos + jnp.where(even, nxt, prv) * sin_signed


def _sigmoid(x):
    return 0.5 * jnp.tanh(0.5 * x) + 0.5


def _silu(x):
    return x * _sigmoid(x)


def _seg_mean_mxu(x, bmat, passes):
    hi = x.astype(BF16)
    out = _dot(hi, bmat)
    if passes == 2:
        out = out + _dot((x - hi.astype(F32)).astype(BF16), bmat)
    return out


def _norm_mod(x, g, sc, sh):
    ms = jnp.mean(x * x, axis=-1, keepdims=True)
    y = x * lax.rsqrt(ms + EPS) * g
    return y * (1.0 + sc) + sh


def _mod_kernel(c_ref, w_ref, b_ref, o_ref):
    s = _silu(c_ref[...])
    o_ref[0] = _dot(s.astype(BF16), w_ref[0].astype(BF16)) + b_ref[0]


def _modulation(cond, w_mod, b_mod):
    tn = 1536
    n = 6 * D_MODEL
    return pl.pallas_call(
        _mod_kernel,
        grid=(DEPTH, n // tn),
        in_specs=[
            pl.BlockSpec((N_MOD_ROWS, D_MODEL), lambda l, j: (0, 0)),
            pl.BlockSpec((1, D_MODEL, tn), lambda l, j: (l, 0, j)),
            pl.BlockSpec((1, 1, tn), lambda l, j: (l, 0, j)),
        ],
        out_specs=pl.BlockSpec((1, N_MOD_ROWS, tn), lambda l, j: (l, 0, j)),
        out_shape=jax.ShapeDtypeStruct((DEPTH, N_MOD_ROWS, n), F32),
        compiler_params=_cparams(("arbitrary", "arbitrary")),
        name="modulation",
    )(cond, w_mod, b_mod.reshape(DEPTH, 1, n))


def _mod_row_fn(tm, seq, latent):
    if latent:
        return lambda i: 1 + (i * tm) // seq
    return lambda i: 0


def _mod_spec(layer, which, mrow):
    return pl.BlockSpec((None, None, 1, 1, D_MODEL), lambda i, *_: (layer, which, mrow(i), 0, 0))


def _layer_spec(layer, shape):
    return pl.BlockSpec((None,) + shape, lambda *_: (layer,) + (0,) * len(shape))


def _proj_kernel(x_ref, g_ref, sc_ref, sh_ref, w_ref, o_ref, h_ref):
    @pl.when(pl.program_id(1) == 0)
    def _():
        h_ref[...] = _norm_mod(x_ref[...], g_ref[...], sc_ref[0], sh_ref[0]).astype(BF16)

    o_ref[...] = _dot(h_ref[...], w_ref[...]).astype(BF16)


def _in_proj(x, g, mod, w, layer, seq, latent):
    t = x.shape[0]
    tm, tn = 1024, 2304
    mrow = _mod_row_fn(tm, seq, latent)
    return pl.pallas_call(
        _proj_kernel,
        grid=(t // tm, IN_W // tn),
        in_specs=[
            pl.BlockSpec((tm, D_MODEL), lambda i, j: (i, 0)),
            _layer_spec(layer, (1, D_MODEL)),
            _mod_spec(layer, MOD_SC1, mrow), _mod_spec(layer, MOD_SH1, mrow),
            pl.BlockSpec((None, D_MODEL, tn), lambda i, j: (layer, 0, j)),
        ],
        out_specs=pl.BlockSpec((tm, tn), lambda i, j: (i, j)),
        out_shape=jax.ShapeDtypeStruct((t, IN_W), BF16),
        scratch_shapes=[pltpu.VMEM((tm, D_MODEL), BF16)],
        compiler_params=_cparams(("parallel", "arbitrary")),
        name="in_proj",
    )(x, g, mod, mod, w)


def _ret_kernel(*refs, nc, latent, layer):
    if latent:
        (dec_ref, q_ref, k_ref, v_ref, rg_ref, gn_ref, cos_ref, sin_ref, s0_ref,
         o_ref, cst_ref, uu_ref, kr_ref) = refs
    else:
        (dec_ref, q_ref, k_ref, v_ref, rg_ref, gn_ref,
         o_ref, sfin_ref, cst_ref, uu_ref) = refs
    hp = pl.program_id(0)
    sq = (CHUNK, LANES)
    lane = lax.broadcasted_iota(jnp.int32, sq, 1)
    row = lax.broadcasted_iota(jnp.int32, sq, 0)
    lo = lane < HEAD_DIM
    blockdiag = (row < HEAD_DIM) == lo
    bmat = jnp.where(blockdiag, 1.0 / HEAD_DIM, 0.0).astype(BF16)
    @pl.when(pl.program_id(1) == 0)
    def _():
        rowf = row.astype(F32)
        rel = (row - lane).astype(F32)

        def log_decay(direction, head):
            d = jnp.full(sq, dec_ref[layer, direction, head], F32)
            return -(jnp.maximum(d, 0.0) + jnp.log1p(jnp.exp(-jnp.abs(d))))

        ldf0, ldf1 = log_decay(0, 2 * hp), log_decay(0, 2 * hp + 1)
        ldb0, ldb1 = log_decay(1, 2 * hp), log_decay(1, 2 * hp + 1)
        ldf = jnp.where(lo, ldf0, ldf1)
        ldb = jnp.where(lo, ldb0, ldb1)

        def decay_mask(lf, lb):
            fwd = jnp.exp(lf * jnp.maximum(rel, 0.0))
            bwd = jnp.exp(lb * jnp.maximum(-rel, 0.0))
            return jnp.where(rel > 0, fwd, jnp.where(rel < 0, bwd, 2.0))

        cst_ref[0] = decay_mask(ldf0, ldb0)
        cst_ref[1] = decay_mask(ldf1, ldb1)
        cst_ref[2] = jnp.exp(ldf * (CHUNK - 1.0 - rowf))
        cst_ref[3] = jnp.exp(ldb * rowf)
        cst_ref[4] = jnp.exp(ldf * (rowf + 1.0))
        cst_ref[5] = jnp.exp(ldb * (CHUNK - rowf))
        cst_ref[6] = jnp.exp(ldf * float(CHUNK))
        cst_ref[7] = jnp.exp(ldb * float(CHUNK))

    def rows_of(c):
        return pl.ds(pl.multiple_of(c * CHUNK, CHUNK), CHUNK)

    def load_qk(ref, c, scale):
        x = ref[rows_of(c), :].astype(F32)
        if scale != 1.0:
            x = x * scale
        if latent:
            x = _rope(x, cos_ref[rows_of(c), :], sin_ref[rows_of(c), :])
        return x

    group = min(nc, RET_GROUP)

    def phase_a(i, carry):
        cs = [i * group + g for g in range(group)]
        ks = [load_qk(k_ref, c, HEAD_DIM ** -0.5) for c in cs]
        if latent:
            for c, k in zip(cs, ks):
                kr_ref[rows_of(c), :] = k
        kzs = [jnp.concatenate([(k * cst_ref[2]).astype(BF16), (k * cst_ref[3]).astype(BF16)], axis=1)
               for k in ks]
        for c, kz in zip(cs, kzs):
            uu_ref[c] = _dot_tn(kz, v_ref[rows_of(c), :])
        return carry

    lax.fori_loop(0, nc // group, phase_a, 0)

    if latent:
        zeros = jnp.zeros((HEAD_DIM, HEAD_DIM), F32)

        def embed(direction):
            top = jnp.concatenate([s0_ref[0, 0, direction, 0], zeros], axis=1)
            bot = jnp.concatenate([zeros, s0_ref[0, 0, direction, 1]], axis=1)
            return jnp.concatenate([top, bot], axis=0)

        sf0, sb0 = embed(0), embed(1)
    else:
        sf0 = sb0 = jnp.zeros(sq, F32)

    def scan_f(c, s):
        u = uu_ref[c, 0:CHUNK, :]
        uu_ref[c, 0:CHUNK, :] = s
        return cst_ref[6] * s + jnp.where(blockdiag, u, 0.0)

    def scan_b(i, s):
        c = nc - 1 - i
        u = uu_ref[c, CHUNK:2 * CHUNK, :]
        uu_ref[c, CHUNK:2 * CHUNK, :] = s
        return cst_ref[7] * s + jnp.where(blockdiag, u, 0.0)

    sf = lax.fori_loop(0, nc, scan_f, sf0, unroll=True)
    sb = lax.fori_loop(0, nc, scan_b, sb0, unroll=True)
    if not latent:
        for d, s in ((0, sf), (1, sb)):
            sfin_ref[0, d, 0] = s[0:HEAD_DIM, 0:HEAD_DIM]
            sfin_ref[0, d, 1] = s[HEAD_DIM:, HEAD_DIM:]

    def phase_c(i, carry):
        cs = [i * group + g for g in range(group)]
        n = range(group)
        q = [load_qk(q_ref, c, 1.0) for c in cs]
        k = [kr_ref[rows_of(c), :] if latent else load_qk(k_ref, c, HEAD_DIM ** -0.5) for c in cs]
        v = [v_ref[rows_of(c), :] for c in cs]
        qb = [x.astype(BF16) for x in q]
        s0 = [_dot_nt(qb[g], jnp.where(lo, k[g], 0.0).astype(BF16)) for g in n]
        s1 = [_dot_nt(qb[g], jnp.where(lo, 0.0, k[g]).astype(BF16)) for g in n]
        qx = [jnp.concatenate([(x * cst_ref[4]).astype(BF16), (x * cst_ref[5]).astype(BF16)], axis=1) for x in q]
        cross = [_dot(qx[g], uu_ref[cs[g]].astype(BF16)) for g in n]
        i0 = [_dot((s0[g] * cst_ref[0]).astype(BF16), v[g]) for g in n]
        i1 = [_dot((s1[g] * cst_ref[1]).astype(BF16), v[g]) for g in n]
        o = [jnp.where(lo, i0[g], i1[g]) + cross[g] for g in n]
        mu = [_seg_mean_mxu(x, bmat, 2) for x in o]
        d = [o[g] - mu[g] for g in n]
        var = [_seg_mean_mxu(x * x, bmat, 1) for x in d]
        for g in n:
            y = d[g] * lax.rsqrt(var[g] + EPS) * gn_ref[...]
            o_ref[rows_of(cs[g]), :] = (y * _silu(rg_ref[rows_of(cs[g]), :].astype(F32))).astype(BF16)
        return carry

    lax.fori_loop(0, nc // group, phase_c, 0)


def _retention(z, dec, gn, layer, seq, latent, cos=None, sin=None, s0=None):
    t = z.shape[0]
    b = t // seq
    nc = seq // CHUNK
    npair = H_RET // 2

    def zcol(base):
        return pl.BlockSpec((seq, LANES), lambda hp, bi: (bi, base // LANES + hp))

    in_specs = [
        pl.BlockSpec(memory_space=pltpu.SMEM),
        zcol(COL_RQ), zcol(COL_RK), zcol(COL_RV), zcol(COL_RG),
        pl.BlockSpec((None, 1, LANES), lambda hp, bi: (layer, 0, hp)),
    ]
    args = [dec, z, z, z, z, gn]
    out_shape = [jax.ShapeDtypeStruct((t, RET_W), BF16)]
    out_specs = [pl.BlockSpec((seq, LANES), lambda hp, bi: (bi, hp))]
    if latent:
        in_specs += [
            pl.BlockSpec((seq, LANES), lambda hp, bi: (0, 0)),
            pl.BlockSpec((seq, LANES), lambda hp, bi: (0, 0)),
            pl.BlockSpec((1, 1, 2, 2, HEAD_DIM, HEAD_DIM), lambda hp, bi: (bi, layer, 0, hp, 0, 0)),
        ]
        args += [cos, sin, s0]
    else:
        out_shape.append(jax.ShapeDtypeStruct((b, 2, H_RET, HEAD_DIM, HEAD_DIM), F32))
        out_specs.append(pl.BlockSpec((1, 2, 2, HEAD_DIM, HEAD_DIM), lambda hp, bi: (bi, 0, hp, 0, 0)))
    res = pl.pallas_call(
        functools.partial(_ret_kernel, nc=nc, latent=latent, layer=layer),
        grid=(npair, b),
        in_specs=in_specs,
        out_specs=out_specs,
        out_shape=out_shape,
        scratch_shapes=[pltpu.VMEM((8, CHUNK, LANES), F32), pltpu.VMEM((nc, 2 * CHUNK, LANES), F32)]
        + ([pltpu.VMEM((seq, LANES), F32)] if latent else []),
        compiler_params=_cparams(("parallel", "arbitrary")),
        name="retention_latent" if latent else "retention_ctx",
    )(*args)
    return res if not latent else res[0]


def _att_kernel(*refs, tq, seq, latent):
    if latent:
        (q_ref, ak_ref, av_ref, ck_ref, cv_ref, gq_ref, gk_ref, cosq_ref, sinq_ref, cosk_ref, sink_ref,
         o_ref, k0_ref, k1_ref, vv_ref, *sp_refs) = refs
        off = PAST_LEN
    else:
        (q_ref, ak_ref, av_ref, gq_ref, gk_ref,
         o_ref, knew_ref, vnew_ref, k0_ref, k1_ref, vv_ref, *sp_refs) = refs
        off = 0

    @pl.when(pl.program_id(1) == 0)
    def _():
        lo = _lo_mask((seq, LANES))
        ak = ak_ref[...].astype(F32)
        kn = ak * lax.rsqrt(_seg_mean(ak * ak, lo) + EPS) * gk_ref[...]
        av = av_ref[...]
        if latent:
            kn = _rope(kn, cosk_ref[...], sink_ref[...])
            lo_c = _lo_mask((PAST_LEN, LANES))
            ck = ck_ref[0, 0]
            k0_ref[:, 0:off] = jnp.where(lo_c, ck, 0.0).T.astype(BF16)
            k1_ref[:, 0:off] = jnp.where(lo_c, 0.0, ck).T.astype(BF16)
            vv_ref[0:off, :] = cv_ref[0, 0].astype(BF16)
        else:
            knew_ref[...] = kn
            vnew_ref[...] = av.astype(F32)
        k0_ref[:, off:off + seq] = jnp.where(lo, kn, 0.0).T.astype(BF16)
        k1_ref[:, off:off + seq] = jnp.where(lo, 0.0, kn).T.astype(BF16)
        vv_ref[off:off + seq, :] = av

    sk = off + seq
    nkt = sk // MXU_N
    lo = _lo_mask((CHUNK, LANES))
    krefs = (k0_ref, k1_ref)
    units = [(sub, r, g) for sub in range(tq // CHUNK) for r in range(ATT_Q_W // LANES) for g in range(H_KV)]
    qb, m_lane, m_row, l_lane, o_acc, done = {}, {}, {}, {}, {}, {}

    def prep(sub, r):
        rows = slice(sub * CHUNK, (sub + 1) * CHUNK)
        x = q_ref[rows, r * LANES:(r + 1) * LANES].astype(F32)
        xn = x * lax.rsqrt(_seg_mean(x * x, lo) + EPS) * gq_ref[...]
        if latent:
            xn = _rope(xn, cosq_ref[rows, :], sinq_ref[rows, :])
        qb[(sub, r)] = (xn * (HEAD_DIM ** -0.5 * LOG2_E)).astype(BF16)

    def score_tile(i, kt):
        sub, r, g = units[i]
        cols = slice(kt * MXU_N, (kt + 1) * MXU_N)
        s_t = _dot(qb[(sub, r)], krefs[g][:, cols])
        sp_refs[i % ATT_BUFS][:, cols] = s_t
        t = jnp.maximum(s_t[:, :LANES], s_t[:, LANES:])
        m_lane[i] = t if kt == 0 else jnp.maximum(m_lane[i], t)

    def soft_tile(i, kt):
        cols = slice(kt * MXU_N, (kt + 1) * MXU_N)
        sbuf = sp_refs[i % ATT_BUFS]
        p0 = jnp.exp2(sbuf[:, kt * MXU_N:kt * MXU_N + LANES] - m_row[i])
        p1 = jnp.exp2(sbuf[:, kt * MXU_N + LANES:(kt + 1) * MXU_N] - m_row[i])
        pv = _dot(jnp.concatenate([p0, p1], axis=1).astype(BF16), vv_ref[cols, :])
        if kt == 0:
            l_lane[i], o_acc[i] = p0 + p1, pv
        else:
            l_lane[i], o_acc[i] = l_lane[i] + (p0 + p1), o_acc[i] + pv

    def finish(i):
        sub, r, g = units[i]
        done[(sub, r, g)] = o_acc.pop(i) / jnp.sum(l_lane.pop(i), axis=-1, keepdims=True)
        if g == H_KV - 1:
            rows = slice(sub * CHUNK, (sub + 1) * CHUNK)
            o = jnp.where(lo, done.pop((sub, r, 0)), done.pop((sub, r, 1)))
            o_ref[rows, r * LANES:(r + 1) * LANES] = o.astype(BF16)

    prep(units[0][0], units[0][1])
    for s in range(len(units) + ATT_LAG):
        nxt = s + H_KV
        if s % H_KV == 0 and nxt < len(units):
            prep(units[nxt][0], units[nxt][1])
        for kt in range(nkt):
            if s < len(units):
                score_tile(s, kt)
            if s >= ATT_LAG:
                soft_tile(s - ATT_LAG, kt)
        if s < len(units):
            m_row[s] = jnp.broadcast_to(jnp.max(m_lane.pop(s), axis=-1, keepdims=True), (CHUNK, LANES))
        if s >= ATT_LAG:
            finish(s - ATT_LAG)


def _attention(z, gq, gk, layer, seq, latent, cos=None, sin=None, cache_k=None, cache_v=None):
    t = z.shape[0]
    b = t // seq
    tq = 256
    nq = seq // tq
    sk = seq + (PAST_LEN if latent else 0)
    kv_spec = lambda base: pl.BlockSpec((seq, LANES), lambda bi, qi: (bi, base // LANES))
    vec_spec = _layer_spec(layer, (1, LANES))
    in_specs = [
        pl.BlockSpec((tq, ATT_Q_W), lambda bi, qi: (bi * nq + qi, COL_AQ // ATT_Q_W)),
        kv_spec(COL_AK), kv_spec(COL_AV),
    ]
    args = [z, z, z]
    if latent:
        cache_spec = pl.BlockSpec((1, 1, PAST_LEN, LANES), lambda bi, qi: (bi, layer, 0, 0))
        in_specs += [cache_spec, cache_spec]
        args += [cache_k, cache_v]
    in_specs += [vec_spec, vec_spec]
    args += [gq, gk]
    out_shape = [jax.ShapeDtypeStruct((t, ATT_Q_W), BF16)]
    out_specs = [pl.BlockSpec((tq, ATT_Q_W), lambda bi, qi: (bi * nq + qi, 0))]
    if latent:
        in_specs += [
            pl.BlockSpec((tq, LANES), lambda bi, qi: (qi, 0)),
            pl.BlockSpec((tq, LANES), lambda bi, qi: (qi, 0)),
            pl.BlockSpec((seq, LANES), lambda bi, qi: (0, 0)),
            pl.BlockSpec((seq, LANES), lambda bi, qi: (0, 0)),
        ]
        args += [cos, sin, cos, sin]
    else:
        new_spec = pl.BlockSpec((seq, LANES), lambda bi, qi: (bi, 0))
        out_shape += [jax.ShapeDtypeStruct((t, LANES), F32)] * 2
        out_specs += [new_spec, new_spec]
    res = pl.pallas_call(
        functools.partial(_att_kernel, tq=tq, seq=seq, latent=latent),
        grid=(b, nq),
        in_specs=in_specs,
        out_specs=out_specs,
        out_shape=out_shape,
        scratch_shapes=([pltpu.VMEM((LANES, sk), BF16)] * 2 + [pltpu.VMEM((sk, LANES), BF16)]
                        + [pltpu.VMEM((CHUNK, sk), F32)] * ATT_BUFS),
        compiler_params=_cparams(("parallel", "arbitrary")),
        name="attention_latent" if latent else "attention_ctx",
    )(*args)
    return res if not latent else res[0]


def _merge_kernel(x_ref, gr_ref, ga_ref, gc_ref, cu_ref, cv_ref, ro_ref, ao_ref,
                  wr_ref, wa_ref, wc_ref, wo_ref, cmn_ref, ws_ref, bs_ref, g1_ref,
                  gn2_ref, sc2_ref, sh2_ref, o_ref, h_ref, *, tm):
    u = jax.nn.gelu(cu_ref[...].astype(F32), approximate=True)
    v = jax.nn.gelu(cv_ref[...].astype(F32), approximate=True)
    vn = v * lax.rsqrt(jnp.mean(v * v, axis=-1, keepdims=True) + EPS) * cmn_ref[...]
    vb = vn.astype(BF16)
    rows = []
    for n in range(tm // CHUNK):
        cols = []
        for g in range(CM_GROUPS):
            vg = vb[n * CHUNK:(n + 1) * CHUNK, g * LANES:(g + 1) * LANES]
            cols.append(_dot(ws_ref[g], vg) + bs_ref[:, g:g + 1])
        rows.append(jnp.concatenate(cols, axis=1))
    cm = (u * jnp.concatenate(rows, axis=0)).astype(BF16)
    merged = (_sigmoid(gr_ref[...]).astype(F32) * _dot(ro_ref[...], wr_ref[...])
              + _sigmoid(ga_ref[...]).astype(F32) * _dot(ao_ref[...], wa_ref[...])
              + _sigmoid(gc_ref[...]).astype(F32) * _dot(cm, wc_ref[...]))
    y = _dot(merged.astype(BF16), wo_ref[...])
    xn = x_ref[...] + g1_ref[0] * y
    o_ref[...] = xn
    h_ref[...] = _norm_mod(xn, gn2_ref[...], sc2_ref[0], sh2_ref[0]).astype(BF16)


def _merge(x, z, ret_o, att_o, wr, wa, wc, wo, cmn, ws, bs_t, mod, gn2, layer, seq, latent):
    t = x.shape[0]
    tm = 512
    mrow = _mod_row_fn(tm, seq, latent)
    row_spec = lambda w, cb: pl.BlockSpec((tm, w), lambda i: (i, cb))
    of_layer = lambda shape: _layer_spec(layer, shape)
    return pl.pallas_call(
        functools.partial(_merge_kernel, tm=tm),
        grid=(t // tm,),
        in_specs=[
            row_spec(D_MODEL, 0),
            row_spec(D_MODEL, COL_GR // D_MODEL), row_spec(D_MODEL, COL_GA // D_MODEL),
            row_spec(D_MODEL, COL_GC // D_MODEL),
            row_spec(CM_W, COL_CU // CM_W), row_spec(CM_W, COL_CV // CM_W),
            row_spec(RET_W, 0), row_spec(ATT_Q_W, 0),
            of_layer((RET_W, D_MODEL)), of_layer((ATT_Q_W, D_MODEL)), of_layer((CM_W, D_MODEL)),
            of_layer((D_MODEL, D_MODEL)),
            of_layer((1, CM_W)), of_layer((CM_GROUPS, CHUNK, CHUNK)), of_layer((CHUNK, CM_GROUPS)),
            _mod_spec(layer, MOD_G1, mrow), of_layer((1, D_MODEL)),
            _mod_spec(layer, MOD_SC2, mrow), _mod_spec(layer, MOD_SH2, mrow),
        ],
        out_specs=[row_spec(D_MODEL, 0), row_spec(D_MODEL, 0)],
        out_shape=[jax.ShapeDtypeStruct((t, D_MODEL), F32), jax.ShapeDtypeStruct((t, D_MODEL), BF16)],
        compiler_params=_cparams(("parallel",)),
        name="merge",
    )(x, z, z, z, z, z, ret_o, att_o, wr, wa, wc, wo, cmn, ws, bs_t, mod, gn2, mod, mod)


def _ffn_kernel(x_ref, h_ref, w1_ref, w2_ref, g2_ref, gf_ref, o_ref, acc_ref, *, final):
    k = pl.program_id(1)
    a = jnp.square(jnp.maximum(_dot(h_ref[...], w1_ref[...]), 0.0)).astype(BF16)

    @pl.when(k == 0)
    def _():
        acc_ref[...] = _dot(a, w2_ref[...])

    @pl.when(k > 0)
    def _():
        acc_ref[...] += _dot(a, w2_ref[...])

    @pl.when(k == pl.num_programs(1) - 1)
    def _():
        xn = x_ref[...] + g2_ref[0] * acc_ref[...]
        if final:
            xn = xn * lax.rsqrt(jnp.mean(xn * xn, axis=-1, keepdims=True) + EPS) * gf_ref[...]
        o_ref[...] = xn


def _ffn(x, h, w1, w2, mod, gfin, layer, seq, latent, final):
    t = x.shape[0]
    tm, tf = 1024, 1024
    mrow = _mod_row_fn(tm, seq, latent)
    return pl.pallas_call(
        functools.partial(_ffn_kernel, final=final),
        grid=(t // tm, D_FF // tf),
        in_specs=[
            pl.BlockSpec((tm, D_MODEL), lambda i, k: (i, 0)),
            pl.BlockSpec((tm, D_MODEL), lambda i, k: (i, 0)),
            pl.BlockSpec((None, D_MODEL, tf), lambda i, k: (layer, 0, k)),
            pl.BlockSpec((None, tf, D_MODEL), lambda i, k: (layer, k, 0)),
            _mod_spec(layer, MOD_G2, mrow),
            pl.BlockSpec((1, D_MODEL), lambda i, k: (0, 0)),
        ],
        out_specs=pl.BlockSpec((tm, D_MODEL), lambda i, k: (i, 0)),
        out_shape=jax.ShapeDtypeStruct((t, D_MODEL), F32),
        scratch_shapes=[pltpu.VMEM((tm, D_MODEL), F32)],
        compiler_params=_cparams(("parallel", "arbitrary")),
        name="ffn_final" if final else "ffn",
    )(x, h, w1, w2, mod, gfin)


def _rope_tables(length):
    rows = length // GRID_W
    row = jnp.repeat(jnp.arange(rows, dtype=F32), GRID_W)
    col = jnp.tile(jnp.arange(GRID_W, dtype=F32), rows)
    half = HEAD_DIM // 2
    inv = 1.0 / (ROPE_BASE ** (jnp.arange(0, half, 2, dtype=F32) / half))
    ang = jnp.concatenate([row[:, None] * inv[None, :], col[:, None] * inv[None, :]], axis=-1)
    cos, sin = jnp.cos(ang), jnp.sin(ang)
    cos2 = jnp.repeat(cos, 2, axis=-1)
    sin2 = jnp.stack([-sin, sin], axis=-1).reshape(length, HEAD_DIM)
    return jnp.tile(cos2, (1, 2)), jnp.tile(sin2, (1, 2))


def _permute_w_in(w_in):
    o = np.cumsum((0, 512, 512, 512, 512, 512, 128, 128, 512, 512, 1024, 1024, 1024))
    seg = lambda i: w_in[:, :, int(o[i]):int(o[i + 1])]
    rq, rk, rv, rg, aq, ak, av, cu, cv, gr, ga, gc = (seg(i) for i in range(12))
    aq = aq.reshape(DEPTH, D_MODEL, H_Q, HEAD_DIM)[:, :, np.array(Q_HEAD_ORDER)].reshape(DEPTH, D_MODEL, ATT_Q_W)
    return jnp.concatenate([gr, ga, gc, rq, rk, rv, rg, aq, cu, cv, ak, av], axis=-1).astype(BF16)


def kernel(x_prompt, x_sample, cache_k, cache_v, state_ret, c, c_ctx, w_mod, b_mod, g_norm1, g_norm2, w_in, g_q, g_k, ret_decay, ret_gn, cm_norm, cm_ws, cm_bs, w_br_ret, w_br_att, w_br_cm, w_out, w_ff1, w_ff2, g_final):
    batch, dec_batch = x_prompt.shape[0], x_sample.shape[0]

    cond = jnp.concatenate(
        [c_ctx[None, :], c, jnp.zeros((N_MOD_ROWS - 1 - dec_batch, D_MODEL), F32)], axis=0)
    mod = _modulation(cond, w_mod, b_mod)
    mod = mod.reshape(DEPTH, N_MOD_ROWS, 6, D_MODEL).transpose(0, 2, 1, 3).reshape(DEPTH, 6, N_MOD_ROWS, 1, D_MODEL)

    w_in_p = _permute_w_in(w_in)
    w_br_ret_b = w_br_ret.astype(BF16)
    w_br_att_b = w_br_att.reshape(DEPTH, H_Q, HEAD_DIM, D_MODEL)[:, np.array(Q_HEAD_ORDER)].reshape(
        DEPTH, ATT_Q_W, D_MODEL).astype(BF16)
    w_br_cm_b = w_br_cm.astype(BF16)
    w_out_b = w_out.astype(BF16)
    w_ff1_b = w_ff1.astype(BF16)
    w_ff2_b = w_ff2.astype(BF16)
    cm_ws_b = cm_ws.astype(BF16)
    cm_bs_t = jnp.swapaxes(cm_bs, 1, 2)
    g_q2 = jnp.tile(g_q, (1, 2)).reshape(DEPTH, 1, LANES)
    g_k2 = jnp.tile(g_k, (1, 2)).reshape(DEPTH, 1, LANES)
    cos, sin = _rope_tables(DEC_SEQ)
    cache_k2 = cache_k.reshape(dec_batch, DEPTH, PAST_LEN, LANES)
    cache_v2 = cache_v.reshape(dec_batch, DEPTH, PAST_LEN, LANES)
    g_fin = g_final.reshape(1, D_MODEL)
    gn1 = g_norm1.reshape(DEPTH, 1, D_MODEL)
    gn2 = g_norm2.reshape(DEPTH, 1, D_MODEL)
    ret_gn2 = ret_gn.reshape(DEPTH, 1, RET_W)
    cm_norm2 = cm_norm.reshape(DEPTH, 1, CM_W)

    xp = x_prompt.reshape(batch * SEQ, D_MODEL)
    xs = x_sample.reshape(dec_batch * DEC_SEQ, D_MODEL)
    ks_list, vs_list, ss_list = [], [], []
    for l in range(DEPTH):
        final = l == DEPTH - 1
        for latent in (False, True):
            x = xs if latent else xp
            seq = DEC_SEQ if latent else SEQ
            z = _in_proj(x, gn1, mod, w_in_p, l, seq, latent)
            if latent:
                ret_o = _retention(z, ret_decay, ret_gn2, l, seq, True, cos, sin, state_ret)
                att_o = _attention(z, g_q2, g_k2, l, seq, True, cos, sin, cache_k2, cache_v2)
            else:
                ret_o, s_fin = _retention(z, ret_decay, ret_gn2, l, seq, False)
                att_o, k_new, v_new = _attention(z, g_q2, g_k2, l, seq, False)
                ks_list.append(k_new.reshape(batch, SEQ, H_KV, HEAD_DIM))
                vs_list.append(v_new.reshape(batch, SEQ, H_KV, HEAD_DIM))
                ss_list.append(s_fin)
            x, h2 = _merge(x, z, ret_o, att_o, w_br_ret_b, w_br_att_b, w_br_cm_b, w_out_b,
                           cm_norm2, cm_ws_b, cm_bs_t, mod, gn2, l, seq, latent)
            x = _ffn(x, h2, w_ff1_b, w_ff2_b, mod, g_fin, l, seq, latent, final)
            if latent:
                xs = x
            else:
                xp = x
    y_prompt = xp.reshape(batch, SEQ, D_MODEL)
    y_sample = xs.reshape(dec_batch, DEC_SEQ, D_MODEL)
    return (y_prompt, y_sample, jnp.stack(ks_list, axis=1), jnp.stack(vs_list, axis=1),
            jnp.stack(ss_list, axis=1))
```

```python
import functools

import numpy as np
import jax
import jax.numpy as jnp
from jax import lax
from jax.experimental import pallas as pl
from jax.experimental.pallas import tpu as pltpu

D_MODEL = 1024
DEPTH = 2
SEQ = 256
DEC_SEQ = 2048
PAST_LEN = 256
GRID_W = 64
HEAD_DIM = 64
H_RET = 8
H_Q = 8
H_KV = 2
CHUNK = 128
CM_GROUPS = 4
RET_W = 512
ATT_Q_W = 512
ATT_KV_W = 128
CM_W = 512
D_FF = 4096
ROPE_BASE = 10000.0
EPS = 1e-6
LOG2_E = 1.4426950408889634
IN_W = 6912
N_MOD_ROWS = 16
MOD_SH1, MOD_SC1, MOD_G1, MOD_SH2, MOD_SC2, MOD_G2 = range(6)

LANES = 128
MXU_N = 256
RET_GROUP = 8
RET_CTX_BATCH = 4
ATT_LAG = 2
ATT_BUFS = ATT_LAG + 1
VMEM_LIMIT = 48 * 1024 * 1024

F32 = jnp.float32
BF16 = jnp.bfloat16

COL_GR, COL_GA, COL_GC = 0, 1024, 2048
COL_RQ, COL_RK, COL_RV, COL_RG = 3072, 3584, 4096, 4608
COL_AQ, COL_CU, COL_CV = 5120, 5632, 6144
COL_AK, COL_AV = 6656, 6784
Q_HEAD_ORDER = (0, 4, 1, 5, 2, 6, 3, 7)


def _cparams(sem):
    return pltpu.CompilerParams(dimension_semantics=sem, vmem_limit_bytes=VMEM_LIMIT)


def _dot(a, b):
    return jnp.dot(a, b, preferred_element_type=F32)


def _dot_nt(a, b):
    return lax.dot_general(a, b, (((1,), (1,)), ((), ())), preferred_element_type=F32)


def _dot_tn(a, b):
    return lax.dot_general(a, b, (((0,), (0,)), ((), ())), preferred_element_type=F32)


def _lo_mask(shape):
    return lax.broadcasted_iota(jnp.int32, shape, len(shape) - 1) < HEAD_DIM


def _seg_mean(x, lo):
    s_lo = jnp.sum(jnp.where(lo, x, 0.0), axis=-1, keepdims=True)
    s_hi = jnp.sum(jnp.where(lo, 0.0, x), axis=-1, keepdims=True)
    return jnp.where(lo, s_lo, s_hi) * (1.0 / HEAD_DIM)


def _rope(x, cos, sin_signed):
    nxt = pltpu.roll(x, LANES - 1, 1)
    prv = pltpu.roll(x, 1, 1)
    even = (lax.broadcasted_iota(jnp.int32, x.shape, 1) & 1) == 0
    return x * cos + jnp.where(even, nxt, prv) * sin_signed


def _sigmoid(x):
    return 0.5 * jnp.tanh(0.5 * x) + 0.5


def _silu(x):
    return x * _sigmoid(x)


def _seg_mean_mxu(x, bmat, passes):
    hi = x.astype(BF16)
    out = _dot(hi, bmat)
    if passes == 2:
        out = out + _dot((x - hi.astype(F32)).astype(BF16), bmat)
    return out


def _norm_mod(x, g, sc, sh):
    ms = jnp.mean(x * x, axis=-1, keepdims=True)
    y = x * lax.rsqrt(ms + EPS) * g
    return y * (1.0 + sc) + sh


def _mod_kernel(c_ref, w_ref, b_ref, o_ref):
    s = _silu(c_ref[...])
    o_ref[0] = _dot(s.astype(BF16), w_ref[0].astype(BF16)) + b_ref[0]


def _modulation(cond, w_mod, b_mod):
    tn = 1536
    n = 6 * D_MODEL
    return pl.pallas_call(
        _mod_kernel,
        grid=(DEPTH, n // tn),
        in_specs=[
            pl.BlockSpec((N_MOD_ROWS, D_MODEL), lambda l, j: (0, 0)),
            pl.BlockSpec((1, D_MODEL, tn), lambda l, j: (l, 0, j)),
            pl.BlockSpec((1, 1, tn), lambda l, j: (l, 0, j)),
        ],
        out_specs=pl.BlockSpec((1, N_MOD_ROWS, tn), lambda l, j: (l, 0, j)),
        out_shape=jax.ShapeDtypeStruct((DEPTH, N_MOD_ROWS, n), F32),
        compiler_params=_cparams(("arbitrary", "arbitrary")),
        name="modulation",
    )(cond, w_mod, b_mod.reshape(DEPTH, 1, n))


def _mod_row_fn(tm, seq, latent):
    if latent:
        return lambda i: 1 + (i * tm) // seq
    return lambda i: 0


def _mod_spec(layer, which, mrow):
    return pl.BlockSpec((None, None, 1, 1, D_MODEL), lambda i, *_: (layer, which, mrow(i), 0, 0))


def _layer_spec(layer, shape):
    return pl.BlockSpec((None,) + shape, lambda *_: (layer,) + (0,) * len(shape))


def _proj_epilogues(tn):
    kinds = ((COL_GR, COL_RQ, _sigmoid), (COL_RQ, COL_RG, None), (COL_RG, COL_AQ, _silu),
             (COL_AQ, COL_CU, None), (COL_CU, COL_AK, functools.partial(jax.nn.gelu, approximate=True)),
             (COL_AK, IN_W, None))
    tiles = [(c, c + MXU_N, fn) for a, e, fn in kinds for c in range(a, e, MXU_N)]
    return [[(a - b * tn, e - b * tn, fn) for a, e, fn in tiles if b * tn <= a < (b + 1) * tn]
            for b in range(IN_W // tn)]


def _proj_kernel(x_ref, g_ref, sc_ref, sh_ref, w_ref, o_ref, h_ref, *, tn):
    j = pl.program_id(1)

    @pl.when(j == 0)
    def _():
        h_ref[...] = _norm_mod(x_ref[...], g_ref[...], sc_ref[0], sh_ref[0]).astype(BF16)

    for b, segs in enumerate(_proj_epilogues(tn)):
        @pl.when(j == b)
        def _(segs=segs):
            for a, e, fn in segs:
                y = _dot(h_ref[...], w_ref[:, a:e])
                o_ref[:, a:e] = (y if fn is None else fn(y)).astype(BF16)


def _in_proj(x, g, mod, w, layer, seq, latent):
    t = x.shape[0]
    tm, tn = 1024, 2304
    mrow = _mod_row_fn(tm, seq, latent)
    return pl.pallas_call(
        functools.partial(_proj_kernel, tn=tn),
        grid=(t // tm, IN_W // tn),
        in_specs=[
            pl.BlockSpec((tm, D_MODEL), lambda i, j: (i, 0)),
            _layer_spec(layer, (1, D_MODEL)),
            _mod_spec(layer, MOD_SC1, mrow), _mod_spec(layer, MOD_SH1, mrow),
            pl.BlockSpec((None, D_MODEL, tn), lambda i, j: (layer, 0, j)),
        ],
        out_specs=pl.BlockSpec((tm, tn), lambda i, j: (i, j)),
        out_shape=jax.ShapeDtypeStruct((t, IN_W), BF16),
        scratch_shapes=[pltpu.VMEM((tm, D_MODEL), BF16)],
        compiler_params=_cparams(("parallel", "arbitrary")),
        name="in_proj",
    )(x, g, mod, mod, w)


def _ret_kernel(*refs, nb, nc, latent, layer):
    if latent:
        (dec_ref, q_ref, k_ref, v_ref, rg_ref, gn_ref, cos_ref, sin_ref, s0_ref,
         o_ref, cst_ref, uu_ref, kr_ref) = refs
    else:
        (dec_ref, q_ref, k_ref, v_ref, rg_ref, gn_ref,
         o_ref, sfin_ref, cst_ref, uu_ref) = refs
    hp = pl.program_id(0)
    sq = (CHUNK, LANES)
    lane = lax.broadcasted_iota(jnp.int32, sq, 1)
    row = lax.broadcasted_iota(jnp.int32, sq, 0)
    lo = lane < HEAD_DIM
    blockdiag = (row < HEAD_DIM) == lo
    bmat = jnp.where(blockdiag, 1.0 / HEAD_DIM, 0.0).astype(BF16)
    @pl.when(pl.program_id(1) == 0)
    def _():
        rowf = row.astype(F32)
        rel = (row - lane).astype(F32)

        def log_decay(direction, head):
            d = jnp.full(sq, dec_ref[layer, direction, head], F32)
            return -(jnp.maximum(d, 0.0) + jnp.log1p(jnp.exp(-jnp.abs(d))))

        ldf0, ldf1 = log_decay(0, 2 * hp), log_decay(0, 2 * hp + 1)
        ldb0, ldb1 = log_decay(1, 2 * hp), log_decay(1, 2 * hp + 1)
        ldf = jnp.where(lo, ldf0, ldf1)
        ldb = jnp.where(lo, ldb0, ldb1)

        def decay_mask(lf, lb):
            fwd = jnp.exp(lf * jnp.maximum(rel, 0.0))
            bwd = jnp.exp(lb * jnp.maximum(-rel, 0.0))
            return jnp.where(rel > 0, fwd, jnp.where(rel < 0, bwd, 2.0))

        cst_ref[0] = decay_mask(ldf0, ldb0)
        cst_ref[1] = decay_mask(ldf1, ldb1)
        cst_ref[2] = jnp.exp(ldf * (CHUNK - 1.0 - rowf))
        cst_ref[3] = jnp.exp(ldb * rowf)
        cst_ref[4] = jnp.exp(ldf * (rowf + 1.0))
        cst_ref[5] = jnp.exp(ldb * (CHUNK - rowf))
        cst_ref[6] = jnp.exp(ldf * float(CHUNK))
        cst_ref[7] = jnp.exp(ldb * float(CHUNK))

    def rows_of(c):
        return pl.ds(pl.multiple_of(c * CHUNK, CHUNK), CHUNK)

    def load_qk(ref, c, scale):
        x = ref[rows_of(c), :].astype(F32)
        if scale != 1.0:
            x = x * scale
        if latent:
            x = _rope(x, cos_ref[rows_of(c), :], sin_ref[rows_of(c), :])
        return x

    nchunks = nb * nc
    group = min(nchunks, RET_GROUP)

    def phase_a(i, carry):
        cs = [i * group + g for g in range(group)]
        ks = [load_qk(k_ref, c, HEAD_DIM ** -0.5) for c in cs]
        if latent:
            for c, k in zip(cs, ks):
                kr_ref[rows_of(c), :] = k
        kzs = [jnp.concatenate([(k * cst_ref[2]).astype(BF16), (k * cst_ref[3]).astype(BF16)], axis=1)
               for k in ks]
        for c, kz in zip(cs, kzs):
            uu_ref[c] = _dot_tn(kz, v_ref[rows_of(c), :])
        return carry

    lax.fori_loop(0, nchunks // group, phase_a, 0)

    zeros = jnp.zeros((HEAD_DIM, HEAD_DIM), F32)

    def embed(direction):
        top = jnp.concatenate([s0_ref[0, 0, direction, 0], zeros], axis=1)
        bot = jnp.concatenate([zeros, s0_ref[0, 0, direction, 1]], axis=1)
        return jnp.concatenate([top, bot], axis=0)

    for bi in range(nb):
        sf = embed(0) if latent else jnp.zeros(sq, F32)
        sb = embed(1) if latent else jnp.zeros(sq, F32)
        for c in range(bi * nc, (bi + 1) * nc):
            u = uu_ref[c, 0:CHUNK, :]
            uu_ref[c, 0:CHUNK, :] = sf
            sf = cst_ref[6] * sf + jnp.where(blockdiag, u, 0.0)
        for c in reversed(range(bi * nc, (bi + 1) * nc)):
            u = uu_ref[c, CHUNK:2 * CHUNK, :]
            uu_ref[c, CHUNK:2 * CHUNK, :] = sb
            sb = cst_ref[7] * sb + jnp.where(blockdiag, u, 0.0)
        if not latent:
            for d, st in ((0, sf), (1, sb)):
                sfin_ref[bi, d, 0] = st[0:HEAD_DIM, 0:HEAD_DIM]
                sfin_ref[bi, d, 1] = st[HEAD_DIM:, HEAD_DIM:]

    def phase_c(i, carry):
        cs = [i * group + g for g in range(group)]
        n = range(group)
        q = [load_qk(q_ref, c, 1.0) for c in cs]
        k = [kr_ref[rows_of(c), :] if latent else load_qk(k_ref, c, HEAD_DIM ** -0.5) for c in cs]
        v = [v_ref[rows_of(c), :] for c in cs]
        qb = [x.astype(BF16) for x in q]
        s0 = [_dot_nt(qb[g], jnp.where(lo, k[g], 0.0).astype(BF16)) for g in n]
        s1 = [_dot_nt(qb[g], jnp.where(lo, 0.0, k[g]).astype(BF16)) for g in n]
        qx = [jnp.concatenate([(x * cst_ref[4]).astype(BF16), (x * cst_ref[5]).astype(BF16)], axis=1) for x in q]
        cross = [_dot(qx[g], uu_ref[cs[g]].astype(BF16)) for g in n]
        i0 = [_dot((s0[g] * cst_ref[0]).astype(BF16), v[g]) for g in n]
        i1 = [_dot((s1[g] * cst_ref[1]).astype(BF16), v[g]) for g in n]
        o = [jnp.where(lo, i0[g], i1[g]) + cross[g] for g in n]
        mu = [_seg_mean_mxu(x, bmat, 2) for x in o]
        d = [o[g] - mu[g] for g in n]
        var = [_seg_mean_mxu(x * x, bmat, 1) for x in d]
        for g in n:
            y = d[g] * lax.rsqrt(var[g] + EPS) * gn_ref[...]
            o_ref[rows_of(cs[g]), :] = (y * rg_ref[rows_of(cs[g]), :].astype(F32)).astype(BF16)
        return carry

    lax.fori_loop(0, nchunks // group, phase_c, 0)


def _retention(z, dec, gn, layer, seq, latent, cos=None, sin=None, s0=None):
    t = z.shape[0]
    b = t // seq
    nc = seq // CHUNK
    npair = H_RET // 2
    nb = 1 if latent else RET_CTX_BATCH
    rows = nb * seq

    def zcol(base):
        return pl.BlockSpec((rows, LANES), lambda hp, bi: (bi, base // LANES + hp))

    in_specs = [
        pl.BlockSpec(memory_space=pltpu.SMEM),
        zcol(COL_RQ), zcol(COL_RK), zcol(COL_RV), zcol(COL_RG),
        pl.BlockSpec((None, 1, LANES), lambda hp, bi: (layer, 0, hp)),
    ]
    args = [dec, z, z, z, z, gn]
    out_shape = [jax.ShapeDtypeStruct((t, RET_W), BF16)]
    out_specs = [pl.BlockSpec((rows, LANES), lambda hp, bi: (bi, hp))]
    if latent:
        in_specs += [
            pl.BlockSpec((seq, LANES), lambda hp, bi: (0, 0)),
            pl.BlockSpec((seq, LANES), lambda hp, bi: (0, 0)),
            pl.BlockSpec((1, 1, 2, 2, HEAD_DIM, HEAD_DIM), lambda hp, bi: (bi, layer, 0, hp, 0, 0)),
        ]
        args += [cos, sin, s0]
    else:
        out_shape.append(jax.ShapeDtypeStruct((b, 2, H_RET, HEAD_DIM, HEAD_DIM), F32))
        out_specs.append(pl.BlockSpec((nb, 2, 2, HEAD_DIM, HEAD_DIM), lambda hp, bi: (bi, 0, hp, 0, 0)))
    res = pl.pallas_call(
        functools.partial(_ret_kernel, nb=nb, nc=nc, latent=latent, layer=layer),
        grid=(npair, b // nb),
        in_specs=in_specs,
        out_specs=out_specs,
        out_shape=out_shape,
        scratch_shapes=[pltpu.VMEM((8, CHUNK, LANES), F32), pltpu.VMEM((nb * nc, 2 * CHUNK, LANES), F32)]
        + ([pltpu.VMEM((seq, LANES), F32)] if latent else []),
        compiler_params=_cparams(("parallel", "arbitrary")),
        name="retention_latent" if latent else "retention_ctx",
    )(*args)
    return res if not latent else res[0]


def _att_kernel(*refs, tq, seq, latent):
    if latent:
        (q_ref, ak_ref, av_ref, ck_ref, cv_ref, gq_ref, gk_ref, cosq_ref, sinq_ref, cosk_ref, sink_ref,
         o_ref, k0_ref, k1_ref, vv_ref, *sp_refs) = refs
        off = PAST_LEN
    else:
        (q_ref, ak_ref, av_ref, gq_ref, gk_ref,
         o_ref, knew_ref, vnew_ref, k0_ref, k1_ref, vv_ref, *sp_refs) = refs
        off = 0

    @pl.when(pl.program_id(1) == 0)
    def _():
        lo = _lo_mask((seq, LANES))
        ak = ak_ref[...].astype(F32)
        kn = ak * lax.rsqrt(_seg_mean(ak * ak, lo) + EPS) * gk_ref[...]
        av = av_ref[...]
        if latent:
            kn = _rope(kn, cosk_ref[...], sink_ref[...])
            lo_c = _lo_mask((PAST_LEN, LANES))
            ck = ck_ref[0, 0]
            k0_ref[:, 0:off] = jnp.where(lo_c, ck, 0.0).T.astype(BF16)
            k1_ref[:, 0:off] = jnp.where(lo_c, 0.0, ck).T.astype(BF16)
            vv_ref[0:off, :] = cv_ref[0, 0].astype(BF16)
        else:
            knew_ref[...] = kn
            vnew_ref[...] = av.astype(F32)
        k0_ref[:, off:off + seq] = jnp.where(lo, kn, 0.0).T.astype(BF16)
        k1_ref[:, off:off + seq] = jnp.where(lo, 0.0, kn).T.astype(BF16)
        vv_ref[off:off + seq, :] = av

    sk = off + seq
    nkt = sk // MXU_N
    lo = _lo_mask((CHUNK, LANES))
    krefs = (k0_ref, k1_ref)
    units = [(sub, r, g) for sub in range(tq // CHUNK) for r in range(ATT_Q_W // LANES) for g in range(H_KV)]
    qb, m_lane, m_row, l_lane, o_acc, done = {}, {}, {}, {}, {}, {}

    def prep(sub, r):
        rows = slice(sub * CHUNK, (sub + 1) * CHUNK)
        x = q_ref[rows, r * LANES:(r + 1) * LANES].astype(F32)
        xn = x * lax.rsqrt(_seg_mean(x * x, lo) + EPS) * gq_ref[...]
        if latent:
            xn = _rope(xn, cosq_ref[rows, :], sinq_ref[rows, :])
        qb[(sub, r)] = (xn * (HEAD_DIM ** -0.5 * LOG2_E)).astype(BF16)

    def score_tile(i, kt):
        sub, r, g = units[i]
        cols = slice(kt * MXU_N, (kt + 1) * MXU_N)
        s_t = _dot(qb[(sub, r)], krefs[g][:, cols])
        sp_refs[i % ATT_BUFS][:, cols] = s_t
        t = jnp.maximum(s_t[:, :LANES], s_t[:, LANES:])
        m_lane[i] = t if kt == 0 else jnp.maximum(m_lane[i], t)

    def soft_tile(i, kt):
        cols = slice(kt * MXU_N, (kt + 1) * MXU_N)
        sbuf = sp_refs[i % ATT_BUFS]
        p0 = jnp.exp2(sbuf[:, kt * MXU_N:kt * MXU_N + LANES] - m_row[i])
        p1 = jnp.exp2(sbuf[:, kt * MXU_N + LANES:(kt + 1) * MXU_N] - m_row[i])
        pv = _dot(jnp.concatenate([p0, p1], axis=1).astype(BF16), vv_ref[cols, :])
        if kt == 0:
            l_lane[i], o_acc[i] = p0 + p1, pv
        else:
            l_lane[i], o_acc[i] = l_lane[i] + (p0 + p1), o_acc[i] + pv

    def finish(i):
        sub, r, g = units[i]
        done[(sub, r, g)] = o_acc.pop(i) / jnp.sum(l_lane.pop(i), axis=-1, keepdims=True)
        if g == H_KV - 1:
            rows = slice(sub * CHUNK, (sub + 1) * CHUNK)
            o = jnp.where(lo, done.pop((sub, r, 0)), done.pop((sub, r, 1)))
            o_ref[rows, r * LANES:(r + 1) * LANES] = o.astype(BF16)

    prep(units[0][0], units[0][1])
    for s in range(len(units) + ATT_LAG):
        nxt = s + H_KV
        if s % H_KV == 0 and nxt < len(units):
            prep(units[nxt][0], units[nxt][1])
        for kt in range(nkt):
            if s < len(units):
                score_tile(s, kt)
            if s >= ATT_LAG:
                soft_tile(s - ATT_LAG, kt)
        if s < len(units):
            m_row[s] = jnp.broadcast_to(jnp.max(m_lane.pop(s), axis=-1, keepdims=True), (CHUNK, LANES))
        if s >= ATT_LAG:
            finish(s - ATT_LAG)


def _attention(z, gq, gk, layer, seq, latent, cos=None, sin=None, cache_k=None, cache_v=None):
    t = z.shape[0]
    b = t // seq
    tq = 256
    nq = seq // tq
    sk = seq + (PAST_LEN if latent else 0)
    kv_spec = lambda base: pl.BlockSpec((seq, LANES), lambda bi, qi: (bi, base // LANES))
    vec_spec = _layer_spec(layer, (1, LANES))
    in_specs = [
        pl.BlockSpec((tq, ATT_Q_W), lambda bi, qi: (bi * nq + qi, COL_AQ // ATT_Q_W)),
        kv_spec(COL_AK), kv_spec(COL_AV),
    ]
    args = [z, z, z]
    if latent:
        cache_spec = pl.BlockSpec((1, 1, PAST_LEN, LANES), lambda bi, qi: (bi, layer, 0, 0))
        in_specs += [cache_spec, cache_spec]
        args += [cache_k, cache_v]
    in_specs += [vec_spec, vec_spec]
    args += [gq, gk]
    out_shape = [jax.ShapeDtypeStruct((t, ATT_Q_W), BF16)]
    out_specs = [pl.BlockSpec((tq, ATT_Q_W), lambda bi, qi: (bi * nq + qi, 0))]
    if latent:
        in_specs += [
            pl.BlockSpec((tq, LANES), lambda bi, qi: (qi, 0)),
            pl.BlockSpec((tq, LANES), lambda bi, qi: (qi, 0)),
            pl.BlockSpec((seq, LANES), lambda bi, qi: (0, 0)),
            pl.BlockSpec((seq, LANES), lambda bi, qi: (0, 0)),
        ]
        args += [cos, sin, cos, sin]
    else:
        new_spec = pl.BlockSpec((seq, LANES), lambda bi, qi: (bi, 0))
        out_shape += [jax.ShapeDtypeStruct((t, LANES), F32)] * 2
        out_specs += [new_spec, new_spec]
    res = pl.pallas_call(
        functools.partial(_att_kernel, tq=tq, seq=seq, latent=latent),
        grid=(b, nq),
        in_specs=in_specs,
        out_specs=out_specs,
        out_shape=out_shape,
        scratch_shapes=([pltpu.VMEM((LANES, sk), BF16)] * 2 + [pltpu.VMEM((sk, LANES), BF16)]
                        + [pltpu.VMEM((CHUNK, sk), F32)] * ATT_BUFS),
        compiler_params=_cparams(("parallel", "arbitrary")),
        name="attention_latent" if latent else "attention_ctx",
    )(*args)
    return res if not latent else res[0]


def _merge_kernel(x_ref, gr_ref, ga_ref, gc_ref, cu_ref, cv_ref, ro_ref, ao_ref,
                  wr_ref, wa_ref, wc_ref, wo_ref, cmn_ref, ws_ref, bs_ref, g1_ref,
                  gn2_ref, sc2_ref, sh2_ref, o_ref, h_ref, *, tm):
    u = cu_ref[...].astype(F32)
    v = cv_ref[...].astype(F32)
    vn = v * lax.rsqrt(jnp.mean(v * v, axis=-1, keepdims=True) + EPS) * cmn_ref[...]
    vb = vn.astype(BF16)
    rows = []
    for n in range(tm // CHUNK):
        cols = []
        for g in range(CM_GROUPS):
            vg = vb[n * CHUNK:(n + 1) * CHUNK, g * LANES:(g + 1) * LANES]
            cols.append(_dot(ws_ref[g], vg) + bs_ref[:, g:g + 1])
        rows.append(jnp.concatenate(cols, axis=1))
    cm = (u * jnp.concatenate(rows, axis=0)).astype(BF16)
    merged = (gr_ref[...].astype(F32) * _dot(ro_ref[...], wr_ref[...])
              + ga_ref[...].astype(F32) * _dot(ao_ref[...], wa_ref[...])
              + gc_ref[...].astype(F32) * _dot(cm, wc_ref[...]))
    y = _dot(merged.astype(BF16), wo_ref[...])
    xn = x_ref[...] + g1_ref[0] * y
    o_ref[...] = xn
    h_ref[...] = _norm_mod(xn, gn2_ref[...], sc2_ref[0], sh2_ref[0]).astype(BF16)


def _merge(x, z, ret_o, att_o, wr, wa, wc, wo, cmn, ws, bs_t, mod, gn2, layer, seq, latent):
    t = x.shape[0]
    tm = 512
    mrow = _mod_row_fn(tm, seq, latent)
    row_spec = lambda w, cb: pl.BlockSpec((tm, w), lambda i: (i, cb))
    of_layer = lambda shape: _layer_spec(layer, shape)
    return pl.pallas_call(
        functools.partial(_merge_kernel, tm=tm),
        grid=(t // tm,),
        in_specs=[
            row_spec(D_MODEL, 0),
            row_spec(D_MODEL, COL_GR // D_MODEL), row_spec(D_MODEL, COL_GA // D_MODEL),
            row_spec(D_MODEL, COL_GC // D_MODEL),
            row_spec(CM_W, COL_CU // CM_W), row_spec(CM_W, COL_CV // CM_W),
            row_spec(RET_W, 0), row_spec(ATT_Q_W, 0),
            of_layer((RET_W, D_MODEL)), of_layer((ATT_Q_W, D_MODEL)), of_layer((CM_W, D_MODEL)),
            of_layer((D_MODEL, D_MODEL)),
            of_layer((1, CM_W)), of_layer((CM_GROUPS, CHUNK, CHUNK)), of_layer((CHUNK, CM_GROUPS)),
            _mod_spec(layer, MOD_G1, mrow), of_layer((1, D_MODEL)),
            _mod_spec(layer, MOD_SC2, mrow), _mod_spec(layer, MOD_SH2, mrow),
        ],
        out_specs=[row_spec(D_MODEL, 0), row_spec(D_MODEL, 0)],
        out_shape=[jax.ShapeDtypeStruct((t, D_MODEL), F32), jax.ShapeDtypeStruct((t, D_MODEL), BF16)],
        compiler_params=_cparams(("parallel",)),
        name="merge",
    )(x, z, z, z, z, z, ret_o, att_o, wr, wa, wc, wo, cmn, ws, bs_t, mod, gn2, mod, mod)


def _ffn_kernel(x_ref, h_ref, w1_ref, w2_ref, g2_ref, gf_ref, o_ref, acc_ref, *, final):
    k = pl.program_id(1)
    a = jnp.square(jnp.maximum(_dot(h_ref[...], w1_ref[...]), 0.0)).astype(BF16)

    @pl.when(k == 0)
    def _():
        acc_ref[...] = _dot(a, w2_ref[...])

    @pl.when(k > 0)
    def _():
        acc_ref[...] += _dot(a, w2_ref[...])

    @pl.when(k == pl.num_programs(1) - 1)
    def _():
        xn = x_ref[...] + g2_ref[0] * acc_ref[...]
        if final:
            xn = xn * lax.rsqrt(jnp.mean(xn * xn, axis=-1, keepdims=True) + EPS) * gf_ref[...]
        o_ref[...] = xn


def _ffn(x, h, w1, w2, mod, gfin, layer, seq, latent, final):
    t = x.shape[0]
    tm, tf = 1024, 1024
    mrow = _mod_row_fn(tm, seq, latent)
    return pl.pallas_call(
        functools.partial(_ffn_kernel, final=final),
        grid=(t // tm, D_FF // tf),
        in_specs=[
            pl.BlockSpec((tm, D_MODEL), lambda i, k: (i, 0)),
            pl.BlockSpec((tm, D_MODEL), lambda i, k: (i, 0)),
            pl.BlockSpec((None, D_MODEL, tf), lambda i, k: (layer, 0, k)),
            pl.BlockSpec((None, tf, D_MODEL), lambda i, k: (layer, k, 0)),
            _mod_spec(layer, MOD_G2, mrow),
            pl.BlockSpec((1, D_MODEL), lambda i, k: (0, 0)),
        ],
        out_specs=pl.BlockSpec((tm, D_MODEL), lambda i, k: (i, 0)),
        out_shape=jax.ShapeDtypeStruct((t, D_MODEL), F32),
        scratch_shapes=[pltpu.VMEM((tm, D_MODEL), F32)],
        compiler_params=_cparams(("parallel", "arbitrary")),
        name="ffn_final" if final else "ffn",
    )(x, h, w1, w2, mod, gfin)


def _rope_tables(length):
    rows = length // GRID_W
    row = jnp.repeat(jnp.arange(rows, dtype=F32), GRID_W)
    col = jnp.tile(jnp.arange(GRID_W, dtype=F32), rows)
    half = HEAD_DIM // 2
    inv = 1.0 / (ROPE_BASE ** (jnp.arange(0, half, 2, dtype=F32) / half))
    ang = jnp.concatenate([row[:, None] * inv[None, :], col[:, None] * inv[None, :]], axis=-1)
    cos, sin = jnp.cos(ang), jnp.sin(ang)
    cos2 = jnp.repeat(cos, 2, axis=-1)
    sin2 = jnp.stack([-sin, sin], axis=-1).reshape(length, HEAD_DIM)
    return jnp.tile(cos2, (1, 2)), jnp.tile(sin2, (1, 2))


def _permute_w_in(w_in):
    o = np.cumsum((0, 512, 512, 512, 512, 512, 128, 128, 512, 512, 1024, 1024, 1024))
    seg = lambda i: w_in[:, :, int(o[i]):int(o[i + 1])]
    rq, rk, rv, rg, aq, ak, av, cu, cv, gr, ga, gc = (seg(i) for i in range(12))
    aq = aq.reshape(DEPTH, D_MODEL, H_Q, HEAD_DIM)[:, :, np.array(Q_HEAD_ORDER)].reshape(DEPTH, D_MODEL, ATT_Q_W)
    return jnp.concatenate([gr, ga, gc, rq, rk, rv, rg, aq, cu, cv, ak, av], axis=-1).astype(BF16)


def kernel(x_prompt, x_sample, cache_k, cache_v, state_ret, c, c_ctx, w_mod, b_mod, g_norm1, g_norm2, w_in, g_q, g_k, ret_decay, ret_gn, cm_norm, cm_ws, cm_bs, w_br_ret, w_br_att, w_br_cm, w_out, w_ff1, w_ff2, g_final):
    batch, dec_batch = x_prompt.shape[0], x_sample.shape[0]

    cond = jnp.concatenate(
        [c_ctx[None, :], c, jnp.zeros((N_MOD_ROWS - 1 - dec_batch, D_MODEL), F32)], axis=0)
    mod = _modulation(cond, w_mod, b_mod)
    mod = mod.reshape(DEPTH, N_MOD_ROWS, 6, D_MODEL).transpose(0, 2, 1, 3).reshape(DEPTH, 6, N_MOD_ROWS, 1, D_MODEL)

    w_in_p = _permute_w_in(w_in)
    w_br_ret_b = w_br_ret.astype(BF16)
    w_br_att_b = w_br_att.reshape(DEPTH, H_Q, HEAD_DIM, D_MODEL)[:, np.array(Q_HEAD_ORDER)].reshape(
        DEPTH, ATT_Q_W, D_MODEL).astype(BF16)
    w_br_cm_b = w_br_cm.astype(BF16)
    w_out_b = w_out.astype(BF16)
    w_ff1_b = w_ff1.astype(BF16)
    w_ff2_b = w_ff2.astype(BF16)
    cm_ws_b = cm_ws.astype(BF16)
    cm_bs_t = jnp.swapaxes(cm_bs, 1, 2)
    g_q2 = jnp.tile(g_q, (1, 2)).reshape(DEPTH, 1, LANES)
    g_k2 = jnp.tile(g_k, (1, 2)).reshape(DEPTH, 1, LANES)
    cos, sin = _rope_tables(DEC_SEQ)
    cache_k2 = cache_k.reshape(dec_batch, DEPTH, PAST_LEN, LANES)
    cache_v2 = cache_v.reshape(dec_batch, DEPTH, PAST_LEN, LANES)
    g_fin = g_final.reshape(1, D_MODEL)
    gn1 = g_norm1.reshape(DEPTH, 1, D_MODEL)
    gn2 = g_norm2.reshape(DEPTH, 1, D_MODEL)
    ret_gn2 = ret_gn.reshape(DEPTH, 1, RET_W)
    cm_norm2 = cm_norm.reshape(DEPTH, 1, CM_W)

    xp = x_prompt.reshape(batch * SEQ, D_MODEL)
    xs = x_sample.reshape(dec_batch * DEC_SEQ, D_MODEL)
    ks_list, vs_list, ss_list = [], [], []
    for l in range(DEPTH):
        final = l == DEPTH - 1
        for latent in (False, True):
            x = xs if latent else xp
            seq = DEC_SEQ if latent else SEQ
            z = _in_proj(x, gn1, mod, w_in_p, l, seq, latent)
            if latent:
                ret_o = _retention(z, ret_decay, ret_gn2, l, seq, True, cos, sin, state_ret)
                att_o = _attention(z, g_q2, g_k2, l, seq, True, cos, sin, cache_k2, cache_v2)
            else:
                ret_o, s_fin = _retention(z, ret_decay, ret_gn2, l, seq, False)
                att_o, k_new, v_new = _attention(z, g_q2, g_k2, l, seq, False)
                ks_list.append(k_new.reshape(batch, SEQ, H_KV, HEAD_DIM))
                vs_list.append(v_new.reshape(batch, SEQ, H_KV, HEAD_DIM))
                ss_list.append(s_fin)
            x, h2 = _merge(x, z, ret_o, att_o, w_br_ret_b, w_br_att_b, w_br_cm_b, w_out_b,
                           cm_norm2, cm_ws_b, cm_bs_t, mod, gn2, l, seq, latent)
            x = _ffn(x, h2, w_ff1_b, w_ff2_b, mod, g_fin, l, seq, latent, final)
            if latent:
                xs = x
            else:
                xp = x
    y_prompt = xp.reshape(batch, SEQ, D_MODEL)
    y_sample = xs.reshape(dec_batch, DEC_SEQ, D_MODEL)
    return (y_prompt, y_sample, jnp.stack(ks_list, axis=1), jnp.stack(vs_list, axis=1),
            jnp.stack(ss_list, axis=1))
```

```python
import functools

import numpy as np
import jax
import jax.numpy as jnp
from jax import lax
from jax.experimental import pallas as pl
from jax.experimental.pallas import tpu as pltpu

D_MODEL = 1024
DEPTH = 2
SEQ = 256
DEC_SEQ = 2048
PAST_LEN = 256
GRID_W = 64
HEAD_DIM = 64
H_RET = 8
H_Q = 8
H_KV = 2
CHUNK = 128
CM_GROUPS = 4
RET_W = 512
ATT_Q_W = 512
ATT_KV_W = 128
CM_W = 512
D_FF = 4096
ROPE_BASE = 10000.0
EPS = 1e-6
LOG2_E = 1.4426950408889634
IN_W = 6912
N_MOD_ROWS = 16
MOD_SH1, MOD_SC1, MOD_G1, MOD_SH2, MOD_SC2, MOD_G2 = range(6)

LANES = 128
MXU_N = 256
RET_GROUP = 16
RET_CTX_BATCH = 8
ATT_TQ = 512
ATT_LAG = 2
ATT_BUFS = ATT_LAG + 1
VMEM_LIMIT = 48 * 1024 * 1024

F32 = jnp.float32
BF16 = jnp.bfloat16

COL_GR, COL_GA, COL_GC = 0, 1024, 2048
COL_RQ, COL_RK, COL_RV, COL_RG = 3072, 3584, 4096, 4608
COL_AQ, COL_CU, COL_CV = 5120, 5632, 6144
COL_AK, COL_AV = 6656, 6784
Q_HEAD_ORDER = (0, 4, 1, 5, 2, 6, 3, 7)


def _cparams(sem):
    return pltpu.CompilerParams(dimension_semantics=sem, vmem_limit_bytes=VMEM_LIMIT)


def _dot(a, b):
    return jnp.dot(a, b, preferred_element_type=F32)


def _dot_nt(a, b):
    return lax.dot_general(a, b, (((1,), (1,)), ((), ())), preferred_element_type=F32)


def _dot_tn(a, b):
    return lax.dot_general(a, b, (((0,), (0,)), ((), ())), preferred_element_type=F32)


def _lo_mask(shape):
    return lax.broadcasted_iota(jnp.int32, shape, len(shape) - 1) < HEAD_DIM


def _seg_mean(x, lo):
    s_lo = jnp.sum(jnp.where(lo, x, 0.0), axis=-1, keepdims=True)
    s_hi = jnp.sum(jnp.where(lo, 0.0, x), axis=-1, keepdims=True)
    return jnp.where(lo, s_lo, s_hi) * (1.0 / HEAD_DIM)


def _rope(x, cos, sin_signed):
    nxt = pltpu.roll(x, LANES - 1, 1)
    prv = pltpu.roll(x, 1, 1)
    even = (lax.broadcasted_iota(jnp.int32, x.shape, 1) & 1) == 0
    return x * cos + jnp.where(even, nxt, prv) * sin_signed


def _sigmoid(x):
    return 0.5 * jnp.tanh(0.5 * x) + 0.5


def _silu(x):
    return x * _sigmoid(x)


def _seg_mean_mxu(x, bmat, passes):
    hi = x.astype(BF16)
    out = _dot(hi, bmat)
    if passes == 2:
        out = out + _dot((x - hi.astype(F32)).astype(BF16), bmat)
    return out


def _norm_mod(x, g, sc, sh):
    ms = jnp.mean(x * x, axis=-1, keepdims=True)
    y = x * lax.rsqrt(ms + EPS) * g
    return y * (1.0 + sc) + sh


def _mod_kernel(c_ref, w_ref, b_ref, o_ref):
    s = _silu(c_ref[...])
    o_ref[0] = _dot(s.astype(BF16), w_ref[0].astype(BF16)) + b_ref[0]


def _modulation(cond, w_mod, b_mod):
    tn = 1536
    n = 6 * D_MODEL
    return pl.pallas_call(
        _mod_kernel,
        grid=(DEPTH, n // tn),
        in_specs=[
            pl.BlockSpec((N_MOD_ROWS, D_MODEL), lambda l, j: (0, 0)),
            pl.BlockSpec((1, D_MODEL, tn), lambda l, j: (l, 0, j)),
            pl.BlockSpec((1, 1, tn), lambda l, j: (l, 0, j)),
        ],
        out_specs=pl.BlockSpec((1, N_MOD_ROWS, tn), lambda l, j: (l, 0, j)),
        out_shape=jax.ShapeDtypeStruct((DEPTH, N_MOD_ROWS, n), F32),
        compiler_params=_cparams(("arbitrary", "arbitrary")),
        name="modulation",
    )(cond, w_mod, b_mod.reshape(DEPTH, 1, n))


def _mod_row_fn(tm, seq, latent):
    if latent:
        return lambda i: 1 + (i * tm) // seq
    return lambda i: 0


def _mod_spec(layer, which, mrow):
    return pl.BlockSpec((None, None, 1, 1, D_MODEL), lambda i, *_: (layer, which, mrow(i), 0, 0))


def _layer_spec(layer, shape):
    return pl.BlockSpec((None,) + shape, lambda *_: (layer,) + (0,) * len(shape))


def _proj_epilogues(tn):
    kinds = ((COL_GR, COL_RQ, _sigmoid), (COL_RQ, COL_RG, None), (COL_RG, COL_AQ, _silu),
             (COL_AQ, COL_CU, None), (COL_CU, COL_AK, functools.partial(jax.nn.gelu, approximate=True)),
             (COL_AK, IN_W, None))
    tiles = [(c, c + MXU_N, fn) for a, e, fn in kinds for c in range(a, e, MXU_N)]
    return [[(a - b * tn, e - b * tn, fn) for a, e, fn in tiles if b * tn <= a < (b + 1) * tn]
            for b in range(IN_W // tn)]


def _proj_kernel(x_ref, g_ref, sc_ref, sh_ref, w_ref, o_ref, h_ref, *, tn):
    j = pl.program_id(1)

    @pl.when(j == 0)
    def _():
        h_ref[...] = _norm_mod(x_ref[...], g_ref[...], sc_ref[0], sh_ref[0]).astype(BF16)

    for b, segs in enumerate(_proj_epilogues(tn)):
        @pl.when(j == b)
        def _(segs=segs):
            for a, e, fn in segs:
                y = _dot(h_ref[...], w_ref[:, a:e])
                o_ref[:, a:e] = (y if fn is None else fn(y)).astype(BF16)


def _in_proj(x, g, mod, w, layer, seq, latent):
    t = x.shape[0]
    tm, tn = 1024, 2304
    mrow = _mod_row_fn(tm, seq, latent)
    return pl.pallas_call(
        functools.partial(_proj_kernel, tn=tn),
        grid=(t // tm, IN_W // tn),
        in_specs=[
            pl.BlockSpec((tm, D_MODEL), lambda i, j: (i, 0)),
            _layer_spec(layer, (1, D_MODEL)),
            _mod_spec(layer, MOD_SC1, mrow), _mod_spec(layer, MOD_SH1, mrow),
            pl.BlockSpec((None, D_MODEL, tn), lambda i, j: (layer, 0, j)),
        ],
        out_specs=pl.BlockSpec((tm, tn), lambda i, j: (i, j)),
        out_shape=jax.ShapeDtypeStruct((t, IN_W), BF16),
        scratch_shapes=[pltpu.VMEM((tm, D_MODEL), BF16)],
        compiler_params=_cparams(("parallel", "arbitrary")),
        name="in_proj",
    )(x, g, mod, mod, w)


def _ret_kernel(*refs, nb, nc, latent, layer):
    if latent:
        (dec_ref, q_ref, k_ref, v_ref, rg_ref, gn_ref, cos_ref, sin_ref, s0_ref,
         o_ref, cst_ref, uu_ref, kr_ref) = refs
    else:
        (dec_ref, q_ref, k_ref, v_ref, rg_ref, gn_ref,
         o_ref, sfin_ref, cst_ref, uu_ref) = refs
    hp = pl.program_id(0)
    sq = (CHUNK, LANES)
    lane = lax.broadcasted_iota(jnp.int32, sq, 1)
    row = lax.broadcasted_iota(jnp.int32, sq, 0)
    lo = lane < HEAD_DIM
    blockdiag = (row < HEAD_DIM) == lo
    bmat = jnp.where(blockdiag, 1.0 / HEAD_DIM, 0.0).astype(BF16)
    @pl.when(pl.program_id(1) == 0)
    def _():
        rowf = row.astype(F32)
        rel = (row - lane).astype(F32)

        def log_decay(direction, head):
            d = jnp.full(sq, dec_ref[layer, direction, head], F32)
            return -(jnp.maximum(d, 0.0) + jnp.log1p(jnp.exp(-jnp.abs(d))))

        ldf0, ldf1 = log_decay(0, 2 * hp), log_decay(0, 2 * hp + 1)
        ldb0, ldb1 = log_decay(1, 2 * hp), log_decay(1, 2 * hp + 1)
        ldf = jnp.where(lo, ldf0, ldf1)
        ldb = jnp.where(lo, ldb0, ldb1)

        def decay_mask(lf, lb):
            fwd = jnp.exp(lf * jnp.maximum(rel, 0.0))
            bwd = jnp.exp(lb * jnp.maximum(-rel, 0.0))
            return jnp.where(rel > 0, fwd, jnp.where(rel < 0, bwd, 2.0))

        cst_ref[0] = decay_mask(ldf0, ldb0)
        cst_ref[1] = decay_mask(ldf1, ldb1)
        cst_ref[2] = jnp.exp(ldf * (CHUNK - 1.0 - rowf))
        cst_ref[3] = jnp.exp(ldb * rowf)
        cst_ref[4] = jnp.exp(ldf * (rowf + 1.0))
        cst_ref[5] = jnp.exp(ldb * (CHUNK - rowf))
        cst_ref[6] = jnp.exp(ldf * float(CHUNK))
        cst_ref[7] = jnp.exp(ldb * float(CHUNK))

    def rows_of(c):
        return pl.ds(pl.multiple_of(c * CHUNK, CHUNK), CHUNK)

    def load_qk(ref, c, scale):
        x = ref[rows_of(c), :].astype(F32)
        if scale != 1.0:
            x = x * scale
        if latent:
            x = _rope(x, cos_ref[rows_of(c), :], sin_ref[rows_of(c), :])
        return x

    nchunks = nb * nc
    group = min(nchunks, RET_GROUP)

    def phase_a(i, carry):
        cs = [i * group + g for g in range(group)]
        ks = [load_qk(k_ref, c, HEAD_DIM ** -0.5) for c in cs]
        if latent:
            for c, k in zip(cs, ks):
                kr_ref[rows_of(c), :] = k
        kzs = [jnp.concatenate([(k * cst_ref[2]).astype(BF16), (k * cst_ref[3]).astype(BF16)], axis=1)
               for k in ks]
        for c, kz in zip(cs, kzs):
            uu_ref[c] = _dot_tn(kz, v_ref[rows_of(c), :])
        return carry

    lax.fori_loop(0, nchunks // group, phase_a, 0)

    zeros = jnp.zeros((HEAD_DIM, HEAD_DIM), F32)

    def embed(direction):
        top = jnp.concatenate([s0_ref[0, 0, direction, 0], zeros], axis=1)
        bot = jnp.concatenate([zeros, s0_ref[0, 0, direction, 1]], axis=1)
        return jnp.concatenate([top, bot], axis=0)

    for bi in range(nb):
        sf = embed(0) if latent else jnp.zeros(sq, F32)
        sb = embed(1) if latent else jnp.zeros(sq, F32)
        for c in range(bi * nc, (bi + 1) * nc):
            u = uu_ref[c, 0:CHUNK, :]
            uu_ref[c, 0:CHUNK, :] = sf
            sf = cst_ref[6] * sf + jnp.where(blockdiag, u, 0.0)
        for c in reversed(range(bi * nc, (bi + 1) * nc)):
            u = uu_ref[c, CHUNK:2 * CHUNK, :]
            uu_ref[c, CHUNK:2 * CHUNK, :] = sb
            sb = cst_ref[7] * sb + jnp.where(blockdiag, u, 0.0)
        if not latent:
            for d, st in ((0, sf), (1, sb)):
                sfin_ref[bi, d, 0] = st[0:HEAD_DIM, 0:HEAD_DIM]
                sfin_ref[bi, d, 1] = st[HEAD_DIM:, HEAD_DIM:]

    def phase_c(i, carry):
        cs = [i * group + g for g in range(group)]
        n = range(group)
        q = [load_qk(q_ref, c, 1.0) for c in cs]
        k = [kr_ref[rows_of(c), :] if latent else load_qk(k_ref, c, HEAD_DIM ** -0.5) for c in cs]
        v = [v_ref[rows_of(c), :] for c in cs]
        qb = [x.astype(BF16) for x in q]
        s0 = [_dot_nt(qb[g], jnp.where(lo, k[g], 0.0).astype(BF16)) for g in n]
        s1 = [_dot_nt(qb[g], jnp.where(lo, 0.0, k[g]).astype(BF16)) for g in n]
        qx = [jnp.concatenate([(x * cst_ref[4]).astype(BF16), (x * cst_ref[5]).astype(BF16)], axis=1) for x in q]
        cross = [_dot(qx[g], uu_ref[cs[g]].astype(BF16)) for g in n]
        i0 = [_dot((s0[g] * cst_ref[0]).astype(BF16), v[g]) for g in n]
        i1 = [_dot((s1[g] * cst_ref[1]).astype(BF16), v[g]) for g in n]
        o = [jnp.where(lo, i0[g], i1[g]) + cross[g] for g in n]
        mu = [_seg_mean_mxu(x, bmat, 2) for x in o]
        d = [o[g] - mu[g] for g in n]
        var = [_seg_mean_mxu(x * x, bmat, 1) for x in d]
        for g in n:
            y = d[g] * lax.rsqrt(var[g] + EPS) * gn_ref[...]
            o_ref[rows_of(cs[g]), :] = (y * rg_ref[rows_of(cs[g]), :].astype(F32)).astype(BF16)
        return carry

    lax.fori_loop(0, nchunks // group, phase_c, 0)


def _retention(z, dec, gn, layer, seq, latent, cos=None, sin=None, s0=None):
    t = z.shape[0]
    b = t // seq
    nc = seq // CHUNK
    npair = H_RET // 2
    nb = 1 if latent else RET_CTX_BATCH
    rows = nb * seq

    def zcol(base):
        return pl.BlockSpec((rows, LANES), lambda hp, bi: (bi, base // LANES + hp))

    in_specs = [
        pl.BlockSpec(memory_space=pltpu.SMEM),
        zcol(COL_RQ), zcol(COL_RK), zcol(COL_RV), zcol(COL_RG),
        pl.BlockSpec((None, 1, LANES), lambda hp, bi: (layer, 0, hp)),
    ]
    args = [dec, z, z, z, z, gn]
    out_shape = [jax.ShapeDtypeStruct((t, RET_W), BF16)]
    out_specs = [pl.BlockSpec((rows, LANES), lambda hp, bi: (bi, hp))]
    if latent:
        in_specs += [
            pl.BlockSpec((seq, LANES), lambda hp, bi: (0, 0)),
            pl.BlockSpec((seq, LANES), lambda hp, bi: (0, 0)),
            pl.BlockSpec((1, 1, 2, 2, HEAD_DIM, HEAD_DIM), lambda hp, bi: (bi, layer, 0, hp, 0, 0)),
        ]
        args += [cos, sin, s0]
    else:
        out_shape.append(jax.ShapeDtypeStruct((b, 2, H_RET, HEAD_DIM, HEAD_DIM), F32))
        out_specs.append(pl.BlockSpec((nb, 2, 2, HEAD_DIM, HEAD_DIM), lambda hp, bi: (bi, 0, hp, 0, 0)))
    res = pl.pallas_call(
        functools.partial(_ret_kernel, nb=nb, nc=nc, latent=latent, layer=layer),
        grid=(npair, b // nb),
        in_specs=in_specs,
        out_specs=out_specs,
        out_shape=out_shape,
        scratch_shapes=[pltpu.VMEM((8, CHUNK, LANES), F32), pltpu.VMEM((nb * nc, 2 * CHUNK, LANES), F32)]
        + ([pltpu.VMEM((seq, LANES), F32)] if latent else []),
        compiler_params=_cparams(("parallel", "arbitrary")),
        name="retention_latent" if latent else "retention_ctx",
    )(*args)
    return res if not latent else res[0]


def _att_kernel(*refs, tq, seq, latent):
    if latent:
        (q_ref, ak_ref, av_ref, ck_ref, cv_ref, gq_ref, gk_ref, cosq_ref, sinq_ref, cosk_ref, sink_ref,
         o_ref, k0_ref, k1_ref, vv_ref, *sp_refs) = refs
        off = PAST_LEN
    else:
        (q_ref, ak_ref, av_ref, gq_ref, gk_ref,
         o_ref, knew_ref, vnew_ref, k0_ref, k1_ref, vv_ref, *sp_refs) = refs
        off = 0

    @pl.when(pl.program_id(1) == 0)
    def _():
        lo = _lo_mask((seq, LANES))
        ak = ak_ref[...].astype(F32)
        kn = ak * lax.rsqrt(_seg_mean(ak * ak, lo) + EPS) * gk_ref[...]
        av = av_ref[...]
        if latent:
            kn = _rope(kn, cosk_ref[...], sink_ref[...])
            lo_c = _lo_mask((PAST_LEN, LANES))
            ck = ck_ref[0, 0]
            k0_ref[:, 0:off] = jnp.where(lo_c, ck, 0.0).T.astype(BF16)
            k1_ref[:, 0:off] = jnp.where(lo_c, 0.0, ck).T.astype(BF16)
            vv_ref[0:off, :] = cv_ref[0, 0].astype(BF16)
        else:
            knew_ref[...] = kn
            vnew_ref[...] = av.astype(F32)
        k0_ref[:, off:off + seq] = jnp.where(lo, kn, 0.0).T.astype(BF16)
        k1_ref[:, off:off + seq] = jnp.where(lo, 0.0, kn).T.astype(BF16)
        vv_ref[off:off + seq, :] = av

    sk = off + seq
    nkt = sk // MXU_N
    lo = _lo_mask((CHUNK, LANES))
    krefs = (k0_ref, k1_ref)
    units = [(sub, r, g) for sub in range(tq // CHUNK) for r in range(ATT_Q_W // LANES) for g in range(H_KV)]
    qb, m_lane, m_row, l_lane, o_acc, done = {}, {}, {}, {}, {}, {}

    def prep(sub, r):
        rows = slice(sub * CHUNK, (sub + 1) * CHUNK)
        x = q_ref[rows, r * LANES:(r + 1) * LANES].astype(F32)
        xn = x * lax.rsqrt(_seg_mean(x * x, lo) + EPS) * gq_ref[...]
        if latent:
            xn = _rope(xn, cosq_ref[rows, :], sinq_ref[rows, :])
        qb[(sub, r)] = (xn * (HEAD_DIM ** -0.5 * LOG2_E)).astype(BF16)

    def score_tile(i, kt):
        sub, r, g = units[i]
        cols = slice(kt * MXU_N, (kt + 1) * MXU_N)
        s_t = _dot(qb[(sub, r)], krefs[g][:, cols])
        sp_refs[i % ATT_BUFS][:, cols] = s_t
        t = jnp.maximum(s_t[:, :LANES], s_t[:, LANES:])
        m_lane[i] = t if kt == 0 else jnp.maximum(m_lane[i], t)

    def soft_tile(i, kt):
        cols = slice(kt * MXU_N, (kt + 1) * MXU_N)
        sbuf = sp_refs[i % ATT_BUFS]
        p0 = jnp.exp2(sbuf[:, kt * MXU_N:kt * MXU_N + LANES] - m_row[i])
        p1 = jnp.exp2(sbuf[:, kt * MXU_N + LANES:(kt + 1) * MXU_N] - m_row[i])
        pv = _dot(jnp.concatenate([p0, p1], axis=1).astype(BF16), vv_ref[cols, :])
        if kt == 0:
            l_lane[i], o_acc[i] = p0 + p1, pv
        else:
            l_lane[i], o_acc[i] = l_lane[i] + (p0 + p1), o_acc[i] + pv

    def finish(i):
        sub, r, g = units[i]
        done[(sub, r, g)] = o_acc.pop(i) / jnp.sum(l_lane.pop(i), axis=-1, keepdims=True)
        if g == H_KV - 1:
            rows = slice(sub * CHUNK, (sub + 1) * CHUNK)
            o = jnp.where(lo, done.pop((sub, r, 0)), done.pop((sub, r, 1)))
            o_ref[rows, r * LANES:(r + 1) * LANES] = o.astype(BF16)

    prep(units[0][0], units[0][1])
    for s in range(len(units) + ATT_LAG):
        nxt = s + H_KV
        if s % H_KV == 0 and nxt < len(units):
            prep(units[nxt][0], units[nxt][1])
        for kt in range(nkt):
            if s < len(units):
                score_tile(s, kt)
            if s >= ATT_LAG:
                soft_tile(s - ATT_LAG, kt)
        if s < len(units):
            m_row[s] = jnp.broadcast_to(jnp.max(m_lane.pop(s), axis=-1, keepdims=True), (CHUNK, LANES))
        if s >= ATT_LAG:
            finish(s - ATT_LAG)


def _attention(z, gq, gk, layer, seq, latent, cos=None, sin=None, cache_k=None, cache_v=None):
    t = z.shape[0]
    b = t // seq
    tq = min(seq, ATT_TQ)
    nq = seq // tq
    sk = seq + (PAST_LEN if latent else 0)
    kv_spec = lambda base: pl.BlockSpec((seq, LANES), lambda bi, qi: (bi, base // LANES))
    vec_spec = _layer_spec(layer, (1, LANES))
    in_specs = [
        pl.BlockSpec((tq, ATT_Q_W), lambda bi, qi: (bi * nq + qi, COL_AQ // ATT_Q_W)),
        kv_spec(COL_AK), kv_spec(COL_AV),
    ]
    args = [z, z, z]
    if latent:
        cache_spec = pl.BlockSpec((1, 1, PAST_LEN, LANES), lambda bi, qi: (bi, layer, 0, 0))
        in_specs += [cache_spec, cache_spec]
        args += [cache_k, cache_v]
    in_specs += [vec_spec, vec_spec]
    args += [gq, gk]
    out_shape = [jax.ShapeDtypeStruct((t, ATT_Q_W), BF16)]
    out_specs = [pl.BlockSpec((tq, ATT_Q_W), lambda bi, qi: (bi * nq + qi, 0))]
    if latent:
        in_specs += [
            pl.BlockSpec((tq, LANES), lambda bi, qi: (qi, 0)),
            pl.BlockSpec((tq, LANES), lambda bi, qi: (qi, 0)),
            pl.BlockSpec((seq, LANES), lambda bi, qi: (0, 0)),
            pl.BlockSpec((seq, LANES), lambda bi, qi: (0, 0)),
        ]
        args += [cos, sin, cos, sin]
    else:
        new_spec = pl.BlockSpec((seq, LANES), lambda bi, qi: (bi, 0))
        out_shape += [jax.ShapeDtypeStruct((t, LANES), F32)] * 2
        out_specs += [new_spec, new_spec]
    res = pl.pallas_call(
        functools.partial(_att_kernel, tq=tq, seq=seq, latent=latent),
        grid=(b, nq),
        in_specs=in_specs,
        out_specs=out_specs,
        out_shape=out_shape,
        scratch_shapes=([pltpu.VMEM((LANES, sk), BF16)] * 2 + [pltpu.VMEM((sk, LANES), BF16)]
                        + [pltpu.VMEM((CHUNK, sk), F32)] * ATT_BUFS),
        compiler_params=_cparams(("parallel", "arbitrary")),
        name="attention_latent" if latent else "attention_ctx",
    )(*args)
    return res if not latent else res[0]


def _merge_kernel(x_ref, gr_ref, ga_ref, gc_ref, cu_ref, cv_ref, ro_ref, ao_ref,
                  wr_ref, wa_ref, wc_ref, wo_ref, cmn_ref, ws_ref, bs_ref, g1_ref,
                  gn2_ref, sc2_ref, sh2_ref, o_ref, h_ref, *, tm):
    u = cu_ref[...].astype(F32)
    v = cv_ref[...].astype(F32)
    vn = v * lax.rsqrt(jnp.mean(v * v, axis=-1, keepdims=True) + EPS) * cmn_ref[...]
    vb = vn.astype(BF16)
    rows = []
    for n in range(tm // CHUNK):
        cols = []
        for g in range(CM_GROUPS):
            vg = vb[n * CHUNK:(n + 1) * CHUNK, g * LANES:(g + 1) * LANES]
            cols.append(_dot(ws_ref[g], vg) + bs_ref[:, g:g + 1])
        rows.append(jnp.concatenate(cols, axis=1))
    cm = (u * jnp.concatenate(rows, axis=0)).astype(BF16)
    merged = (gr_ref[...].astype(F32) * _dot(ro_ref[...], wr_ref[...])
              + ga_ref[...].astype(F32) * _dot(ao_ref[...], wa_ref[...])
              + gc_ref[...].astype(F32) * _dot(cm, wc_ref[...]))
    y = _dot(merged.astype(BF16), wo_ref[...])
    xn = x_ref[...] + g1_ref[0] * y
    o_ref[...] = xn
    h_ref[...] = _norm_mod(xn, gn2_ref[...], sc2_ref[0], sh2_ref[0]).astype(BF16)


def _merge(x, z, ret_o, att_o, wr, wa, wc, wo, cmn, ws, bs_t, mod, gn2, layer, seq, latent):
    t = x.shape[0]
    tm = 512
    mrow = _mod_row_fn(tm, seq, latent)
    row_spec = lambda w, cb: pl.BlockSpec((tm, w), lambda i: (i, cb))
    of_layer = lambda shape: _layer_spec(layer, shape)
    return pl.pallas_call(
        functools.partial(_merge_kernel, tm=tm),
        grid=(t // tm,),
        in_specs=[
            row_spec(D_MODEL, 0),
            row_spec(D_MODEL, COL_GR // D_MODEL), row_spec(D_MODEL, COL_GA // D_MODEL),
            row_spec(D_MODEL, COL_GC // D_MODEL),
            row_spec(CM_W, COL_CU // CM_W), row_spec(CM_W, COL_CV // CM_W),
            row_spec(RET_W, 0), row_spec(ATT_Q_W, 0),
            of_layer((RET_W, D_MODEL)), of_layer((ATT_Q_W, D_MODEL)), of_layer((CM_W, D_MODEL)),
            of_layer((D_MODEL, D_MODEL)),
            of_layer((1, CM_W)), of_layer((CM_GROUPS, CHUNK, CHUNK)), of_layer((CHUNK, CM_GROUPS)),
            _mod_spec(layer, MOD_G1, mrow), of_layer((1, D_MODEL)),
            _mod_spec(layer, MOD_SC2, mrow), _mod_spec(layer, MOD_SH2, mrow),
        ],
        out_specs=[row_spec(D_MODEL, 0), row_spec(D_MODEL, 0)],
        out_shape=[jax.ShapeDtypeStruct((t, D_MODEL), F32), jax.ShapeDtypeStruct((t, D_MODEL), BF16)],
        compiler_params=_cparams(("parallel",)),
        name="merge",
    )(x, z, z, z, z, z, ret_o, att_o, wr, wa, wc, wo, cmn, ws, bs_t, mod, gn2, mod, mod)


def _ffn_kernel(x_ref, h_ref, w1_ref, w2_ref, g2_ref, gf_ref, o_ref, acc_ref, *, final):
    k = pl.program_id(1)
    a = jnp.square(jnp.maximum(_dot(h_ref[...], w1_ref[...]), 0.0)).astype(BF16)

    @pl.when(k == 0)
    def _():
        acc_ref[...] = _dot(a, w2_ref[...])

    @pl.when(k > 0)
    def _():
        acc_ref[...] += _dot(a, w2_ref[...])

    @pl.when(k == pl.num_programs(1) - 1)
    def _():
        xn = x_ref[...] + g2_ref[0] * acc_ref[...]
        if final:
            xn = xn * lax.rsqrt(jnp.mean(xn * xn, axis=-1, keepdims=True) + EPS) * gf_ref[...]
        o_ref[...] = xn


def _ffn(x, h, w1, w2, mod, gfin, layer, seq, latent, final):
    t = x.shape[0]
    tm, tf = 1024, 1024
    mrow = _mod_row_fn(tm, seq, latent)
    return pl.pallas_call(
        functools.partial(_ffn_kernel, final=final),
        grid=(t // tm, D_FF // tf),
        in_specs=[
            pl.BlockSpec((tm, D_MODEL), lambda i, k: (i, 0)),
            pl.BlockSpec((tm, D_MODEL), lambda i, k: (i, 0)),
            pl.BlockSpec((None, D_MODEL, tf), lambda i, k: (layer, 0, k)),
            pl.BlockSpec((None, tf, D_MODEL), lambda i, k: (layer, k, 0)),
            _mod_spec(layer, MOD_G2, mrow),
            pl.BlockSpec((1, D_MODEL), lambda i, k: (0, 0)),
        ],
        out_specs=pl.BlockSpec((tm, D_MODEL), lambda i, k: (i, 0)),
        out_shape=jax.ShapeDtypeStruct((t, D_MODEL), F32),
        scratch_shapes=[pltpu.VMEM((tm, D_MODEL), F32)],
        compiler_params=_cparams(("parallel", "arbitrary")),
        name="ffn_final" if final else "ffn",
    )(x, h, w1, w2, mod, gfin)


def _rope_tables(length):
    rows = length // GRID_W
    row = jnp.repeat(jnp.arange(rows, dtype=F32), GRID_W)
    col = jnp.tile(jnp.arange(GRID_W, dtype=F32), rows)
    half = HEAD_DIM // 2
    inv = 1.0 / (ROPE_BASE ** (jnp.arange(0, half, 2, dtype=F32) / half))
    ang = jnp.concatenate([row[:, None] * inv[None, :], col[:, None] * inv[None, :]], axis=-1)
    cos, sin = jnp.cos(ang), jnp.sin(ang)
    cos2 = jnp.repeat(cos, 2, axis=-1)
    sin2 = jnp.stack([-sin, sin], axis=-1).reshape(length, HEAD_DIM)
    return jnp.tile(cos2, (1, 2)), jnp.tile(sin2, (1, 2))


def _permute_w_in(w_in):
    o = np.cumsum((0, 512, 512, 512, 512, 512, 128, 128, 512, 512, 1024, 1024, 1024))
    seg = lambda i: w_in[:, :, int(o[i]):int(o[i + 1])]
    rq, rk, rv, rg, aq, ak, av, cu, cv, gr, ga, gc = (seg(i) for i in range(12))
    aq = aq.reshape(DEPTH, D_MODEL, H_Q, HEAD_DIM)[:, :, np.array(Q_HEAD_ORDER)].reshape(DEPTH, D_MODEL, ATT_Q_W)
    return jnp.concatenate([gr, ga, gc, rq, rk, rv, rg, aq, cu, cv, ak, av], axis=-1).astype(BF16)


def kernel(x_prompt, x_sample, cache_k, cache_v, state_ret, c, c_ctx, w_mod, b_mod, g_norm1, g_norm2, w_in, g_q, g_k, ret_decay, ret_gn, cm_norm, cm_ws, cm_bs, w_br_ret, w_br_att, w_br_cm, w_out, w_ff1, w_ff2, g_final):
    batch, dec_batch = x_prompt.shape[0], x_sample.shape[0]

    cond = jnp.concatenate(
        [c_ctx[None, :], c, jnp.zeros((N_MOD_ROWS - 1 - dec_batch, D_MODEL), F32)], axis=0)
    mod = _modulation(cond, w_mod, b_mod)
    mod = mod.reshape(DEPTH, N_MOD_ROWS, 6, D_MODEL).transpose(0, 2, 1, 3).reshape(DEPTH, 6, N_MOD_ROWS, 1, D_MODEL)

    w_in_p = _permute_w_in(w_in)
    w_br_ret_b = w_br_ret.astype(BF16)
    w_br_att_b = w_br_att.reshape(DEPTH, H_Q, HEAD_DIM, D_MODEL)[:, np.array(Q_HEAD_ORDER)].reshape(
        DEPTH, ATT_Q_W, D_MODEL).astype(BF16)
    w_br_cm_b = w_br_cm.astype(BF16)
    w_out_b = w_out.astype(BF16)
    w_ff1_b = w_ff1.astype(BF16)
    w_ff2_b = w_ff2.astype(BF16)
    cm_ws_b = cm_ws.astype(BF16)
    cm_bs_t = jnp.swapaxes(cm_bs, 1, 2)
    g_q2 = jnp.tile(g_q, (1, 2)).reshape(DEPTH, 1, LANES)
    g_k2 = jnp.tile(g_k, (1, 2)).reshape(DEPTH, 1, LANES)
    cos, sin = _rope_tables(DEC_SEQ)
    cache_k2 = cache_k.reshape(dec_batch, DEPTH, PAST_LEN, LANES)
    cache_v2 = cache_v.reshape(dec_batch, DEPTH, PAST_LEN, LANES)
    g_fin = g_final.reshape(1, D_MODEL)
    gn1 = g_norm1.reshape(DEPTH, 1, D_MODEL)
    gn2 = g_norm2.reshape(DEPTH, 1, D_MODEL)
    ret_gn2 = ret_gn.reshape(DEPTH, 1, RET_W)
    cm_norm2 = cm_norm.reshape(DEPTH, 1, CM_W)

    xp = x_prompt.reshape(batch * SEQ, D_MODEL)
    xs = x_sample.reshape(dec_batch * DEC_SEQ, D_MODEL)
    ks_list, vs_list, ss_list = [], [], []
    for l in range(DEPTH):
        final = l == DEPTH - 1
        for latent in (False, True):
            x = xs if latent else xp
            seq = DEC_SEQ if latent else SEQ
            z = _in_proj(x, gn1, mod, w_in_p, l, seq, latent)
            if latent:
                ret_o = _retention(z, ret_decay, ret_gn2, l, seq, True, cos, sin, state_ret)
                att_o = _attention(z, g_q2, g_k2, l, seq, True, cos, sin, cache_k2, cache_v2)
            else:
                ret_o, s_fin = _retention(z, ret_decay, ret_gn2, l, seq, False)
                att_o, k_new, v_new = _attention(z, g_q2, g_k2, l, seq, False)
                ks_list.append(k_new.reshape(batch, SEQ, H_KV, HEAD_DIM))
                vs_list.append(v_new.reshape(batch, SEQ, H_KV, HEAD_DIM))
                ss_list.append(s_fin)
            x, h2 = _merge(x, z, ret_o, att_o, w_br_ret_b, w_br_att_b, w_br_cm_b, w_out_b,
                           cm_norm2, cm_ws_b, cm_bs_t, mod, gn2, l, seq, latent)
            x = _ffn(x, h2, w_ff1_b, w_ff2_b, mod, g_fin, l, seq, latent, final)
            if latent:
                xs = x
            else:
                xp = x
    y_prompt = xp.reshape(batch, SEQ, D_MODEL)
    y_sample = xs.reshape(dec_batch, DEC_SEQ, D_MODEL)
    return (y_prompt, y_sample, jnp.stack(ks_list, axis=1), jnp.stack(vs_list, axis=1),
            jnp.stack(ss_list, axis=1))
```

```python
import functools

import numpy as np
import jax
import jax.numpy as jnp
from jax import lax
from jax.experimental import pallas as pl
from jax.experimental.pallas import tpu as pltpu

D_MODEL = 1024
DEPTH = 2
SEQ = 256
DEC_SEQ = 2048
PAST_LEN = 256
GRID_W = 64
HEAD_DIM = 64
H_RET = 8
H_Q = 8
H_KV = 2
CHUNK = 128
CM_GROUPS = 4
RET_W = 512
ATT_Q_W = 512
ATT_KV_W = 128
CM_W = 512
D_FF = 4096
ROPE_BASE = 10000.0
EPS = 1e-6
LOG2_E = 1.4426950408889634
IN_W = 6912
N_MOD_ROWS = 16
MOD_SH1, MOD_SC1, MOD_G1, MOD_SH2, MOD_SC2, MOD_G2 = range(6)

LANES = 128
MXU_N = 256
RET_GROUP = 16
RET_CTX_BATCH = 8
ATT_TQ = 512
ATT_LAG = 2
ATT_BUFS = ATT_LAG + 1
VMEM_LIMIT = 48 * 1024 * 1024

F32 = jnp.float32
BF16 = jnp.bfloat16

COL_GR, COL_GA, COL_GC = 0, 1024, 2048
COL_RQ, COL_RK, COL_RV, COL_RG = 3072, 3584, 4096, 4608
COL_AQ, COL_CU, COL_CV = 5120, 5632, 6144
COL_AK, COL_AV = 6656, 6784
Q_HEAD_ORDER = (0, 4, 1, 5, 2, 6, 3, 7)


def _cparams(sem):
    return pltpu.CompilerParams(dimension_semantics=sem, vmem_limit_bytes=VMEM_LIMIT)


def _dot(a, b):
    return jnp.dot(a, b, preferred_element_type=F32)


def _dot_nt(a, b):
    return lax.dot_general(a, b, (((1,), (1,)), ((), ())), preferred_element_type=F32)


def _dot_tn(a, b):
    return lax.dot_general(a, b, (((0,), (0,)), ((), ())), preferred_element_type=F32)


def _lo_mask(shape):
    return lax.broadcasted_iota(jnp.int32, shape, len(shape) - 1) < HEAD_DIM


def _seg_mean(x, lo):
    s_lo = jnp.sum(jnp.where(lo, x, 0.0), axis=-1, keepdims=True)
    s_hi = jnp.sum(jnp.where(lo, 0.0, x), axis=-1, keepdims=True)
    return jnp.where(lo, s_lo, s_hi) * (1.0 / HEAD_DIM)


def _rope(x, cos, sin_signed):
    nxt = pltpu.roll(x, LANES - 1, 1)
    prv = pltpu.roll(x, 1, 1)
    even = (lax.broadcasted_iota(jnp.int32, x.shape, 1) & 1) == 0
    return x * cos + jnp.where(even, nxt, prv) * sin_signed


def _sigmoid(x):
    return 0.5 * jnp.tanh(0.5 * x) + 0.5


def _silu(x):
    return x * _sigmoid(x)


def _seg_mean_mxu(x, bmat, passes):
    hi = x.astype(BF16)
    out = _dot(hi, bmat)
    if passes == 2:
        out = out + _dot((x - hi.astype(F32)).astype(BF16), bmat)
    return out


def _norm_mod(x, g, sc, sh):
    ms = jnp.mean(x * x, axis=-1, keepdims=True)
    y = x * lax.rsqrt(ms + EPS) * g
    return y * (1.0 + sc) + sh


def _mod_kernel(c_ref, w_ref, b_ref, o_ref):
    s = _silu(c_ref[...])
    o_ref[0] = _dot(s.astype(BF16), w_ref[0].astype(BF16)) + b_ref[0]


def _modulation(cond, w_mod, b_mod):
    tn = 1536
    n = 6 * D_MODEL
    return pl.pallas_call(
        _mod_kernel,
        grid=(DEPTH, n // tn),
        in_specs=[
            pl.BlockSpec((N_MOD_ROWS, D_MODEL), lambda l, j: (0, 0)),
            pl.BlockSpec((1, D_MODEL, tn), lambda l, j: (l, 0, j)),
            pl.BlockSpec((1, 1, tn), lambda l, j: (l, 0, j)),
        ],
        out_specs=pl.BlockSpec((1, N_MOD_ROWS, tn), lambda l, j: (l, 0, j)),
        out_shape=jax.ShapeDtypeStruct((DEPTH, N_MOD_ROWS, n), F32),
        compiler_params=_cparams(("arbitrary", "arbitrary")),
        name="modulation",
    )(cond, w_mod, b_mod.reshape(DEPTH, 1, n))


def _mod_row_fn(tm, seq, latent):
    if latent:
        return lambda i: 1 + (i * tm) // seq
    return lambda i: 0


def _mod_spec(layer, which, mrow):
    return pl.BlockSpec((None, None, 1, 1, D_MODEL), lambda i, *_: (layer, which, mrow(i), 0, 0))


def _layer_spec(layer, shape):
    return pl.BlockSpec((None,) + shape, lambda *_: (layer,) + (0,) * len(shape))


def _proj_epilogues(tn):
    kinds = ((COL_GR, COL_RQ, "sigmoid"), (COL_RQ, COL_RV, "rope"), (COL_RV, COL_RG, None),
             (COL_RG, COL_AQ, "silu"), (COL_AQ, COL_CU, None), (COL_CU, COL_AK, "gelu"), (COL_AK, IN_W, None))
    tiles = [(c, c + MXU_N, kind) for a, e, kind in kinds for c in range(a, e, MXU_N)]
    return [[(a - b * tn, e - b * tn, kind) for a, e, kind in tiles if b * tn <= a < (b + 1) * tn]
            for b in range(IN_W // tn)]


def _proj_kernel(*refs, tn, latent):
    if latent:
        x_ref, g_ref, sc_ref, sh_ref, w_ref, cos_ref, sin_ref, o_ref, h_ref = refs
    else:
        x_ref, g_ref, sc_ref, sh_ref, w_ref, o_ref, h_ref = refs
    j = pl.program_id(1)

    def rope_tile(y):
        if not latent:
            return y
        halves = [_rope(y[:, c:c + LANES], cos_ref[...], sin_ref[...]) for c in range(0, MXU_N, LANES)]
        return jnp.concatenate(halves, axis=1)

    epilogue = {None: lambda y: y, "sigmoid": _sigmoid, "silu": _silu, "rope": rope_tile,
                "gelu": functools.partial(jax.nn.gelu, approximate=True)}

    @pl.when(j == 0)
    def _():
        h_ref[...] = _norm_mod(x_ref[...], g_ref[...], sc_ref[0], sh_ref[0]).astype(BF16)

    for b, segs in enumerate(_proj_epilogues(tn)):
        @pl.when(j == b)
        def _(segs=segs):
            for a, e, kind in segs:
                o_ref[:, a:e] = epilogue[kind](_dot(h_ref[...], w_ref[:, a:e])).astype(BF16)


def _in_proj(x, g, mod, w, layer, seq, latent, cos=None, sin=None):
    t = x.shape[0]
    tm, tn = 1024, 2304
    mrow = _mod_row_fn(tm, seq, latent)
    return pl.pallas_call(
        functools.partial(_proj_kernel, tn=tn, latent=latent),
        grid=(t // tm, IN_W // tn),
        in_specs=[
            pl.BlockSpec((tm, D_MODEL), lambda i, j: (i, 0)),
            _layer_spec(layer, (1, D_MODEL)),
            _mod_spec(layer, MOD_SC1, mrow), _mod_spec(layer, MOD_SH1, mrow),
            pl.BlockSpec((None, D_MODEL, tn), lambda i, j: (layer, 0, j)),
        ] + ([pl.BlockSpec((tm, LANES), lambda i, j: (i % (seq // tm), 0))] * 2 if latent else []),
        out_specs=pl.BlockSpec((tm, tn), lambda i, j: (i, j)),
        out_shape=jax.ShapeDtypeStruct((t, IN_W), BF16),
        scratch_shapes=[pltpu.VMEM((tm, D_MODEL), BF16)],
        compiler_params=_cparams(("parallel", "arbitrary")),
        name="in_proj",
    )(x, g, mod, mod, w, *((cos, sin) if latent else ()))


def _ret_kernel(*refs, nb, nc, latent, layer):
    if latent:
        (dec_ref, q_ref, k_ref, v_ref, rg_ref, gn_ref, s0_ref,
         o_ref, cst_ref, uu_ref) = refs
    else:
        (dec_ref, q_ref, k_ref, v_ref, rg_ref, gn_ref,
         o_ref, sfin_ref, cst_ref, uu_ref) = refs
    hp = pl.program_id(0)
    sq = (CHUNK, LANES)
    lane = lax.broadcasted_iota(jnp.int32, sq, 1)
    row = lax.broadcasted_iota(jnp.int32, sq, 0)
    lo = lane < HEAD_DIM
    blockdiag = (row < HEAD_DIM) == lo
    bmat = jnp.where(blockdiag, 1.0 / HEAD_DIM, 0.0).astype(BF16)
    @pl.when(pl.program_id(1) == 0)
    def _():
        rowf = row.astype(F32)
        rel = (row - lane).astype(F32)

        def log_decay(direction, head):
            d = jnp.full(sq, dec_ref[layer, direction, head], F32)
            return -(jnp.maximum(d, 0.0) + jnp.log1p(jnp.exp(-jnp.abs(d))))

        ldf0, ldf1 = log_decay(0, 2 * hp), log_decay(0, 2 * hp + 1)
        ldb0, ldb1 = log_decay(1, 2 * hp), log_decay(1, 2 * hp + 1)
        ldf = jnp.where(lo, ldf0, ldf1)
        ldb = jnp.where(lo, ldb0, ldb1)

        def decay_mask(lf, lb):
            fwd = jnp.exp(lf * jnp.maximum(rel, 0.0))
            bwd = jnp.exp(lb * jnp.maximum(-rel, 0.0))
            return jnp.where(rel > 0, fwd, jnp.where(rel < 0, bwd, 2.0))

        cst_ref[0] = decay_mask(ldf0, ldb0)
        cst_ref[1] = decay_mask(ldf1, ldb1)
        cst_ref[2] = jnp.exp(ldf * (CHUNK - 1.0 - rowf))
        cst_ref[3] = jnp.exp(ldb * rowf)
        cst_ref[4] = jnp.exp(ldf * (rowf + 1.0))
        cst_ref[5] = jnp.exp(ldb * (CHUNK - rowf))
        cst_ref[6] = jnp.exp(ldf * float(CHUNK))
        cst_ref[7] = jnp.exp(ldb * float(CHUNK))

    def rows_of(c):
        return pl.ds(pl.multiple_of(c * CHUNK, CHUNK), CHUNK)

    def load_qk(ref, c, scale):
        x = ref[rows_of(c), :].astype(F32)
        return x if scale == 1.0 else x * scale

    nchunks = nb * nc
    group = min(nchunks, RET_GROUP)

    def phase_a(i, carry):
        cs = [i * group + g for g in range(group)]
        ks = [load_qk(k_ref, c, HEAD_DIM ** -0.5) for c in cs]
        kzs = [jnp.concatenate([(k * cst_ref[2]).astype(BF16), (k * cst_ref[3]).astype(BF16)], axis=1)
               for k in ks]
        for c, kz in zip(cs, kzs):
            uu_ref[c] = _dot_tn(kz, v_ref[rows_of(c), :])
        return carry

    lax.fori_loop(0, nchunks // group, phase_a, 0)

    zeros = jnp.zeros((HEAD_DIM, HEAD_DIM), F32)

    def embed(direction):
        top = jnp.concatenate([s0_ref[0, 0, direction, 0], zeros], axis=1)
        bot = jnp.concatenate([zeros, s0_ref[0, 0, direction, 1]], axis=1)
        return jnp.concatenate([top, bot], axis=0)

    for bi in range(nb):
        sf = embed(0) if latent else jnp.zeros(sq, F32)
        sb = embed(1) if latent else jnp.zeros(sq, F32)
        for c in range(bi * nc, (bi + 1) * nc):
            u = uu_ref[c, 0:CHUNK, :]
            uu_ref[c, 0:CHUNK, :] = sf
            sf = cst_ref[6] * sf + jnp.where(blockdiag, u, 0.0)
        for c in reversed(range(bi * nc, (bi + 1) * nc)):
            u = uu_ref[c, CHUNK:2 * CHUNK, :]
            uu_ref[c, CHUNK:2 * CHUNK, :] = sb
            sb = cst_ref[7] * sb + jnp.where(blockdiag, u, 0.0)
        if not latent:
            for d, st in ((0, sf), (1, sb)):
                sfin_ref[bi, d, 0] = st[0:HEAD_DIM, 0:HEAD_DIM]
                sfin_ref[bi, d, 1] = st[HEAD_DIM:, HEAD_DIM:]

    def phase_c(i, carry):
        cs = [i * group + g for g in range(group)]
        n = range(group)
        q = [load_qk(q_ref, c, 1.0) for c in cs]
        k = [load_qk(k_ref, c, HEAD_DIM ** -0.5) for c in cs]
        v = [v_ref[rows_of(c), :] for c in cs]
        qb = [x.astype(BF16) for x in q]
        s0 = [_dot_nt(qb[g], jnp.where(lo, k[g], 0.0).astype(BF16)) for g in n]
        s1 = [_dot_nt(qb[g], jnp.where(lo, 0.0, k[g]).astype(BF16)) for g in n]
        qx = [jnp.concatenate([(x * cst_ref[4]).astype(BF16), (x * cst_ref[5]).astype(BF16)], axis=1) for x in q]
        cross = [_dot(qx[g], uu_ref[cs[g]].astype(BF16)) for g in n]
        i0 = [_dot((s0[g] * cst_ref[0]).astype(BF16), v[g]) for g in n]
        i1 = [_dot((s1[g] * cst_ref[1]).astype(BF16), v[g]) for g in n]
        o = [jnp.where(lo, i0[g], i1[g]) + cross[g] for g in n]
        mu = [_seg_mean_mxu(x, bmat, 2) for x in o]
        d = [o[g] - mu[g] for g in n]
        var = [_seg_mean_mxu(x * x, bmat, 1) for x in d]
        for g in n:
            y = d[g] * lax.rsqrt(var[g] + EPS) * gn_ref[...]
            o_ref[rows_of(cs[g]), :] = (y * rg_ref[rows_of(cs[g]), :].astype(F32)).astype(BF16)
        return carry

    lax.fori_loop(0, nchunks // group, phase_c, 0)


def _retention(z, dec, gn, layer, seq, latent, s0=None):
    t = z.shape[0]
    b = t // seq
    nc = seq // CHUNK
    npair = H_RET // 2
    nb = 1 if latent else RET_CTX_BATCH
    rows = nb * seq

    def zcol(base):
        return pl.BlockSpec((rows, LANES), lambda hp, bi: (bi, base // LANES + hp))

    in_specs = [
        pl.BlockSpec(memory_space=pltpu.SMEM),
        zcol(COL_RQ), zcol(COL_RK), zcol(COL_RV), zcol(COL_RG),
        pl.BlockSpec((None, 1, LANES), lambda hp, bi: (layer, 0, hp)),
    ]
    args = [dec, z, z, z, z, gn]
    out_shape = [jax.ShapeDtypeStruct((t, RET_W), BF16)]
    out_specs = [pl.BlockSpec((rows, LANES), lambda hp, bi: (bi, hp))]
    if latent:
        in_specs.append(pl.BlockSpec((1, 1, 2, 2, HEAD_DIM, HEAD_DIM), lambda hp, bi: (bi, layer, 0, hp, 0, 0)))
        args.append(s0)
    else:
        out_shape.append(jax.ShapeDtypeStruct((b, 2, H_RET, HEAD_DIM, HEAD_DIM), F32))
        out_specs.append(pl.BlockSpec((nb, 2, 2, HEAD_DIM, HEAD_DIM), lambda hp, bi: (bi, 0, hp, 0, 0)))
    res = pl.pallas_call(
        functools.partial(_ret_kernel, nb=nb, nc=nc, latent=latent, layer=layer),
        grid=(npair, b // nb),
        in_specs=in_specs,
        out_specs=out_specs,
        out_shape=out_shape,
        scratch_shapes=[pltpu.VMEM((8, CHUNK, LANES), F32), pltpu.VMEM((nb * nc, 2 * CHUNK, LANES), F32)],
        compiler_params=_cparams(("parallel", "arbitrary")),
        name="retention_latent" if latent else "retention_ctx",
    )(*args)
    return res if not latent else res[0]


def _att_kernel(*refs, tq, seq, latent):
    if latent:
        (q_ref, ak_ref, av_ref, ck_ref, cv_ref, gq_ref, gk_ref, cosq_ref, sinq_ref, cosk_ref, sink_ref,
         o_ref, k0_ref, k1_ref, vv_ref, *sp_refs) = refs
        off = PAST_LEN
    else:
        (q_ref, ak_ref, av_ref, gq_ref, gk_ref,
         o_ref, knew_ref, vnew_ref, k0_ref, k1_ref, vv_ref, *sp_refs) = refs
        off = 0

    @pl.when(pl.program_id(1) == 0)
    def _():
        lo = _lo_mask((seq, LANES))
        ak = ak_ref[...].astype(F32)
        kn = ak * lax.rsqrt(_seg_mean(ak * ak, lo) + EPS) * gk_ref[...]
        av = av_ref[...]
        if latent:
            kn = _rope(kn, cosk_ref[...], sink_ref[...])
            lo_c = _lo_mask((PAST_LEN, LANES))
            ck = ck_ref[0, 0]
            k0_ref[:, 0:off] = jnp.where(lo_c, ck, 0.0).T.astype(BF16)
            k1_ref[:, 0:off] = jnp.where(lo_c, 0.0, ck).T.astype(BF16)
            vv_ref[0:off, :] = cv_ref[0, 0].astype(BF16)
        else:
            knew_ref[...] = kn
            vnew_ref[...] = av.astype(F32)
        k0_ref[:, off:off + seq] = jnp.where(lo, kn, 0.0).T.astype(BF16)
        k1_ref[:, off:off + seq] = jnp.where(lo, 0.0, kn).T.astype(BF16)
        vv_ref[off:off + seq, :] = av

    sk = off + seq
    nkt = sk // MXU_N
    lo = _lo_mask((CHUNK, LANES))
    krefs = (k0_ref, k1_ref)
    units = [(sub, r, g) for sub in range(tq // CHUNK) for r in range(ATT_Q_W // LANES) for g in range(H_KV)]
    qb, m_lane, m_row, l_lane, o_acc, done = {}, {}, {}, {}, {}, {}

    def prep(sub, r):
        rows = slice(sub * CHUNK, (sub + 1) * CHUNK)
        x = q_ref[rows, r * LANES:(r + 1) * LANES].astype(F32)
        xn = x * lax.rsqrt(_seg_mean(x * x, lo) + EPS) * gq_ref[...]
        if latent:
            xn = _rope(xn, cosq_ref[rows, :], sinq_ref[rows, :])
        qb[(sub, r)] = (xn * (HEAD_DIM ** -0.5 * LOG2_E)).astype(BF16)

    def score_tile(i, kt):
        sub, r, g = units[i]
        cols = slice(kt * MXU_N, (kt + 1) * MXU_N)
        s_t = _dot(qb[(sub, r)], krefs[g][:, cols])
        sp_refs[i % ATT_BUFS][:, cols] = s_t
        t = jnp.maximum(s_t[:, :LANES], s_t[:, LANES:])
        m_lane[i] = t if kt == 0 else jnp.maximum(m_lane[i], t)

    def soft_tile(i, kt):
        cols = slice(kt * MXU_N, (kt + 1) * MXU_N)
        sbuf = sp_refs[i % ATT_BUFS]
        p0 = jnp.exp2(sbuf[:, kt * MXU_N:kt * MXU_N + LANES] - m_row[i])
        p1 = jnp.exp2(sbuf[:, kt * MXU_N + LANES:(kt + 1) * MXU_N] - m_row[i])
        pv = _dot(jnp.concatenate([p0, p1], axis=1).astype(BF16), vv_ref[cols, :])
        if kt == 0:
            l_lane[i], o_acc[i] = p0 + p1, pv
        else:
            l_lane[i], o_acc[i] = l_lane[i] + (p0 + p1), o_acc[i] + pv

    def finish(i):
        sub, r, g = units[i]
        done[(sub, r, g)] = o_acc.pop(i) / jnp.sum(l_lane.pop(i), axis=-1, keepdims=True)
        if g == H_KV - 1:
            rows = slice(sub * CHUNK, (sub + 1) * CHUNK)
            o = jnp.where(lo, done.pop((sub, r, 0)), done.pop((sub, r, 1)))
            o_ref[rows, r * LANES:(r + 1) * LANES] = o.astype(BF16)

    prep(units[0][0], units[0][1])
    for s in range(len(units) + ATT_LAG):
        nxt = s + H_KV
        if s % H_KV == 0 and nxt < len(units):
            prep(units[nxt][0], units[nxt][1])
        for kt in range(nkt):
            if s < len(units):
                score_tile(s, kt)
            if s >= ATT_LAG:
                soft_tile(s - ATT_LAG, kt)
        if s < len(units):
            m_row[s] = jnp.broadcast_to(jnp.max(m_lane.pop(s), axis=-1, keepdims=True), (CHUNK, LANES))
        if s >= ATT_LAG:
            finish(s - ATT_LAG)


def _attention(z, gq, gk, layer, seq, latent, cos=None, sin=None, cache_k=None, cache_v=None):
    t = z.shape[0]
    b = t // seq
    tq = min(seq, ATT_TQ)
    nq = seq // tq
    sk = seq + (PAST_LEN if latent else 0)
    kv_spec = lambda base: pl.BlockSpec((seq, LANES), lambda bi, qi: (bi, base // LANES))
    vec_spec = _layer_spec(layer, (1, LANES))
    in_specs = [
        pl.BlockSpec((tq, ATT_Q_W), lambda bi, qi: (bi * nq + qi, COL_AQ // ATT_Q_W)),
        kv_spec(COL_AK), kv_spec(COL_AV),
    ]
    args = [z, z, z]
    if latent:
        cache_spec = pl.BlockSpec((1, 1, PAST_LEN, LANES), lambda bi, qi: (bi, layer, 0, 0))
        in_specs += [cache_spec, cache_spec]
        args += [cache_k, cache_v]
    in_specs += [vec_spec, vec_spec]
    args += [gq, gk]
    out_shape = [jax.ShapeDtypeStruct((t, ATT_Q_W), BF16)]
    out_specs = [pl.BlockSpec((tq, ATT_Q_W), lambda bi, qi: (bi * nq + qi, 0))]
    if latent:
        in_specs += [
            pl.BlockSpec((tq, LANES), lambda bi, qi: (qi, 0)),
            pl.BlockSpec((tq, LANES), lambda bi, qi: (qi, 0)),
            pl.BlockSpec((seq, LANES), lambda bi, qi: (0, 0)),
            pl.BlockSpec((seq, LANES), lambda bi, qi: (0, 0)),
        ]
        args += [cos, sin, cos, sin]
    else:
        new_spec = pl.BlockSpec((seq, LANES), lambda bi, qi: (bi, 0))
        out_shape += [jax.ShapeDtypeStruct((t, LANES), F32)] * 2
        out_specs += [new_spec, new_spec]
    res = pl.pallas_call(
        functools.partial(_att_kernel, tq=tq, seq=seq, latent=latent),
        grid=(b, nq),
        in_specs=in_specs,
        out_specs=out_specs,
        out_shape=out_shape,
        scratch_shapes=([pltpu.VMEM((LANES, sk), BF16)] * 2 + [pltpu.VMEM((sk, LANES), BF16)]
                        + [pltpu.VMEM((CHUNK, sk), F32)] * ATT_BUFS),
        compiler_params=_cparams(("parallel", "arbitrary")),
        name="attention_latent" if latent else "attention_ctx",
    )(*args)
    return res if not latent else res[0]


def _merge_kernel(x_ref, gr_ref, ga_ref, gc_ref, cu_ref, cv_ref, ro_ref, ao_ref,
                  wr_ref, wa_ref, wc_ref, wo_ref, cmn_ref, ws_ref, bs_ref, g1_ref,
                  gn2_ref, sc2_ref, sh2_ref, o_ref, h_ref, *, tm):
    u = cu_ref[...].astype(F32)
    v = cv_ref[...].astype(F32)
    vn = v * lax.rsqrt(jnp.mean(v * v, axis=-1, keepdims=True) + EPS) * cmn_ref[...]
    vb = vn.astype(BF16)
    rows = []
    for n in range(tm // CHUNK):
        cols = []
        for g in range(CM_GROUPS):
            vg = vb[n * CHUNK:(n + 1) * CHUNK, g * LANES:(g + 1) * LANES]
            cols.append(_dot(ws_ref[g], vg) + bs_ref[:, g:g + 1])
        rows.append(jnp.concatenate(cols, axis=1))
    cm = (u * jnp.concatenate(rows, axis=0)).astype(BF16)
    merged = (gr_ref[...].astype(F32) * _dot(ro_ref[...], wr_ref[...])
              + ga_ref[...].astype(F32) * _dot(ao_ref[...], wa_ref[...])
              + gc_ref[...].astype(F32) * _dot(cm, wc_ref[...]))
    mb = merged.astype(BF16)
    for r0 in range(0, tm, tm // 2):
        rows = slice(r0, r0 + tm // 2)
        xn = x_ref[rows, :] + g1_ref[0] * _dot(mb[rows, :], wo_ref[...])
        o_ref[rows, :] = xn
        h_ref[rows, :] = _norm_mod(xn, gn2_ref[...], sc2_ref[0], sh2_ref[0]).astype(BF16)


def _merge(x, z, ret_o, att_o, wr, wa, wc, wo, cmn, ws, bs_t, mod, gn2, layer, seq, latent):
    t = x.shape[0]
    tm = 512
    mrow = _mod_row_fn(tm, seq, latent)
    row_spec = lambda w, cb: pl.BlockSpec((tm, w), lambda i: (i, cb))
    of_layer = lambda shape: _layer_spec(layer, shape)
    return pl.pallas_call(
        functools.partial(_merge_kernel, tm=tm),
        grid=(t // tm,),
        in_specs=[
            row_spec(D_MODEL, 0),
            row_spec(D_MODEL, COL_GR // D_MODEL), row_spec(D_MODEL, COL_GA // D_MODEL),
            row_spec(D_MODEL, COL_GC // D_MODEL),
            row_spec(CM_W, COL_CU // CM_W), row_spec(CM_W, COL_CV // CM_W),
            row_spec(RET_W, 0), row_spec(ATT_Q_W, 0),
            of_layer((RET_W, D_MODEL)), of_layer((ATT_Q_W, D_MODEL)), of_layer((CM_W, D_MODEL)),
            of_layer((D_MODEL, D_MODEL)),
            of_layer((1, CM_W)), of_layer((CM_GROUPS, CHUNK, CHUNK)), of_layer((CHUNK, CM_GROUPS)),
            _mod_spec(layer, MOD_G1, mrow), of_layer((1, D_MODEL)),
            _mod_spec(layer, MOD_SC2, mrow), _mod_spec(layer, MOD_SH2, mrow),
        ],
        out_specs=[row_spec(D_MODEL, 0), row_spec(D_MODEL, 0)],
        out_shape=[jax.ShapeDtypeStruct((t, D_MODEL), F32), jax.ShapeDtypeStruct((t, D_MODEL), BF16)],
        compiler_params=_cparams(("parallel",)),
        name="merge",
    )(x, z, z, z, z, z, ret_o, att_o, wr, wa, wc, wo, cmn, ws, bs_t, mod, gn2, mod, mod)


def _ffn_kernel(x_ref, h_ref, w1_ref, w2_ref, g2_ref, gf_ref, o_ref, acc_ref, *, final):
    k = pl.program_id(1)
    a = jnp.square(jnp.maximum(_dot(h_ref[...], w1_ref[...]), 0.0)).astype(BF16)

    @pl.when(k == 0)
    def _():
        acc_ref[...] = _dot(a, w2_ref[...])

    @pl.when(k > 0)
    def _():
        acc_ref[...] += _dot(a, w2_ref[...])

    @pl.when(k == pl.num_programs(1) - 1)
    def _():
        xn = x_ref[...] + g2_ref[0] * acc_ref[...]
        if final:
            xn = xn * lax.rsqrt(jnp.mean(xn * xn, axis=-1, keepdims=True) + EPS) * gf_ref[...]
        o_ref[...] = xn


def _ffn(x, h, w1, w2, mod, gfin, layer, seq, latent, final):
    t = x.shape[0]
    tm, tf = 1024, 1024
    mrow = _mod_row_fn(tm, seq, latent)
    return pl.pallas_call(
        functools.partial(_ffn_kernel, final=final),
        grid=(t // tm, D_FF // tf),
        in_specs=[
            pl.BlockSpec((tm, D_MODEL), lambda i, k: (i, 0)),
            pl.BlockSpec((tm, D_MODEL), lambda i, k: (i, 0)),
            pl.BlockSpec((None, D_MODEL, tf), lambda i, k: (layer, 0, k)),
            pl.BlockSpec((None, tf, D_MODEL), lambda i, k: (layer, k, 0)),
            _mod_spec(layer, MOD_G2, mrow),
            pl.BlockSpec((1, D_MODEL), lambda i, k: (0, 0)),
        ],
        out_specs=pl.BlockSpec((tm, D_MODEL), lambda i, k: (i, 0)),
        out_shape=jax.ShapeDtypeStruct((t, D_MODEL), F32),
        scratch_shapes=[pltpu.VMEM((tm, D_MODEL), F32)],
        compiler_params=_cparams(("parallel", "arbitrary")),
        name="ffn_final" if final else "ffn",
    )(x, h, w1, w2, mod, gfin)


def _rope_tables(length):
    rows = length // GRID_W
    row = jnp.repeat(jnp.arange(rows, dtype=F32), GRID_W)
    col = jnp.tile(jnp.arange(GRID_W, dtype=F32), rows)
    half = HEAD_DIM // 2
    inv = 1.0 / (ROPE_BASE ** (jnp.arange(0, half, 2, dtype=F32) / half))
    ang = jnp.concatenate([row[:, None] * inv[None, :], col[:, None] * inv[None, :]], axis=-1)
    cos, sin = jnp.cos(ang), jnp.sin(ang)
    cos2 = jnp.repeat(cos, 2, axis=-1)
    sin2 = jnp.stack([-sin, sin], axis=-1).reshape(length, HEAD_DIM)
    return jnp.tile(cos2, (1, 2)), jnp.tile(sin2, (1, 2))


def _permute_w_in(w_in):
    o = np.cumsum((0, 512, 512, 512, 512, 512, 128, 128, 512, 512, 1024, 1024, 1024))
    seg = lambda i: w_in[:, :, int(o[i]):int(o[i + 1])]
    rq, rk, rv, rg, aq, ak, av, cu, cv, gr, ga, gc = (seg(i) for i in range(12))
    aq = aq.reshape(DEPTH, D_MODEL, H_Q, HEAD_DIM)[:, :, np.array(Q_HEAD_ORDER)].reshape(DEPTH, D_MODEL, ATT_Q_W)
    return jnp.concatenate([gr, ga, gc, rq, rk, rv, rg, aq, cu, cv, ak, av], axis=-1).astype(BF16)


def kernel(x_prompt, x_sample, cache_k, cache_v, state_ret, c, c_ctx, w_mod, b_mod, g_norm1, g_norm2, w_in, g_q, g_k, ret_decay, ret_gn, cm_norm, cm_ws, cm_bs, w_br_ret, w_br_att, w_br_cm, w_out, w_ff1, w_ff2, g_final):
    batch, dec_batch = x_prompt.shape[0], x_sample.shape[0]

    cond = jnp.concatenate(
        [c_ctx[None, :], c, jnp.zeros((N_MOD_ROWS - 1 - dec_batch, D_MODEL), F32)], axis=0)
    mod = _modulation(cond, w_mod, b_mod)
    mod = mod.reshape(DEPTH, N_MOD_ROWS, 6, D_MODEL).transpose(0, 2, 1, 3).reshape(DEPTH, 6, N_MOD_ROWS, 1, D_MODEL)

    w_in_p = _permute_w_in(w_in)
    w_br_ret_b = w_br_ret.astype(BF16)
    w_br_att_b = w_br_att.reshape(DEPTH, H_Q, HEAD_DIM, D_MODEL)[:, np.array(Q_HEAD_ORDER)].reshape(
        DEPTH, ATT_Q_W, D_MODEL).astype(BF16)
    w_br_cm_b = w_br_cm.astype(BF16)
    w_out_b = w_out.astype(BF16)
    w_ff1_b = w_ff1.astype(BF16)
    w_ff2_b = w_ff2.astype(BF16)
    cm_ws_b = cm_ws.astype(BF16)
    cm_bs_t = jnp.swapaxes(cm_bs, 1, 2)
    g_q2 = jnp.tile(g_q, (1, 2)).reshape(DEPTH, 1, LANES)
    g_k2 = jnp.tile(g_k, (1, 2)).reshape(DEPTH, 1, LANES)
    cos, sin = _rope_tables(DEC_SEQ)
    cache_k2 = cache_k.reshape(dec_batch, DEPTH, PAST_LEN, LANES)
    cache_v2 = cache_v.reshape(dec_batch, DEPTH, PAST_LEN, LANES)
    g_fin = g_final.reshape(1, D_MODEL)
    gn1 = g_norm1.reshape(DEPTH, 1, D_MODEL)
    gn2 = g_norm2.reshape(DEPTH, 1, D_MODEL)
    ret_gn2 = ret_gn.reshape(DEPTH, 1, RET_W)
    cm_norm2 = cm_norm.reshape(DEPTH, 1, CM_W)

    xp = x_prompt.reshape(batch * SEQ, D_MODEL)
    xs = x_sample.reshape(dec_batch * DEC_SEQ, D_MODEL)
    ks_list, vs_list, ss_list = [], [], []
    for l in range(DEPTH):
        final = l == DEPTH - 1
        for latent in (False, True):
            x = xs if latent else xp
            seq = DEC_SEQ if latent else SEQ
            z = _in_proj(x, gn1, mod, w_in_p, l, seq, latent, cos, sin)
            if latent:
                ret_o = _retention(z, ret_decay, ret_gn2, l, seq, True, state_ret)
                att_o = _attention(z, g_q2, g_k2, l, seq, True, cos, sin, cache_k2, cache_v2)
            else:
                ret_o, s_fin = _retention(z, ret_decay, ret_gn2, l, seq, False)
                att_o, k_new, v_new = _attention(z, g_q2, g_k2, l, seq, False)
                ks_list.append(k_new.reshape(batch, SEQ, H_KV, HEAD_DIM))
                vs_list.append(v_new.reshape(batch, SEQ, H_KV, HEAD_DIM))
                ss_list.append(s_fin)
            x, h2 = _merge(x, z, ret_o, att_o, w_br_ret_b, w_br_att_b, w_br_cm_b, w_out_b,
                           cm_norm2, cm_ws_b, cm_bs_t, mod, gn2, l, seq, latent)
            x = _ffn(x, h2, w_ff1_b, w_ff2_b, mod, g_fin, l, seq, latent, final)
            if latent:
                xs = x
            else:
                xp = x
    y_prompt = xp.reshape(batch, SEQ, D_MODEL)
    y_sample = xs.reshape(dec_batch, DEC_SEQ, D_MODEL)
    return (y_prompt, y_sample, jnp.stack(ks_list, axis=1), jnp.stack(vs_list, axis=1),
            jnp.stack(ss_list, axis=1))
```

```python
import functools

import numpy as np
import jax
import jax.numpy as jnp
from jax import lax
from jax.experimental import pallas as pl
from jax.experimental.pallas import tpu as pltpu

D_MODEL = 1024
DEPTH = 2
SEQ = 256
DEC_SEQ = 2048
PAST_LEN = 256
GRID_W = 64
HEAD_DIM = 64
H_RET = 8
H_Q = 8
H_KV = 2
CHUNK = 128
CM_GROUPS = 4
RET_W = 512
ATT_Q_W = 512
ATT_KV_W = 128
CM_W = 512
D_FF = 4096
ROPE_BASE = 10000.0
EPS = 1e-6
LOG2_E = 1.4426950408889634
IN_W = 6912
N_MOD_ROWS = 16
MOD_SH1, MOD_SC1, MOD_G1, MOD_SH2, MOD_SC2, MOD_G2 = range(6)

LANES = 128
MXU_N = 256
RET_GROUP = 16
RET_CTX_BATCH = 8
ATT_TQ = 512
ATT_LAG = 2
ATT_BUFS = ATT_LAG + 1
VMEM_LIMIT = 48 * 1024 * 1024

F32 = jnp.float32
BF16 = jnp.bfloat16

COL_GR, COL_GA, COL_GC = 0, 1024, 2048
COL_RQ, COL_RK, COL_RV, COL_RG = 3072, 3584, 4096, 4608
COL_AQ, COL_CU, COL_CV = 5120, 5632, 6144
COL_AK, COL_AV = 6656, 6784
Q_HEAD_ORDER = (0, 4, 1, 5, 2, 6, 3, 7)


def _cparams(sem):
    return pltpu.CompilerParams(dimension_semantics=sem, vmem_limit_bytes=VMEM_LIMIT)


def _dot(a, b):
    return jnp.dot(a, b, preferred_element_type=F32)


def _dot_nt(a, b):
    return lax.dot_general(a, b, (((1,), (1,)), ((), ())), preferred_element_type=F32)


def _dot_tn(a, b):
    return lax.dot_general(a, b, (((0,), (0,)), ((), ())), preferred_element_type=F32)


def _lo_mask(shape):
    return lax.broadcasted_iota(jnp.int32, shape, len(shape) - 1) < HEAD_DIM


def _seg_mean(x, lo):
    s_lo = jnp.sum(jnp.where(lo, x, 0.0), axis=-1, keepdims=True)
    s_hi = jnp.sum(jnp.where(lo, 0.0, x), axis=-1, keepdims=True)
    return jnp.where(lo, s_lo, s_hi) * (1.0 / HEAD_DIM)


def _rope(x, cos, sin_signed):
    nxt = pltpu.roll(x, LANES - 1, 1)
    prv = pltpu.roll(x, 1, 1)
    even = (lax.broadcasted_iota(jnp.int32, x.shape, 1) & 1) == 0
    return x * cos + jnp.where(even, nxt, prv) * sin_signed


def _sigmoid(x):
    return 0.5 * jnp.tanh(0.5 * x) + 0.5


def _silu(x):
    return x * _sigmoid(x)


def _seg_mean_mxu(x, bmat, passes):
    hi = x.astype(BF16)
    out = _dot(hi, bmat)
    if passes == 2:
        out = out + _dot((x - hi.astype(F32)).astype(BF16), bmat)
    return out


def _norm_mod(x, g, sc, sh):
    ms = jnp.mean(x * x, axis=-1, keepdims=True)
    y = x * lax.rsqrt(ms + EPS) * g
    return y * (1.0 + sc) + sh


def _mod_kernel(c_ref, w_ref, b_ref, o_ref):
    s = _silu(c_ref[...])
    o_ref[0] = _dot(s.astype(BF16), w_ref[0].astype(BF16)) + b_ref[0]


def _modulation(cond, w_mod, b_mod):
    tn = 1536
    n = 6 * D_MODEL
    return pl.pallas_call(
        _mod_kernel,
        grid=(DEPTH, n // tn),
        in_specs=[
            pl.BlockSpec((N_MOD_ROWS, D_MODEL), lambda l, j: (0, 0)),
            pl.BlockSpec((1, D_MODEL, tn), lambda l, j: (l, 0, j)),
            pl.BlockSpec((1, 1, tn), lambda l, j: (l, 0, j)),
        ],
        out_specs=pl.BlockSpec((1, N_MOD_ROWS, tn), lambda l, j: (l, 0, j)),
        out_shape=jax.ShapeDtypeStruct((DEPTH, N_MOD_ROWS, n), F32),
        compiler_params=_cparams(("arbitrary", "arbitrary")),
        name="modulation",
    )(cond, w_mod, b_mod.reshape(DEPTH, 1, n))


def _mod_row_fn(tm, seq, latent):
    if latent:
        return lambda i: 1 + (i * tm) // seq
    return lambda i: 0


def _mod_spec(layer, which, mrow):
    return pl.BlockSpec((None, None, 1, 1, D_MODEL), lambda i, *_: (layer, which, mrow(i), 0, 0))


def _layer_spec(layer, shape):
    return pl.BlockSpec((None,) + shape, lambda *_: (layer,) + (0,) * len(shape))


def _proj_epilogues(tn):
    kinds = ((COL_GR, COL_RQ, "sigmoid"), (COL_RQ, COL_RV, "rope"), (COL_RV, COL_RG, None),
             (COL_RG, COL_AQ, "silu"), (COL_AQ, COL_CU, None), (COL_CU, COL_AK, "gelu"), (COL_AK, IN_W, None))
    tiles = [(c, c + MXU_N, kind) for a, e, kind in kinds for c in range(a, e, MXU_N)]
    return [[(a - b * tn, e - b * tn, kind) for a, e, kind in tiles if b * tn <= a < (b + 1) * tn]
            for b in range(IN_W // tn)]


def _proj_kernel(*refs, tn, latent):
    if latent:
        x_ref, g_ref, sc_ref, sh_ref, w_ref, cos_ref, sin_ref, o_ref, h_ref = refs
    else:
        x_ref, g_ref, sc_ref, sh_ref, w_ref, o_ref, h_ref = refs
    j = pl.program_id(1)

    def rope_tile(y):
        if not latent:
            return y
        halves = [_rope(y[:, c:c + LANES], cos_ref[...], sin_ref[...]) for c in range(0, MXU_N, LANES)]
        return jnp.concatenate(halves, axis=1)

    epilogue = {None: lambda y: y, "sigmoid": _sigmoid, "silu": _silu, "rope": rope_tile,
                "gelu": functools.partial(jax.nn.gelu, approximate=True)}

    tm = x_ref.shape[0]
    for b, segs in enumerate(_proj_epilogues(tn)):
        @pl.when(j == b)
        def _(b=b, segs=segs):
            for r0 in range(0, tm, tm // 2) if b == 0 else (None,):
                rows = slice(None) if r0 is None else slice(r0, r0 + tm // 2)
                if b == 0:
                    h_ref[rows, :] = _norm_mod(x_ref[rows, :], g_ref[...], sc_ref[0], sh_ref[0]).astype(BF16)
                for a, e, kind in segs:
                    o_ref[rows, a:e] = epilogue[kind](_dot(h_ref[rows, :], w_ref[:, a:e])).astype(BF16)


def _in_proj(x, g, mod, w, layer, seq, latent, cos=None, sin=None):
    t = x.shape[0]
    tm, tn = 1024, 2304
    mrow = _mod_row_fn(tm, seq, latent)
    return pl.pallas_call(
        functools.partial(_proj_kernel, tn=tn, latent=latent),
        grid=(t // tm, IN_W // tn),
        in_specs=[
            pl.BlockSpec((tm, D_MODEL), lambda i, j: (i, 0)),
            _layer_spec(layer, (1, D_MODEL)),
            _mod_spec(layer, MOD_SC1, mrow), _mod_spec(layer, MOD_SH1, mrow),
            pl.BlockSpec((None, D_MODEL, tn), lambda i, j: (layer, 0, j)),
        ] + ([pl.BlockSpec((tm, LANES), lambda i, j: (i % (seq // tm), 0))] * 2 if latent else []),
        out_specs=pl.BlockSpec((tm, tn), lambda i, j: (i, j)),
        out_shape=jax.ShapeDtypeStruct((t, IN_W), BF16),
        scratch_shapes=[pltpu.VMEM((tm, D_MODEL), BF16)],
        compiler_params=_cparams(("parallel", "arbitrary")),
        name="in_proj",
    )(x, g, mod, mod, w, *((cos, sin) if latent else ()))


def _ret_kernel(*refs, nb, nc, latent, layer):
    if latent:
        (dec_ref, q_ref, k_ref, v_ref, rg_ref, gn_ref, s0_ref,
         o_ref, cst_ref, uu_ref) = refs
    else:
        (dec_ref, q_ref, k_ref, v_ref, rg_ref, gn_ref,
         o_ref, sfin_ref, cst_ref, uu_ref) = refs
    hp = pl.program_id(0)
    sq = (CHUNK, LANES)
    lane = lax.broadcasted_iota(jnp.int32, sq, 1)
    row = lax.broadcasted_iota(jnp.int32, sq, 0)
    lo = lane < HEAD_DIM
    blockdiag = (row < HEAD_DIM) == lo
    bmat = jnp.where(blockdiag, 1.0 / HEAD_DIM, 0.0).astype(BF16)
    @pl.when(pl.program_id(1) == 0)
    def _():
        rowf = row.astype(F32)
        rel = (row - lane).astype(F32)

        def log_decay(direction, head):
            d = jnp.full(sq, dec_ref[layer, direction, head], F32)
            return -(jnp.maximum(d, 0.0) + jnp.log1p(jnp.exp(-jnp.abs(d))))

        ldf0, ldf1 = log_decay(0, 2 * hp), log_decay(0, 2 * hp + 1)
        ldb0, ldb1 = log_decay(1, 2 * hp), log_decay(1, 2 * hp + 1)
        ldf = jnp.where(lo, ldf0, ldf1)
        ldb = jnp.where(lo, ldb0, ldb1)

        def decay_mask(lf, lb):
            fwd = jnp.exp(lf * jnp.maximum(rel, 0.0))
            bwd = jnp.exp(lb * jnp.maximum(-rel, 0.0))
            return jnp.where(rel > 0, fwd, jnp.where(rel < 0, bwd, 2.0))

        cst_ref[0] = decay_mask(ldf0, ldb0)
        cst_ref[1] = decay_mask(ldf1, ldb1)
        cst_ref[2] = jnp.exp(ldf * (CHUNK - 1.0 - rowf))
        cst_ref[3] = jnp.exp(ldb * rowf)
        cst_ref[4] = jnp.exp(ldf * (rowf + 1.0))
        cst_ref[5] = jnp.exp(ldb * (CHUNK - rowf))
        cst_ref[6] = jnp.exp(ldf * float(CHUNK))
        cst_ref[7] = jnp.exp(ldb * float(CHUNK))

    def rows_of(c):
        return pl.ds(pl.multiple_of(c * CHUNK, CHUNK), CHUNK)

    def load_qk(ref, c, scale):
        x = ref[rows_of(c), :].astype(F32)
        return x if scale == 1.0 else x * scale

    nchunks = nb * nc
    group = min(nchunks, RET_GROUP)

    def phase_a(i, carry):
        cs = [i * group + g for g in range(group)]
        ks = [load_qk(k_ref, c, HEAD_DIM ** -0.5) for c in cs]
        kzs = [jnp.concatenate([(k * cst_ref[2]).astype(BF16), (k * cst_ref[3]).astype(BF16)], axis=1)
               for k in ks]
        for c, kz in zip(cs, kzs):
            uu_ref[c] = _dot_tn(kz, v_ref[rows_of(c), :])
        return carry

    lax.fori_loop(0, nchunks // group, phase_a, 0)

    zeros = jnp.zeros((HEAD_DIM, HEAD_DIM), F32)

    def embed(direction):
        top = jnp.concatenate([s0_ref[0, 0, direction, 0], zeros], axis=1)
        bot = jnp.concatenate([zeros, s0_ref[0, 0, direction, 1]], axis=1)
        return jnp.concatenate([top, bot], axis=0)

    for bi in range(nb):
        sf = embed(0) if latent else jnp.zeros(sq, F32)
        sb = embed(1) if latent else jnp.zeros(sq, F32)
        for c in range(bi * nc, (bi + 1) * nc):
            u = uu_ref[c, 0:CHUNK, :]
            uu_ref[c, 0:CHUNK, :] = sf
            sf = cst_ref[6] * sf + jnp.where(blockdiag, u, 0.0)
        for c in reversed(range(bi * nc, (bi + 1) * nc)):
            u = uu_ref[c, CHUNK:2 * CHUNK, :]
            uu_ref[c, CHUNK:2 * CHUNK, :] = sb
            sb = cst_ref[7] * sb + jnp.where(blockdiag, u, 0.0)
        if not latent:
            for d, st in ((0, sf), (1, sb)):
                sfin_ref[bi, d, 0] = st[0:HEAD_DIM, 0:HEAD_DIM]
                sfin_ref[bi, d, 1] = st[HEAD_DIM:, HEAD_DIM:]

    def phase_c(i, carry):
        cs = [i * group + g for g in range(group)]
        n = range(group)
        q = [load_qk(q_ref, c, 1.0) for c in cs]
        k = [load_qk(k_ref, c, HEAD_DIM ** -0.5) for c in cs]
        v = [v_ref[rows_of(c), :] for c in cs]
        qb = [x.astype(BF16) for x in q]
        s0 = [_dot_nt(qb[g], jnp.where(lo, k[g], 0.0).astype(BF16)) for g in n]
        s1 = [_dot_nt(qb[g], jnp.where(lo, 0.0, k[g]).astype(BF16)) for g in n]
        qx = [jnp.concatenate([(x * cst_ref[4]).astype(BF16), (x * cst_ref[5]).astype(BF16)], axis=1) for x in q]
        cross = [_dot(qx[g], uu_ref[cs[g]].astype(BF16)) for g in n]
        i0 = [_dot((s0[g] * cst_ref[0]).astype(BF16), v[g]) for g in n]
        i1 = [_dot((s1[g] * cst_ref[1]).astype(BF16), v[g]) for g in n]
        o = [jnp.where(lo, i0[g], i1[g]) + cross[g] for g in n]
        mu = [_seg_mean_mxu(x, bmat, 2) for x in o]
        d = [o[g] - mu[g] for g in n]
        var = [_seg_mean_mxu(x * x, bmat, 1) for x in d]
        for g in n:
            y = d[g] * lax.rsqrt(var[g] + EPS) * gn_ref[...]
            o_ref[rows_of(cs[g]), :] = (y * rg_ref[rows_of(cs[g]), :].astype(F32)).astype(BF16)
        return carry

    lax.fori_loop(0, nchunks // group, phase_c, 0)


def _retention(z, dec, gn, layer, seq, latent, s0=None):
    t = z.shape[0]
    b = t // seq
    nc = seq // CHUNK
    npair = H_RET // 2
    nb = 1 if latent else RET_CTX_BATCH
    rows = nb * seq

    def zcol(base):
        return pl.BlockSpec((rows, LANES), lambda hp, bi: (bi, base // LANES + hp))

    in_specs = [
        pl.BlockSpec(memory_space=pltpu.SMEM),
        zcol(COL_RQ), zcol(COL_RK), zcol(COL_RV), zcol(COL_RG),
        pl.BlockSpec((None, 1, LANES), lambda hp, bi: (layer, 0, hp)),
    ]
    args = [dec, z, z, z, z, gn]
    out_shape = [jax.ShapeDtypeStruct((t, RET_W), BF16)]
    out_specs = [pl.BlockSpec((rows, LANES), lambda hp, bi: (bi, hp))]
    if latent:
        in_specs.append(pl.BlockSpec((1, 1, 2, 2, HEAD_DIM, HEAD_DIM), lambda hp, bi: (bi, layer, 0, hp, 0, 0)))
        args.append(s0)
    else:
        out_shape.append(jax.ShapeDtypeStruct((b, 2, H_RET, HEAD_DIM, HEAD_DIM), F32))
        out_specs.append(pl.BlockSpec((nb, 2, 2, HEAD_DIM, HEAD_DIM), lambda hp, bi: (bi, 0, hp, 0, 0)))
    res = pl.pallas_call(
        functools.partial(_ret_kernel, nb=nb, nc=nc, latent=latent, layer=layer),
        grid=(npair, b // nb),
        in_specs=in_specs,
        out_specs=out_specs,
        out_shape=out_shape,
        scratch_shapes=[pltpu.VMEM((8, CHUNK, LANES), F32), pltpu.VMEM((nb * nc, 2 * CHUNK, LANES), F32)],
        compiler_params=_cparams(("parallel", "arbitrary")),
        name="retention_latent" if latent else "retention_ctx",
    )(*args)
    return res if not latent else res[0]


def _att_kernel(*refs, tq, seq, latent):
    if latent:
        (q_ref, ak_ref, av_ref, ck_ref, cv_ref, gq_ref, gk_ref, cosq_ref, sinq_ref, cosk_ref, sink_ref,
         o_ref, k0_ref, k1_ref, vv_ref, *sp_refs) = refs
        off = PAST_LEN
    else:
        (q_ref, ak_ref, av_ref, gq_ref, gk_ref,
         o_ref, knew_ref, vnew_ref, k0_ref, k1_ref, vv_ref, *sp_refs) = refs
        off = 0

    @pl.when(pl.program_id(1) == 0)
    def _():
        lo = _lo_mask((seq, LANES))
        ak = ak_ref[...].astype(F32)
        kn = ak * lax.rsqrt(_seg_mean(ak * ak, lo) + EPS) * gk_ref[...]
        av = av_ref[...]
        if latent:
            kn = _rope(kn, cosk_ref[...], sink_ref[...])
            lo_c = _lo_mask((PAST_LEN, LANES))
            ck = ck_ref[0, 0]
            k0_ref[:, 0:off] = jnp.where(lo_c, ck, 0.0).T.astype(BF16)
            k1_ref[:, 0:off] = jnp.where(lo_c, 0.0, ck).T.astype(BF16)
            vv_ref[0:off, :] = cv_ref[0, 0].astype(BF16)
        else:
            knew_ref[...] = kn
            vnew_ref[...] = av.astype(F32)
        k0_ref[:, off:off + seq] = jnp.where(lo, kn, 0.0).T.astype(BF16)
        k1_ref[:, off:off + seq] = jnp.where(lo, 0.0, kn).T.astype(BF16)
        vv_ref[off:off + seq, :] = av

    sk = off + seq
    nkt = sk // MXU_N
    lo = _lo_mask((CHUNK, LANES))
    krefs = (k0_ref, k1_ref)
    units = [(sub, r, g) for sub in range(tq // CHUNK) for r in range(ATT_Q_W // LANES) for g in range(H_KV)]
    qb, m_lane, m_row, l_lane, o_acc, done = {}, {}, {}, {}, {}, {}

    def prep(sub, r):
        rows = slice(sub * CHUNK, (sub + 1) * CHUNK)
        x = q_ref[rows, r * LANES:(r + 1) * LANES].astype(F32)
        xn = x * lax.rsqrt(_seg_mean(x * x, lo) + EPS) * gq_ref[...]
        if latent:
            xn = _rope(xn, cosq_ref[rows, :], sinq_ref[rows, :])
        qb[(sub, r)] = (xn * (HEAD_DIM ** -0.5 * LOG2_E)).astype(BF16)

    def score_tile(i, kt):
        sub, r, g = units[i]
        cols = slice(kt * MXU_N, (kt + 1) * MXU_N)
        s_t = _dot(qb[(sub, r)], krefs[g][:, cols])
        sp_refs[i % ATT_BUFS][:, cols] = s_t
        t = jnp.maximum(s_t[:, :LANES], s_t[:, LANES:])
        m_lane[i] = t if kt == 0 else jnp.maximum(m_lane[i], t)

    def soft_tile(i, kt):
        cols = slice(kt * MXU_N, (kt + 1) * MXU_N)
        sbuf = sp_refs[i % ATT_BUFS]
        p0 = jnp.exp2(sbuf[:, kt * MXU_N:kt * MXU_N + LANES] - m_row[i])
        p1 = jnp.exp2(sbuf[:, kt * MXU_N + LANES:(kt + 1) * MXU_N] - m_row[i])
        pv = _dot(jnp.concatenate([p0, p1], axis=1).astype(BF16), vv_ref[cols, :])
        if kt == 0:
            l_lane[i], o_acc[i] = p0 + p1, pv
        else:
            l_lane[i], o_acc[i] = l_lane[i] + (p0 + p1), o_acc[i] + pv

    def finish(i):
        sub, r, g = units[i]
        done[(sub, r, g)] = o_acc.pop(i) / jnp.sum(l_lane.pop(i), axis=-1, keepdims=True)
        if g == H_KV - 1:
            rows = slice(sub * CHUNK, (sub + 1) * CHUNK)
            o = jnp.where(lo, done.pop((sub, r, 0)), done.pop((sub, r, 1)))
            o_ref[rows, r * LANES:(r + 1) * LANES] = o.astype(BF16)

    prep(units[0][0], units[0][1])
    for s in range(len(units) + ATT_LAG):
        nxt = s + H_KV
        if s % H_KV == 0 and nxt < len(units):
            prep(units[nxt][0], units[nxt][1])
        for kt in range(nkt):
            if s < len(units):
                score_tile(s, kt)
            if s >= ATT_LAG:
                soft_tile(s - ATT_LAG, kt)
        if s < len(units):
            m_row[s] = jnp.broadcast_to(jnp.max(m_lane.pop(s), axis=-1, keepdims=True), (CHUNK, LANES))
        if s >= ATT_LAG:
            finish(s - ATT_LAG)


def _attention(z, gq, gk, layer, seq, latent, cos=None, sin=None, cache_k=None, cache_v=None):
    t = z.shape[0]
    b = t // seq
    tq = min(seq, ATT_TQ)
    nq = seq // tq
    sk = seq + (PAST_LEN if latent else 0)
    kv_spec = lambda base: pl.BlockSpec((seq, LANES), lambda bi, qi: (bi, base // LANES))
    vec_spec = _layer_spec(layer, (1, LANES))
    in_specs = [
        pl.BlockSpec((tq, ATT_Q_W), lambda bi, qi: (bi * nq + qi, COL_AQ // ATT_Q_W)),
        kv_spec(COL_AK), kv_spec(COL_AV),
    ]
    args = [z, z, z]
    if latent:
        cache_spec = pl.BlockSpec((1, 1, PAST_LEN, LANES), lambda bi, qi: (bi, layer, 0, 0))
        in_specs += [cache_spec, cache_spec]
        args += [cache_k, cache_v]
    in_specs += [vec_spec, vec_spec]
    args += [gq, gk]
    out_shape = [jax.ShapeDtypeStruct((t, ATT_Q_W), BF16)]
    out_specs = [pl.BlockSpec((tq, ATT_Q_W), lambda bi, qi: (bi * nq + qi, 0))]
    if latent:
        in_specs += [
            pl.BlockSpec((tq, LANES), lambda bi, qi: (qi, 0)),
            pl.BlockSpec((tq, LANES), lambda bi, qi: (qi, 0)),
            pl.BlockSpec((seq, LANES), lambda bi, qi: (0, 0)),
            pl.BlockSpec((seq, LANES), lambda bi, qi: (0, 0)),
        ]
        args += [cos, sin, cos, sin]
    else:
        new_spec = pl.BlockSpec((seq, LANES), lambda bi, qi: (bi, 0))
        out_shape += [jax.ShapeDtypeStruct((t, LANES), F32)] * 2
        out_specs += [new_spec, new_spec]
    res = pl.pallas_call(
        functools.partial(_att_kernel, tq=tq, seq=seq, latent=latent),
        grid=(b, nq),
        in_specs=in_specs,
        out_specs=out_specs,
        out_shape=out_shape,
        scratch_shapes=([pltpu.VMEM((LANES, sk), BF16)] * 2 + [pltpu.VMEM((sk, LANES), BF16)]
                        + [pltpu.VMEM((CHUNK, sk), F32)] * ATT_BUFS),
        compiler_params=_cparams(("parallel", "arbitrary")),
        name="attention_latent" if latent else "attention_ctx",
    )(*args)
    return res if not latent else res[0]


def _merge_kernel(x_ref, gr_ref, ga_ref, gc_ref, cu_ref, cv_ref, ro_ref, ao_ref,
                  wr_ref, wa_ref, wc_ref, wo_ref, cmn_ref, ws_ref, bs_ref, g1_ref,
                  gn2_ref, sc2_ref, sh2_ref, o_ref, h_ref, *, tm):
    u = cu_ref[...].astype(F32)
    v = cv_ref[...].astype(F32)
    vn = v * lax.rsqrt(jnp.mean(v * v, axis=-1, keepdims=True) + EPS) * cmn_ref[...]
    vb = vn.astype(BF16)
    rows = []
    for n in range(tm // CHUNK):
        cols = []
        for g in range(CM_GROUPS):
            vg = vb[n * CHUNK:(n + 1) * CHUNK, g * LANES:(g + 1) * LANES]
            cols.append(_dot(ws_ref[g], vg) + bs_ref[:, g:g + 1])
        rows.append(jnp.concatenate(cols, axis=1))
    cm = (u * jnp.concatenate(rows, axis=0)).astype(BF16)
    merged = (gr_ref[...].astype(F32) * _dot(ro_ref[...], wr_ref[...])
              + ga_ref[...].astype(F32) * _dot(ao_ref[...], wa_ref[...])
              + gc_ref[...].astype(F32) * _dot(cm, wc_ref[...]))
    mb = merged.astype(BF16)
    for r0 in range(0, tm, tm // 2):
        rows = slice(r0, r0 + tm // 2)
        xn = x_ref[rows, :] + g1_ref[0] * _dot(mb[rows, :], wo_ref[...])
        o_ref[rows, :] = xn
        h_ref[rows, :] = _norm_mod(xn, gn2_ref[...], sc2_ref[0], sh2_ref[0]).astype(BF16)


def _merge(x, z, ret_o, att_o, wr, wa, wc, wo, cmn, ws, bs_t, mod, gn2, layer, seq, latent):
    t = x.shape[0]
    tm = 512
    mrow = _mod_row_fn(tm, seq, latent)
    row_spec = lambda w, cb: pl.BlockSpec((tm, w), lambda i: (i, cb))
    of_layer = lambda shape: _layer_spec(layer, shape)
    return pl.pallas_call(
        functools.partial(_merge_kernel, tm=tm),
        grid=(t // tm,),
        in_specs=[
            row_spec(D_MODEL, 0),
            row_spec(D_MODEL, COL_GR // D_MODEL), row_spec(D_MODEL, COL_GA // D_MODEL),
            row_spec(D_MODEL, COL_GC // D_MODEL),
            row_spec(CM_W, COL_CU // CM_W), row_spec(CM_W, COL_CV // CM_W),
            row_spec(RET_W, 0), row_spec(ATT_Q_W, 0),
            of_layer((RET_W, D_MODEL)), of_layer((ATT_Q_W, D_MODEL)), of_layer((CM_W, D_MODEL)),
            of_layer((D_MODEL, D_MODEL)),
            of_layer((1, CM_W)), of_layer((CM_GROUPS, CHUNK, CHUNK)), of_layer((CHUNK, CM_GROUPS)),
            _mod_spec(layer, MOD_G1, mrow), of_layer((1, D_MODEL)),
            _mod_spec(layer, MOD_SC2, mrow), _mod_spec(layer, MOD_SH2, mrow),
        ],
        out_specs=[row_spec(D_MODEL, 0), row_spec(D_MODEL, 0)],
        out_shape=[jax.ShapeDtypeStruct((t, D_MODEL), F32), jax.ShapeDtypeStruct((t, D_MODEL), BF16)],
        compiler_params=_cparams(("parallel",)),
        name="merge",
    )(x, z, z, z, z, z, ret_o, att_o, wr, wa, wc, wo, cmn, ws, bs_t, mod, gn2, mod, mod)


def _ffn_kernel(x_ref, h_ref, w1_ref, w2_ref, g2_ref, gf_ref, o_ref, acc_ref, *, final):
    k = pl.program_id(1)
    a = jnp.square(jnp.maximum(_dot(h_ref[...], w1_ref[...]), 0.0)).astype(BF16)

    @pl.when(k == 0)
    def _():
        acc_ref[...] = _dot(a, w2_ref[...])

    last = pl.num_programs(1) - 1

    @pl.when(jnp.logical_and(k > 0, k < last))
    def _():
        acc_ref[...] += _dot(a, w2_ref[...])

    @pl.when(k == last)
    def _():
        tm = a.shape[0]
        for r0 in range(0, tm, tm // 2):
            rows = slice(r0, r0 + tm // 2)
            xn = x_ref[rows, :] + g2_ref[0] * (acc_ref[rows, :] + _dot(a[rows, :], w2_ref[...]))
            if final:
                xn = xn * lax.rsqrt(jnp.mean(xn * xn, axis=-1, keepdims=True) + EPS) * gf_ref[...]
            o_ref[rows, :] = xn


def _ffn(x, h, w1, w2, mod, gfin, layer, seq, latent, final):
    t = x.shape[0]
    tm, tf = 1024, 1024
    mrow = _mod_row_fn(tm, seq, latent)
    return pl.pallas_call(
        functools.partial(_ffn_kernel, final=final),
        grid=(t // tm, D_FF // tf),
        in_specs=[
            pl.BlockSpec((tm, D_MODEL), lambda i, k: (i, 0)),
            pl.BlockSpec((tm, D_MODEL), lambda i, k: (i, 0)),
            pl.BlockSpec((None, D_MODEL, tf), lambda i, k: (layer, 0, k)),
            pl.BlockSpec((None, tf, D_MODEL), lambda i, k: (layer, k, 0)),
            _mod_spec(layer, MOD_G2, mrow),
            pl.BlockSpec((1, D_MODEL), lambda i, k: (0, 0)),
        ],
        out_specs=pl.BlockSpec((tm, D_MODEL), lambda i, k: (i, 0)),
        out_shape=jax.ShapeDtypeStruct((t, D_MODEL), F32),
        scratch_shapes=[pltpu.VMEM((tm, D_MODEL), F32)],
        compiler_params=_cparams(("parallel", "arbitrary")),
        name="ffn_final" if final else "ffn",
    )(x, h, w1, w2, mod, gfin)


def _rope_tables(length):
    rows = length // GRID_W
    row = jnp.repeat(jnp.arange(rows, dtype=F32), GRID_W)
    col = jnp.tile(jnp.arange(GRID_W, dtype=F32), rows)
    half = HEAD_DIM // 2
    inv = 1.0 / (ROPE_BASE ** (jnp.arange(0, half, 2, dtype=F32) / half))
    ang = jnp.concatenate([row[:, None] * inv[None, :], col[:, None] * inv[None, :]], axis=-1)
    cos, sin = jnp.cos(ang), jnp.sin(ang)
    cos2 = jnp.repeat(cos, 2, axis=-1)
    sin2 = jnp.stack([-sin, sin], axis=-1).reshape(length, HEAD_DIM)
    return jnp.tile(cos2, (1, 2)), jnp.tile(sin2, (1, 2))


def _permute_w_in(w_in):
    o = np.cumsum((0, 512, 512, 512, 512, 512, 128, 128, 512, 512, 1024, 1024, 1024))
    seg = lambda i: w_in[:, :, int(o[i]):int(o[i + 1])]
    rq, rk, rv, rg, aq, ak, av, cu, cv, gr, ga, gc = (seg(i) for i in range(12))
    aq = aq.reshape(DEPTH, D_MODEL, H_Q, HEAD_DIM)[:, :, np.array(Q_HEAD_ORDER)].reshape(DEPTH, D_MODEL, ATT_Q_W)
    return jnp.concatenate([gr, ga, gc, rq, rk, rv, rg, aq, cu, cv, ak, av], axis=-1).astype(BF16)


def kernel(x_prompt, x_sample, cache_k, cache_v, state_ret, c, c_ctx, w_mod, b_mod, g_norm1, g_norm2, w_in, g_q, g_k, ret_decay, ret_gn, cm_norm, cm_ws, cm_bs, w_br_ret, w_br_att, w_br_cm, w_out, w_ff1, w_ff2, g_final):
    batch, dec_batch = x_prompt.shape[0], x_sample.shape[0]

    cond = jnp.concatenate(
        [c_ctx[None, :], c, jnp.zeros((N_MOD_ROWS - 1 - dec_batch, D_MODEL), F32)], axis=0)
    mod = _modulation(cond, w_mod, b_mod)
    mod = mod.reshape(DEPTH, N_MOD_ROWS, 6, D_MODEL).transpose(0, 2, 1, 3).reshape(DEPTH, 6, N_MOD_ROWS, 1, D_MODEL)

    w_in_p = _permute_w_in(w_in)
    w_br_ret_b = w_br_ret.astype(BF16)
    w_br_att_b = w_br_att.reshape(DEPTH, H_Q, HEAD_DIM, D_MODEL)[:, np.array(Q_HEAD_ORDER)].reshape(
        DEPTH, ATT_Q_W, D_MODEL).astype(BF16)
    w_br_cm_b = w_br_cm.astype(BF16)
    w_out_b = w_out.astype(BF16)
    w_ff1_b = w_ff1.astype(BF16)
    w_ff2_b = w_ff2.astype(BF16)
    cm_ws_b = cm_ws.astype(BF16)
    cm_bs_t = jnp.swapaxes(cm_bs, 1, 2)
    g_q2 = jnp.tile(g_q, (1, 2)).reshape(DEPTH, 1, LANES)
    g_k2 = jnp.tile(g_k, (1, 2)).reshape(DEPTH, 1, LANES)
    cos, sin = _rope_tables(DEC_SEQ)
    cache_k2 = cache_k.reshape(dec_batch, DEPTH, PAST_LEN, LANES)
    cache_v2 = cache_v.reshape(dec_batch, DEPTH, PAST_LEN, LANES)
    g_fin = g_final.reshape(1, D_MODEL)
    gn1 = g_norm1.reshape(DEPTH, 1, D_MODEL)
    gn2 = g_norm2.reshape(DEPTH, 1, D_MODEL)
    ret_gn2 = ret_gn.reshape(DEPTH, 1, RET_W)
    cm_norm2 = cm_norm.reshape(DEPTH, 1, CM_W)

    xp = x_prompt.reshape(batch * SEQ, D_MODEL)
    xs = x_sample.reshape(dec_batch * DEC_SEQ, D_MODEL)
    ks_list, vs_list, ss_list = [], [], []
    for l in range(DEPTH):
        final = l == DEPTH - 1
        for latent in (False, True):
            x = xs if latent else xp
            seq = DEC_SEQ if latent else SEQ
            z = _in_proj(x, gn1, mod, w_in_p, l, seq, latent, cos, sin)
            if latent:
                ret_o = _retention(z, ret_decay, ret_gn2, l, seq, True, state_ret)
                att_o = _attention(z, g_q2, g_k2, l, seq, True, cos, sin, cache_k2, cache_v2)
            else:
                ret_o, s_fin = _retention(z, ret_decay, ret_gn2, l, seq, False)
                att_o, k_new, v_new = _attention(z, g_q2, g_k2, l, seq, False)
                ks_list.append(k_new.reshape(batch, SEQ, H_KV, HEAD_DIM))
                vs_list.append(v_new.reshape(batch, SEQ, H_KV, HEAD_DIM))
                ss_list.append(s_fin)
            x, h2 = _merge(x, z, ret_o, att_o, w_br_ret_b, w_br_att_b, w_br_cm_b, w_out_b,
                           cm_norm2, cm_ws_b, cm_bs_t, mod, gn2, l, seq, latent)
            x = _ffn(x, h2, w_ff1_b, w_ff2_b, mod, g_fin, l, seq, latent, final)
            if latent:
                xs = x
            else:
                xp = x
    y_prompt = xp.reshape(batch, SEQ, D_MODEL)
    y_sample = xs.reshape(dec_batch, DEC_SEQ, D_MODEL)
    return (y_prompt, y_sample, jnp.stack(ks_list, axis=1), jnp.stack(vs_list, axis=1),
            jnp.stack(ss_list, axis=1))
```

```python
import functools

import numpy as np
import jax
import jax.numpy as jnp
from jax import lax
from jax.experimental import pallas as pl
from jax.experimental.pallas import tpu as pltpu

D_MODEL = 1024
DEPTH = 2
SEQ = 256
DEC_SEQ = 2048
PAST_LEN = 256
GRID_W = 64
HEAD_DIM = 64
H_RET = 8
H_Q = 8
H_KV = 2
CHUNK = 128
CM_GROUPS = 4
RET_W = 512
ATT_Q_W = 512
ATT_KV_W = 128
CM_W = 512
D_FF = 4096
ROPE_BASE = 10000.0
EPS = 1e-6
LOG2_E = 1.4426950408889634
IN_W = 6912
N_MOD_ROWS = 16
MOD_SH1, MOD_SC1, MOD_G1, MOD_SH2, MOD_SC2, MOD_G2 = range(6)

LANES = 128
MXU_N = 256
RET_GROUP = 16
RET_CTX_BATCH = 8
ATT_TQ = 512
ATT_LAG = 2
ATT_BUFS = ATT_LAG + 1
VMEM_LIMIT = 48 * 1024 * 1024

F32 = jnp.float32
BF16 = jnp.bfloat16

COL_GR, COL_GA, COL_GC = 0, 1024, 2048
COL_RQ, COL_RK, COL_RV, COL_RG = 3072, 3584, 4096, 4608
COL_AQ, COL_CU, COL_CV = 5120, 5632, 6144
COL_AK, COL_AV = 6656, 6784
Q_HEAD_ORDER = (0, 4, 1, 5, 2, 6, 3, 7)


def _cparams(sem):
    return pltpu.CompilerParams(dimension_semantics=sem, vmem_limit_bytes=VMEM_LIMIT)


def _dot(a, b):
    return jnp.dot(a, b, preferred_element_type=F32)


def _dot_nt(a, b):
    return lax.dot_general(a, b, (((1,), (1,)), ((), ())), preferred_element_type=F32)


def _dot_tn(a, b):
    return lax.dot_general(a, b, (((0,), (0,)), ((), ())), preferred_element_type=F32)


def _lo_mask(shape):
    return lax.broadcasted_iota(jnp.int32, shape, len(shape) - 1) < HEAD_DIM


def _seg_mean(x, lo):
    s_lo = jnp.sum(jnp.where(lo, x, 0.0), axis=-1, keepdims=True)
    s_hi = jnp.sum(jnp.where(lo, 0.0, x), axis=-1, keepdims=True)
    return jnp.where(lo, s_lo, s_hi) * (1.0 / HEAD_DIM)


def _rope(x, cos, sin_signed):
    nxt = pltpu.roll(x, LANES - 1, 1)
    prv = pltpu.roll(x, 1, 1)
    even = (lax.broadcasted_iota(jnp.int32, x.shape, 1) & 1) == 0
    return x * cos + jnp.where(even, nxt, prv) * sin_signed


def _sigmoid(x):
    return 0.5 * jnp.tanh(0.5 * x) + 0.5


def _silu(x):
    return x * _sigmoid(x)


def _seg_mean_mxu(x, bmat, passes):
    hi = x.astype(BF16)
    out = _dot(hi, bmat)
    if passes == 2:
        out = out + _dot((x - hi.astype(F32)).astype(BF16), bmat)
    return out


def _norm_mod(x, g, sc, sh):
    ms = jnp.mean(x * x, axis=-1, keepdims=True)
    y = x * lax.rsqrt(ms + EPS) * g
    return y * (1.0 + sc) + sh


def _mod_kernel(c_ref, w_ref, b_ref, o_ref):
    s = _silu(c_ref[...])
    o_ref[0] = _dot(s.astype(BF16), w_ref[0].astype(BF16)) + b_ref[0]


def _modulation(cond, w_mod, b_mod):
    tn = 1536
    n = 6 * D_MODEL
    return pl.pallas_call(
        _mod_kernel,
        grid=(DEPTH, n // tn),
        in_specs=[
            pl.BlockSpec((N_MOD_ROWS, D_MODEL), lambda l, j: (0, 0)),
            pl.BlockSpec((1, D_MODEL, tn), lambda l, j: (l, 0, j)),
            pl.BlockSpec((1, 1, tn), lambda l, j: (l, 0, j)),
        ],
        out_specs=pl.BlockSpec((1, N_MOD_ROWS, tn), lambda l, j: (l, 0, j)),
        out_shape=jax.ShapeDtypeStruct((DEPTH, N_MOD_ROWS, n), F32),
        compiler_params=_cparams(("arbitrary", "arbitrary")),
        name="modulation",
    )(cond, w_mod, b_mod.reshape(DEPTH, 1, n))


def _mod_row_fn(tm, seq, latent):
    if latent:
        return lambda i: 1 + (i * tm) // seq
    return lambda i: 0


def _mod_spec(layer, which, mrow):
    return pl.BlockSpec((None, None, 1, 1, D_MODEL), lambda i, *_: (layer, which, mrow(i), 0, 0))


def _layer_spec(layer, shape):
    return pl.BlockSpec((None,) + shape, lambda *_: (layer,) + (0,) * len(shape))


def _proj_epilogues(tn):
    kinds = ((COL_GR, COL_RQ, "sigmoid"), (COL_RQ, COL_RV, "rope"), (COL_RV, COL_RG, None),
             (COL_RG, COL_AQ, "silu"), (COL_AQ, COL_CU, None), (COL_CU, COL_AK, "gelu"), (COL_AK, IN_W, None))
    tiles = [(c, c + MXU_N, kind) for a, e, kind in kinds for c in range(a, e, MXU_N)]
    return [[(a - b * tn, e - b * tn, kind) for a, e, kind in tiles if b * tn <= a < (b + 1) * tn]
            for b in range(IN_W // tn)]


def _proj_kernel(*refs, tn, latent):
    if latent:
        x_ref, g_ref, sc_ref, sh_ref, w_ref, cos_ref, sin_ref, o_ref, h_ref = refs
    else:
        x_ref, g_ref, sc_ref, sh_ref, w_ref, o_ref, h_ref = refs
    j = pl.program_id(1)

    def rope_tile(y):
        if not latent:
            return y
        halves = [_rope(y[:, c:c + LANES], cos_ref[...], sin_ref[...]) for c in range(0, MXU_N, LANES)]
        return jnp.concatenate(halves, axis=1)

    epilogue = {None: lambda y: y, "sigmoid": _sigmoid, "silu": _silu, "rope": rope_tile,
                "gelu": functools.partial(jax.nn.gelu, approximate=True)}

    tm = x_ref.shape[0]
    for b, segs in enumerate(_proj_epilogues(tn)):
        @pl.when(j == b)
        def _(b=b, segs=segs):
            for r0 in range(0, tm, tm // 2) if b == 0 else (None,):
                rows = slice(None) if r0 is None else slice(r0, r0 + tm // 2)
                if b == 0:
                    h_ref[rows, :] = _norm_mod(x_ref[rows, :], g_ref[...], sc_ref[0], sh_ref[0]).astype(BF16)
                for a, e, kind in segs:
                    o_ref[rows, a:e] = epilogue[kind](_dot(h_ref[rows, :], w_ref[:, a:e])).astype(BF16)


def _in_proj(x, g, mod, w, layer, seq, latent, cos=None, sin=None):
    t = x.shape[0]
    tm, tn = 1024, 2304
    mrow = _mod_row_fn(tm, seq, latent)
    return pl.pallas_call(
        functools.partial(_proj_kernel, tn=tn, latent=latent),
        grid=(t // tm, IN_W // tn),
        in_specs=[
            pl.BlockSpec((tm, D_MODEL), lambda i, j: (i, 0)),
            _layer_spec(layer, (1, D_MODEL)),
            _mod_spec(layer, MOD_SC1, mrow), _mod_spec(layer, MOD_SH1, mrow),
            pl.BlockSpec((None, D_MODEL, tn), lambda i, j: (layer, 0, j)),
        ] + ([pl.BlockSpec((tm, LANES), lambda i, j: (i % (seq // tm), 0))] * 2 if latent else []),
        out_specs=pl.BlockSpec((tm, tn), lambda i, j: (i, j)),
        out_shape=jax.ShapeDtypeStruct((t, IN_W), BF16),
        scratch_shapes=[pltpu.VMEM((tm, D_MODEL), BF16)],
        compiler_params=_cparams(("parallel", "arbitrary")),
        name="in_proj",
    )(x, g, mod, mod, w, *((cos, sin) if latent else ()))


def _ret_kernel(*refs, nb, nc, latent, layer):
    if latent:
        (dec_ref, q_ref, k_ref, v_ref, rg_ref, gn_ref, s0_ref,
         o_ref, cst_ref, uu_ref) = refs
    else:
        (dec_ref, q_ref, k_ref, v_ref, rg_ref, gn_ref,
         o_ref, sfin_ref, cst_ref, uu_ref) = refs
    hp = pl.program_id(0)
    sq = (CHUNK, LANES)
    lane = lax.broadcasted_iota(jnp.int32, sq, 1)
    row = lax.broadcasted_iota(jnp.int32, sq, 0)
    lo = lane < HEAD_DIM
    blockdiag = (row < HEAD_DIM) == lo
    bmat = jnp.where(blockdiag, 1.0 / HEAD_DIM, 0.0).astype(BF16)
    @pl.when(pl.program_id(1) == 0)
    def _():
        rowf = row.astype(F32)
        rel = (row - lane).astype(F32)

        def log_decay(direction, head):
            d = jnp.full(sq, dec_ref[layer, direction, head], F32)
            return -(jnp.maximum(d, 0.0) + jnp.log1p(jnp.exp(-jnp.abs(d))))

        ldf0, ldf1 = log_decay(0, 2 * hp), log_decay(0, 2 * hp + 1)
        ldb0, ldb1 = log_decay(1, 2 * hp), log_decay(1, 2 * hp + 1)
        ldf = jnp.where(lo, ldf0, ldf1)
        ldb = jnp.where(lo, ldb0, ldb1)

        def decay_mask(lf, lb):
            fwd = jnp.exp(lf * jnp.maximum(rel, 0.0))
            bwd = jnp.exp(lb * jnp.maximum(-rel, 0.0))
            return jnp.where(rel > 0, fwd, jnp.where(rel < 0, bwd, 2.0))

        cst_ref[0] = decay_mask(ldf0, ldb0)
        cst_ref[1] = decay_mask(ldf1, ldb1)
        cst_ref[2] = jnp.exp(ldf * (CHUNK - 1.0 - rowf))
        cst_ref[3] = jnp.exp(ldb * rowf)
        cst_ref[4] = jnp.exp(ldf * (rowf + 1.0))
        cst_ref[5] = jnp.exp(ldb * (CHUNK - rowf))
        cst_ref[6] = jnp.exp(ldf * float(CHUNK))
        cst_ref[7] = jnp.exp(ldb * float(CHUNK))

    def rows_of(c):
        return pl.ds(pl.multiple_of(c * CHUNK, CHUNK), CHUNK)

    def load_qk(ref, c, scale):
        x = ref[rows_of(c), :].astype(F32)
        return x if scale == 1.0 else x * scale

    nchunks = nb * nc
    group = min(nchunks, RET_GROUP)

    def phase_a(i, carry):
        cs = [i * group + g for g in range(group)]
        ks = [load_qk(k_ref, c, HEAD_DIM ** -0.5) for c in cs]
        kzs = [jnp.concatenate([(k * cst_ref[2]).astype(BF16), (k * cst_ref[3]).astype(BF16)], axis=1)
               for k in ks]
        for c, kz in zip(cs, kzs):
            uu_ref[c] = _dot_tn(kz, v_ref[rows_of(c), :])
        return carry

    lax.fori_loop(0, nchunks // group, phase_a, 0)

    zeros = jnp.zeros((HEAD_DIM, HEAD_DIM), F32)

    def embed(direction):
        top = jnp.concatenate([s0_ref[0, 0, direction, 0], zeros], axis=1)
        bot = jnp.concatenate([zeros, s0_ref[0, 0, direction, 1]], axis=1)
        return jnp.concatenate([top, bot], axis=0)

    for bi in range(nb):
        sf = embed(0) if latent else jnp.zeros(sq, F32)
        sb = embed(1) if latent else jnp.zeros(sq, F32)
        for c in range(bi * nc, (bi + 1) * nc):
            u = uu_ref[c, 0:CHUNK, :]
            uu_ref[c, 0:CHUNK, :] = sf
            sf = cst_ref[6] * sf + jnp.where(blockdiag, u, 0.0)
        for c in reversed(range(bi * nc, (bi + 1) * nc)):
            u = uu_ref[c, CHUNK:2 * CHUNK, :]
            uu_ref[c, CHUNK:2 * CHUNK, :] = sb
            sb = cst_ref[7] * sb + jnp.where(blockdiag, u, 0.0)
        if not latent:
            for d, st in ((0, sf), (1, sb)):
                sfin_ref[bi, d, 0] = st[0:HEAD_DIM, 0:HEAD_DIM]
                sfin_ref[bi, d, 1] = st[HEAD_DIM:, HEAD_DIM:]

    def phase_c(i, carry):
        cs = [i * group + g for g in range(group)]
        n = range(group)
        q = [load_qk(q_ref, c, 1.0) for c in cs]
        k = [load_qk(k_ref, c, HEAD_DIM ** -0.5) for c in cs]
        v = [v_ref[rows_of(c), :] for c in cs]
        qb = [x.astype(BF16) for x in q]
        s0 = [_dot_nt(qb[g], jnp.where(lo, k[g], 0.0).astype(BF16)) for g in n]
        s1 = [_dot_nt(qb[g], jnp.where(lo, 0.0, k[g]).astype(BF16)) for g in n]
        qx = [jnp.concatenate([(x * cst_ref[4]).astype(BF16), (x * cst_ref[5]).astype(BF16)], axis=1) for x in q]
        cross = [_dot(qx[g], uu_ref[cs[g]].astype(BF16)) for g in n]
        i0 = [_dot((s0[g] * cst_ref[0]).astype(BF16), v[g]) for g in n]
        i1 = [_dot((s1[g] * cst_ref[1]).astype(BF16), v[g]) for g in n]
        o = [jnp.where(lo, i0[g], i1[g]) + cross[g] for g in n]
        mu = [_seg_mean_mxu(x, bmat, 2) for x in o]
        d = [o[g] - mu[g] for g in n]
        var = [_seg_mean_mxu(x * x, bmat, 1) for x in d]
        for g in n:
            y = d[g] * lax.rsqrt(var[g] + EPS) * gn_ref[...]
            o_ref[rows_of(cs[g]), :] = (y * rg_ref[rows_of(cs[g]), :].astype(F32)).astype(BF16)
        return carry

    lax.fori_loop(0, nchunks // group, phase_c, 0)


def _retention(z, dec, gn, layer, seq, latent, s0=None):
    t = z.shape[0]
    b = t // seq
    nc = seq // CHUNK
    npair = H_RET // 2
    nb = 1 if latent else RET_CTX_BATCH
    rows = nb * seq

    def zcol(base):
        return pl.BlockSpec((rows, LANES), lambda hp, bi: (bi, base // LANES + hp))

    in_specs = [
        pl.BlockSpec(memory_space=pltpu.SMEM),
        zcol(COL_RQ), zcol(COL_RK), zcol(COL_RV), zcol(COL_RG),
        pl.BlockSpec((None, 1, LANES), lambda hp, bi: (layer, 0, hp)),
    ]
    args = [dec, z, z, z, z, gn]
    out_shape = [jax.ShapeDtypeStruct((t, RET_W), BF16)]
    out_specs = [pl.BlockSpec((rows, LANES), lambda hp, bi: (bi, hp))]
    if latent:
        in_specs.append(pl.BlockSpec((1, 1, 2, 2, HEAD_DIM, HEAD_DIM), lambda hp, bi: (bi, layer, 0, hp, 0, 0)))
        args.append(s0)
    else:
        out_shape.append(jax.ShapeDtypeStruct((b, 2, H_RET, HEAD_DIM, HEAD_DIM), F32))
        out_specs.append(pl.BlockSpec((nb, 2, 2, HEAD_DIM, HEAD_DIM), lambda hp, bi: (bi, 0, hp, 0, 0)))
    res = pl.pallas_call(
        functools.partial(_ret_kernel, nb=nb, nc=nc, latent=latent, layer=layer),
        grid=(npair, b // nb),
        in_specs=in_specs,
        out_specs=out_specs,
        out_shape=out_shape,
        scratch_shapes=[pltpu.VMEM((8, CHUNK, LANES), F32), pltpu.VMEM((nb * nc, 2 * CHUNK, LANES), F32)],
        compiler_params=_cparams(("parallel", "arbitrary")),
        name="retention_latent" if latent else "retention_ctx",
    )(*args)
    return res if not latent else res[0]


def _att_kernel(*refs, tq, seq, latent):
    if latent:
        (q_ref, ak_ref, av_ref, ck_ref, cv_ref, gq_ref, gk_ref, cosq_ref, sinq_ref, cosk_ref, sink_ref,
         o_ref, k0_ref, k1_ref, vv_ref, *sp_refs) = refs
        off = PAST_LEN
    else:
        (q_ref, ak_ref, av_ref, gq_ref, gk_ref,
         o_ref, knew_ref, vnew_ref, k0_ref, k1_ref, vv_ref, *sp_refs) = refs
        off = 0

    @pl.when(pl.program_id(1) == 0)
    def _():
        lo = _lo_mask((seq, LANES))
        ak = ak_ref[...].astype(F32)
        kn = ak * lax.rsqrt(_seg_mean(ak * ak, lo) + EPS) * gk_ref[...]
        av = av_ref[...]
        if latent:
            kn = _rope(kn, cosk_ref[...], sink_ref[...])
            lo_c = _lo_mask((PAST_LEN, LANES))
            ck = ck_ref[0, 0]
            k0_ref[:, 0:off] = jnp.where(lo_c, ck, 0.0).T.astype(BF16)
            k1_ref[:, 0:off] = jnp.where(lo_c, 0.0, ck).T.astype(BF16)
            vv_ref[0:off, :] = cv_ref[0, 0].astype(BF16)
        else:
            knew_ref[...] = kn
            vnew_ref[...] = av.astype(F32)
        k0_ref[:, off:off + seq] = jnp.where(lo, kn, 0.0).T.astype(BF16)
        k1_ref[:, off:off + seq] = jnp.where(lo, 0.0, kn).T.astype(BF16)
        vv_ref[off:off + seq, :] = av

    sk = off + seq
    nkt = sk // MXU_N
    lo = _lo_mask((CHUNK, LANES))
    krefs = (k0_ref, k1_ref)
    units = [(sub, r, g) for sub in range(tq // CHUNK) for r in range(ATT_Q_W // LANES) for g in range(H_KV)]
    qb, m_lane, m_row, l_lane, o_acc, done = {}, {}, {}, {}, {}, {}

    def prep(sub, r):
        rows = slice(sub * CHUNK, (sub + 1) * CHUNK)
        x = q_ref[rows, r * LANES:(r + 1) * LANES].astype(F32)
        xn = x * lax.rsqrt(_seg_mean(x * x, lo) + EPS) * gq_ref[...]
        if latent:
            xn = _rope(xn, cosq_ref[rows, :], sinq_ref[rows, :])
        qb[(sub, r)] = (xn * (HEAD_DIM ** -0.5 * LOG2_E)).astype(BF16)

    def score_tile(i, kt):
        sub, r, g = units[i]
        cols = slice(kt * MXU_N, (kt + 1) * MXU_N)
        s_t = _dot(qb[(sub, r)], krefs[g][:, cols])
        sp_refs[i % ATT_BUFS][:, cols] = s_t
        t = jnp.maximum(s_t[:, :LANES], s_t[:, LANES:])
        m_lane[i] = t if kt == 0 else jnp.maximum(m_lane[i], t)

    def soft_tile(i, kt):
        cols = slice(kt * MXU_N, (kt + 1) * MXU_N)
        sbuf = sp_refs[i % ATT_BUFS]
        p0 = jnp.exp2(sbuf[:, kt * MXU_N:kt * MXU_N + LANES] - m_row[i])
        p1 = jnp.exp2(sbuf[:, kt * MXU_N + LANES:(kt + 1) * MXU_N] - m_row[i])
        pv = _dot(jnp.concatenate([p0, p1], axis=1).astype(BF16), vv_ref[cols, :])
        if kt == 0:
            l_lane[i], o_acc[i] = p0 + p1, pv
        else:
            l_lane[i], o_acc[i] = l_lane[i] + (p0 + p1), o_acc[i] + pv

    def finish(i):
        sub, r, g = units[i]
        done[(sub, r, g)] = o_acc.pop(i) / jnp.sum(l_lane.pop(i), axis=-1, keepdims=True)
        if g == H_KV - 1:
            rows = slice(sub * CHUNK, (sub + 1) * CHUNK)
            o = jnp.where(lo, done.pop((sub, r, 0)), done.pop((sub, r, 1)))
            o_ref[rows, r * LANES:(r + 1) * LANES] = o.astype(BF16)

    prep(units[0][0], units[0][1])
    for s in range(len(units) + ATT_LAG):
        nxt = s + H_KV
        if s % H_KV == 0 and nxt < len(units):
            prep(units[nxt][0], units[nxt][1])
        for kt in range(nkt):
            if s < len(units):
                score_tile(s, kt)
            if s >= ATT_LAG:
                soft_tile(s - ATT_LAG, kt)
        if s < len(units):
            m_row[s] = jnp.broadcast_to(jnp.max(m_lane.pop(s), axis=-1, keepdims=True), (CHUNK, LANES))
        if s >= ATT_LAG:
            finish(s - ATT_LAG)


def _attention(z, gq, gk, layer, seq, latent, cos=None, sin=None, cache_k=None, cache_v=None):
    t = z.shape[0]
    b = t // seq
    tq = min(seq, ATT_TQ)
    nq = seq // tq
    sk = seq + (PAST_LEN if latent else 0)
    kv_spec = lambda base: pl.BlockSpec((seq, LANES), lambda bi, qi: (bi, base // LANES))
    vec_spec = _layer_spec(layer, (1, LANES))
    in_specs = [
        pl.BlockSpec((tq, ATT_Q_W), lambda bi, qi: (bi * nq + qi, COL_AQ // ATT_Q_W)),
        kv_spec(COL_AK), kv_spec(COL_AV),
    ]
    args = [z, z, z]
    if latent:
        cache_spec = pl.BlockSpec((1, 1, PAST_LEN, LANES), lambda bi, qi: (bi, layer, 0, 0))
        in_specs += [cache_spec, cache_spec]
        args += [cache_k, cache_v]
    in_specs += [vec_spec, vec_spec]
    args += [gq, gk]
    out_shape = [jax.ShapeDtypeStruct((t, ATT_Q_W), BF16)]
    out_specs = [pl.BlockSpec((tq, ATT_Q_W), lambda bi, qi: (bi * nq + qi, 0))]
    if latent:
        in_specs += [
            pl.BlockSpec((tq, LANES), lambda bi, qi: (qi, 0)),
            pl.BlockSpec((tq, LANES), lambda bi, qi: (qi, 0)),
            pl.BlockSpec((seq, LANES), lambda bi, qi: (0, 0)),
            pl.BlockSpec((seq, LANES), lambda bi, qi: (0, 0)),
        ]
        args += [cos, sin, cos, sin]
    else:
        new_spec = pl.BlockSpec((seq, LANES), lambda bi, qi: (bi, 0))
        out_shape += [jax.ShapeDtypeStruct((t, LANES), F32)] * 2
        out_specs += [new_spec, new_spec]
    res = pl.pallas_call(
        functools.partial(_att_kernel, tq=tq, seq=seq, latent=latent),
        grid=(b, nq),
        in_specs=in_specs,
        out_specs=out_specs,
        out_shape=out_shape,
        scratch_shapes=([pltpu.VMEM((LANES, sk), BF16)] * 2 + [pltpu.VMEM((sk, LANES), BF16)]
                        + [pltpu.VMEM((CHUNK, sk), F32)] * ATT_BUFS),
        compiler_params=_cparams(("parallel", "arbitrary")),
        name="attention_latent" if latent else "attention_ctx",
    )(*args)
    return res if not latent else res[0]


def _merge_kernel(x_ref, gr_ref, ga_ref, gc_ref, cu_ref, cv_ref, ro_ref, ao_ref,
                  wr_ref, wa_ref, wc_ref, wo_ref, cmn_ref, ws_ref, bs_ref, g1_ref, o_ref, *, tm):
    u = cu_ref[...].astype(F32)
    v = cv_ref[...].astype(F32)
    vn = v * lax.rsqrt(jnp.mean(v * v, axis=-1, keepdims=True) + EPS) * cmn_ref[...]
    vb = vn.astype(BF16)
    rows = []
    for n in range(tm // CHUNK):
        cols = []
        for g in range(CM_GROUPS):
            vg = vb[n * CHUNK:(n + 1) * CHUNK, g * LANES:(g + 1) * LANES]
            cols.append(_dot(ws_ref[g], vg) + bs_ref[:, g:g + 1])
        rows.append(jnp.concatenate(cols, axis=1))
    cm = (u * jnp.concatenate(rows, axis=0)).astype(BF16)
    merged = (gr_ref[...].astype(F32) * _dot(ro_ref[...], wr_ref[...])
              + ga_ref[...].astype(F32) * _dot(ao_ref[...], wa_ref[...])
              + gc_ref[...].astype(F32) * _dot(cm, wc_ref[...]))
    mb = merged.astype(BF16)
    for r0 in range(0, tm, tm // 2):
        rows = slice(r0, r0 + tm // 2)
        o_ref[rows, :] = x_ref[rows, :] + g1_ref[0] * _dot(mb[rows, :], wo_ref[...])


def _merge(x, z, ret_o, att_o, wr, wa, wc, wo, cmn, ws, bs_t, mod, layer, seq, latent):
    t = x.shape[0]
    tm = 512
    mrow = _mod_row_fn(tm, seq, latent)
    row_spec = lambda w, cb: pl.BlockSpec((tm, w), lambda i: (i, cb))
    of_layer = lambda shape: _layer_spec(layer, shape)
    return pl.pallas_call(
        functools.partial(_merge_kernel, tm=tm),
        grid=(t // tm,),
        in_specs=[
            row_spec(D_MODEL, 0),
            row_spec(D_MODEL, COL_GR // D_MODEL), row_spec(D_MODEL, COL_GA // D_MODEL),
            row_spec(D_MODEL, COL_GC // D_MODEL),
            row_spec(CM_W, COL_CU // CM_W), row_spec(CM_W, COL_CV // CM_W),
            row_spec(RET_W, 0), row_spec(ATT_Q_W, 0),
            of_layer((RET_W, D_MODEL)), of_layer((ATT_Q_W, D_MODEL)), of_layer((CM_W, D_MODEL)),
            of_layer((D_MODEL, D_MODEL)),
            of_layer((1, CM_W)), of_layer((CM_GROUPS, CHUNK, CHUNK)), of_layer((CHUNK, CM_GROUPS)),
            _mod_spec(layer, MOD_G1, mrow),
        ],
        out_specs=row_spec(D_MODEL, 0),
        out_shape=jax.ShapeDtypeStruct((t, D_MODEL), F32),
        compiler_params=_cparams(("parallel",)),
        name="merge",
    )(x, z, z, z, z, z, ret_o, att_o, wr, wa, wc, wo, cmn, ws, bs_t, mod)


def _ffn_kernel(x_ref, g_ref, sc_ref, sh_ref, w1_ref, w2_ref, g2_ref, gf_ref, o_ref, h_ref, acc_ref, *, final):
    k = pl.program_id(1)

    @pl.when(k == 0)
    def _():
        h_ref[...] = _norm_mod(x_ref[...], g_ref[...], sc_ref[0], sh_ref[0]).astype(BF16)

    a = jnp.square(jnp.maximum(_dot(h_ref[...], w1_ref[...]), 0.0)).astype(BF16)

    @pl.when(k == 0)
    def _():
        acc_ref[...] = _dot(a, w2_ref[...])

    last = pl.num_programs(1) - 1

    @pl.when(jnp.logical_and(k > 0, k < last))
    def _():
        acc_ref[...] += _dot(a, w2_ref[...])

    @pl.when(k == last)
    def _():
        tm = a.shape[0]
        for r0 in range(0, tm, tm // 2):
            rows = slice(r0, r0 + tm // 2)
            xn = x_ref[rows, :] + g2_ref[0] * (acc_ref[rows, :] + _dot(a[rows, :], w2_ref[...]))
            if final:
                xn = xn * lax.rsqrt(jnp.mean(xn * xn, axis=-1, keepdims=True) + EPS) * gf_ref[...]
            o_ref[rows, :] = xn


def _ffn(x, gn2, w1, w2, mod, gfin, layer, seq, latent, final):
    t = x.shape[0]
    tm, tf = 1024, 1024
    mrow = _mod_row_fn(tm, seq, latent)
    return pl.pallas_call(
        functools.partial(_ffn_kernel, final=final),
        grid=(t // tm, D_FF // tf),
        in_specs=[
            pl.BlockSpec((tm, D_MODEL), lambda i, k: (i, 0)),
            _layer_spec(layer, (1, D_MODEL)),
            _mod_spec(layer, MOD_SC2, mrow), _mod_spec(layer, MOD_SH2, mrow),
            pl.BlockSpec((None, D_MODEL, tf), lambda i, k: (layer, 0, k)),
            pl.BlockSpec((None, tf, D_MODEL), lambda i, k: (layer, k, 0)),
            _mod_spec(layer, MOD_G2, mrow),
            pl.BlockSpec((1, D_MODEL), lambda i, k: (0, 0)),
        ],
        out_specs=pl.BlockSpec((tm, D_MODEL), lambda i, k: (i, 0)),
        out_shape=jax.ShapeDtypeStruct((t, D_MODEL), F32),
        scratch_shapes=[pltpu.VMEM((tm, D_MODEL), BF16), pltpu.VMEM((tm, D_MODEL), F32)],
        compiler_params=_cparams(("parallel", "arbitrary")),
        name="ffn_final" if final else "ffn",
    )(x, gn2, mod, mod, w1, w2, mod, gfin)


def _rope_tables(length):
    rows = length // GRID_W
    row = jnp.repeat(jnp.arange(rows, dtype=F32), GRID_W)
    col = jnp.tile(jnp.arange(GRID_W, dtype=F32), rows)
    half = HEAD_DIM // 2
    inv = 1.0 / (ROPE_BASE ** (jnp.arange(0, half, 2, dtype=F32) / half))
    ang = jnp.concatenate([row[:, None] * inv[None, :], col[:, None] * inv[None, :]], axis=-1)
    cos, sin = jnp.cos(ang), jnp.sin(ang)
    cos2 = jnp.repeat(cos, 2, axis=-1)
    sin2 = jnp.stack([-sin, sin], axis=-1).reshape(length, HEAD_DIM)
    return jnp.tile(cos2, (1, 2)), jnp.tile(sin2, (1, 2))


def _permute_w_in(w_in):
    o = np.cumsum((0, 512, 512, 512, 512, 512, 128, 128, 512, 512, 1024, 1024, 1024))
    seg = lambda i: w_in[:, :, int(o[i]):int(o[i + 1])]
    rq, rk, rv, rg, aq, ak, av, cu, cv, gr, ga, gc = (seg(i) for i in range(12))
    aq = aq.reshape(DEPTH, D_MODEL, H_Q, HEAD_DIM)[:, :, np.array(Q_HEAD_ORDER)].reshape(DEPTH, D_MODEL, ATT_Q_W)
    return jnp.concatenate([gr, ga, gc, rq, rk, rv, rg, aq, cu, cv, ak, av], axis=-1).astype(BF16)


def kernel(x_prompt, x_sample, cache_k, cache_v, state_ret, c, c_ctx, w_mod, b_mod, g_norm1, g_norm2, w_in, g_q, g_k, ret_decay, ret_gn, cm_norm, cm_ws, cm_bs, w_br_ret, w_br_att, w_br_cm, w_out, w_ff1, w_ff2, g_final):
    batch, dec_batch = x_prompt.shape[0], x_sample.shape[0]

    cond = jnp.concatenate(
        [c_ctx[None, :], c, jnp.zeros((N_MOD_ROWS - 1 - dec_batch, D_MODEL), F32)], axis=0)
    mod = _modulation(cond, w_mod, b_mod)
    mod = mod.reshape(DEPTH, N_MOD_ROWS, 6, D_MODEL).transpose(0, 2, 1, 3).reshape(DEPTH, 6, N_MOD_ROWS, 1, D_MODEL)

    w_in_p = _permute_w_in(w_in)
    w_br_ret_b = w_br_ret.astype(BF16)
    w_br_att_b = w_br_att.reshape(DEPTH, H_Q, HEAD_DIM, D_MODEL)[:, np.array(Q_HEAD_ORDER)].reshape(
        DEPTH, ATT_Q_W, D_MODEL).astype(BF16)
    w_br_cm_b = w_br_cm.astype(BF16)
    w_out_b = w_out.astype(BF16)
    w_ff1_b = w_ff1.astype(BF16)
    w_ff2_b = w_ff2.astype(BF16)
    cm_ws_b = cm_ws.astype(BF16)
    cm_bs_t = jnp.swapaxes(cm_bs, 1, 2)
    g_q2 = jnp.tile(g_q, (1, 2)).reshape(DEPTH, 1, LANES)
    g_k2 = jnp.tile(g_k, (1, 2)).reshape(DEPTH, 1, LANES)
    cos, sin = _rope_tables(DEC_SEQ)
    cache_k2 = cache_k.reshape(dec_batch, DEPTH, PAST_LEN, LANES)
    cache_v2 = cache_v.reshape(dec_batch, DEPTH, PAST_LEN, LANES)
    g_fin = g_final.reshape(1, D_MODEL)
    gn1 = g_norm1.reshape(DEPTH, 1, D_MODEL)
    gn2 = g_norm2.reshape(DEPTH, 1, D_MODEL)
    ret_gn2 = ret_gn.reshape(DEPTH, 1, RET_W)
    cm_norm2 = cm_norm.reshape(DEPTH, 1, CM_W)

    xp = x_prompt.reshape(batch * SEQ, D_MODEL)
    xs = x_sample.reshape(dec_batch * DEC_SEQ, D_MODEL)
    ks_list, vs_list, ss_list = [], [], []
    for l in range(DEPTH):
        final = l == DEPTH - 1
        for latent in (False, True):
            x = xs if latent else xp
            seq = DEC_SEQ if latent else SEQ
            z = _in_proj(x, gn1, mod, w_in_p, l, seq, latent, cos, sin)
            if latent:
                ret_o = _retention(z, ret_decay, ret_gn2, l, seq, True, state_ret)
                att_o = _attention(z, g_q2, g_k2, l, seq, True, cos, sin, cache_k2, cache_v2)
            else:
                ret_o, s_fin = _retention(z, ret_decay, ret_gn2, l, seq, False)
                att_o, k_new, v_new = _attention(z, g_q2, g_k2, l, seq, False)
                ks_list.append(k_new.reshape(batch, SEQ, H_KV, HEAD_DIM))
                vs_list.append(v_new.reshape(batch, SEQ, H_KV, HEAD_DIM))
                ss_list.append(s_fin)
            x = _merge(x, z, ret_o, att_o, w_br_ret_b, w_br_att_b, w_br_cm_b, w_out_b,
                       cm_norm2, cm_ws_b, cm_bs_t, mod, l, seq, latent)
            x = _ffn(x, gn2, w_ff1_b, w_ff2_b, mod, g_fin, l, seq, latent, final)
            if latent:
                xs = x
            else:
                xp = x
    y_prompt = xp.reshape(batch, SEQ, D_MODEL)
    y_sample = xs.reshape(dec_batch, DEC_SEQ, D_MODEL)
    return (y_prompt, y_sample, jnp.stack(ks_list, axis=1), jnp.stack(vs_list, axis=1),
            jnp.stack(ss_list, axis=1))
```

```python
import functools

import numpy as np
import jax
import jax.numpy as jnp
from jax import lax
from jax.experimental import pallas as pl
from jax.experimental.pallas import tpu as pltpu

D_MODEL = 1024
DEPTH = 2
SEQ = 256
DEC_SEQ = 2048
PAST_LEN = 256
GRID_W = 64
HEAD_DIM = 64
H_RET = 8
H_Q = 8
H_KV = 2
CHUNK = 128
CM_GROUPS = 4
RET_W = 512
ATT_Q_W = 512
ATT_KV_W = 128
CM_W = 512
D_FF = 4096
ROPE_BASE = 10000.0
EPS = 1e-6
LOG2_E = 1.4426950408889634
IN_W = 6912
N_MOD_ROWS = 16
MOD_SH1, MOD_SC1, MOD_G1, MOD_SH2, MOD_SC2, MOD_G2 = range(6)

LANES = 128
MXU_N = 256
RET_GROUP = 16
RET_CTX_BATCH = 16
ATT_TQ = 512
ATT_LAG = 2
ATT_BUFS = ATT_LAG + 1
VMEM_LIMIT = 48 * 1024 * 1024

F32 = jnp.float32
BF16 = jnp.bfloat16

COL_GR, COL_GA, COL_GC = 0, 1024, 2048
COL_RQ, COL_RK, COL_RV, COL_RG = 3072, 3584, 4096, 4608
COL_AQ, COL_CU, COL_CV = 5120, 5632, 6144
COL_AK, COL_AV = 6656, 6784
Q_HEAD_ORDER = (0, 4, 1, 5, 2, 6, 3, 7)


def _cparams(sem):
    return pltpu.CompilerParams(dimension_semantics=sem, vmem_limit_bytes=VMEM_LIMIT)


def _dot(a, b):
    return jnp.dot(a, b, preferred_element_type=F32)


def _dot_nt(a, b):
    return lax.dot_general(a, b, (((1,), (1,)), ((), ())), preferred_element_type=F32)


def _dot_tn(a, b):
    return lax.dot_general(a, b, (((0,), (0,)), ((), ())), preferred_element_type=F32)


def _lo_mask(shape):
    return lax.broadcasted_iota(jnp.int32, shape, len(shape) - 1) < HEAD_DIM


def _seg_mean(x, lo):
    s_lo = jnp.sum(jnp.where(lo, x, 0.0), axis=-1, keepdims=True)
    s_hi = jnp.sum(jnp.where(lo, 0.0, x), axis=-1, keepdims=True)
    return jnp.where(lo, s_lo, s_hi) * (1.0 / HEAD_DIM)


def _rope(x, cos, sin_signed):
    nxt = pltpu.roll(x, LANES - 1, 1)
    prv = pltpu.roll(x, 1, 1)
    even = (lax.broadcasted_iota(jnp.int32, x.shape, 1) & 1) == 0
    return x * cos + jnp.where(even, nxt, prv) * sin_signed


def _sigmoid(x):
    return 0.5 * jnp.tanh(0.5 * x) + 0.5


def _silu(x):
    return x * _sigmoid(x)


def _seg_mean_mxu(x, bmat, passes):
    hi = x.astype(BF16)
    out = _dot(hi, bmat)
    if passes == 2:
        out = out + _dot((x - hi.astype(F32)).astype(BF16), bmat)
    return out


def _norm_mod(x, g, sc, sh):
    ms = jnp.mean(x * x, axis=-1, keepdims=True)
    y = x * lax.rsqrt(ms + EPS) * g
    return y * (1.0 + sc) + sh


def _mod_kernel(c_ref, w_ref, b_ref, o_ref):
    s = _silu(c_ref[...])
    o_ref[0] = _dot(s.astype(BF16), w_ref[0].astype(BF16)) + b_ref[0]


def _modulation(cond, w_mod, b_mod):
    tn = 1536
    n = 6 * D_MODEL
    return pl.pallas_call(
        _mod_kernel,
        grid=(DEPTH, n // tn),
        in_specs=[
            pl.BlockSpec((N_MOD_ROWS, D_MODEL), lambda l, j: (0, 0)),
            pl.BlockSpec((1, D_MODEL, tn), lambda l, j: (l, 0, j)),
            pl.BlockSpec((1, 1, tn), lambda l, j: (l, 0, j)),
        ],
        out_specs=pl.BlockSpec((1, N_MOD_ROWS, tn), lambda l, j: (l, 0, j)),
        out_shape=jax.ShapeDtypeStruct((DEPTH, N_MOD_ROWS, n), F32),
        compiler_params=_cparams(("arbitrary", "arbitrary")),
        name="modulation",
    )(cond, w_mod, b_mod.reshape(DEPTH, 1, n))


def _mod_row_fn(tm, seq, latent):
    if latent:
        return lambda i: 1 + (i * tm) // seq
    return lambda i: 0


def _mod_spec(layer, which, mrow):
    return pl.BlockSpec((None, None, 1, 1, D_MODEL), lambda i, *_: (layer, which, mrow(i), 0, 0))


def _layer_spec(layer, shape):
    return pl.BlockSpec((None,) + shape, lambda *_: (layer,) + (0,) * len(shape))


def _proj_epilogues(tn):
    kinds = ((COL_GR, COL_RQ, "sigmoid"), (COL_RQ, COL_RV, "rope"), (COL_RV, COL_RG, None),
             (COL_RG, COL_AQ, "silu"), (COL_AQ, COL_CU, None), (COL_CU, COL_AK, "gelu"), (COL_AK, IN_W, None))
    tiles = [(c, c + MXU_N, kind) for a, e, kind in kinds for c in range(a, e, MXU_N)]
    return [[(a - b * tn, e - b * tn, kind) for a, e, kind in tiles if b * tn <= a < (b + 1) * tn]
            for b in range(IN_W // tn)]


def _proj_kernel(*refs, tn, latent):
    if latent:
        x_ref, g_ref, sc_ref, sh_ref, w_ref, cos_ref, sin_ref, o_ref, h_ref = refs
    else:
        x_ref, g_ref, sc_ref, sh_ref, w_ref, o_ref, h_ref = refs
    j = pl.program_id(1)

    def rope_tile(y):
        if not latent:
            return y
        halves = [_rope(y[:, c:c + LANES], cos_ref[...], sin_ref[...]) for c in range(0, MXU_N, LANES)]
        return jnp.concatenate(halves, axis=1)

    epilogue = {None: lambda y: y, "sigmoid": _sigmoid, "silu": _silu, "rope": rope_tile,
                "gelu": functools.partial(jax.nn.gelu, approximate=True)}

    tm = x_ref.shape[0]
    for b, segs in enumerate(_proj_epilogues(tn)):
        @pl.when(j == b)
        def _(b=b, segs=segs):
            for r0 in range(0, tm, tm // 2) if b == 0 else (None,):
                rows = slice(None) if r0 is None else slice(r0, r0 + tm // 2)
                if b == 0:
                    h_ref[rows, :] = _norm_mod(x_ref[rows, :], g_ref[...], sc_ref[0], sh_ref[0]).astype(BF16)
                for a, e, kind in segs:
                    o_ref[rows, a:e] = epilogue[kind](_dot(h_ref[rows, :], w_ref[:, a:e])).astype(BF16)


def _in_proj(x, g, mod, w, layer, seq, latent, cos=None, sin=None):
    t = x.shape[0]
    tm, tn = 1024, 2304
    mrow = _mod_row_fn(tm, seq, latent)
    return pl.pallas_call(
        functools.partial(_proj_kernel, tn=tn, latent=latent),
        grid=(t // tm, IN_W // tn),
        in_specs=[
            pl.BlockSpec((tm, D_MODEL), lambda i, j: (i, 0)),
            _layer_spec(layer, (1, D_MODEL)),
            _mod_spec(layer, MOD_SC1, mrow), _mod_spec(layer, MOD_SH1, mrow),
            pl.BlockSpec((None, D_MODEL, tn), lambda i, j: (layer, 0, j)),
        ] + ([pl.BlockSpec((tm, LANES), lambda i, j: (i % (seq // tm), 0))] * 2 if latent else []),
        out_specs=pl.BlockSpec((tm, tn), lambda i, j: (i, j)),
        out_shape=jax.ShapeDtypeStruct((t, IN_W), BF16),
        scratch_shapes=[pltpu.VMEM((tm, D_MODEL), BF16)],
        compiler_params=_cparams(("parallel", "arbitrary")),
        name="in_proj",
    )(x, g, mod, mod, w, *((cos, sin) if latent else ()))


def _ret_kernel(*refs, nb, nc, latent, layer):
    if latent:
        (dec_ref, q_ref, k_ref, v_ref, rg_ref, gn_ref, s0_ref,
         o_ref, cst_ref, uu_ref) = refs
    else:
        (dec_ref, q_ref, k_ref, v_ref, rg_ref, gn_ref,
         o_ref, sfin_ref, cst_ref, uu_ref) = refs
    hp = pl.program_id(0)
    sq = (CHUNK, LANES)
    lane = lax.broadcasted_iota(jnp.int32, sq, 1)
    row = lax.broadcasted_iota(jnp.int32, sq, 0)
    lo = lane < HEAD_DIM
    blockdiag = (row < HEAD_DIM) == lo
    bmat = jnp.where(blockdiag, 1.0 / HEAD_DIM, 0.0).astype(BF16)
    @pl.when(pl.program_id(1) == 0)
    def _():
        rowf = row.astype(F32)
        rel = (row - lane).astype(F32)

        def log_decay(direction, head):
            d = jnp.full(sq, dec_ref[layer, direction, head], F32)
            return -(jnp.maximum(d, 0.0) + jnp.log1p(jnp.exp(-jnp.abs(d))))

        ldf0, ldf1 = log_decay(0, 2 * hp), log_decay(0, 2 * hp + 1)
        ldb0, ldb1 = log_decay(1, 2 * hp), log_decay(1, 2 * hp + 1)
        ldf = jnp.where(lo, ldf0, ldf1)
        ldb = jnp.where(lo, ldb0, ldb1)

        def decay_mask(lf, lb):
            fwd = jnp.exp(lf * jnp.maximum(rel, 0.0))
            bwd = jnp.exp(lb * jnp.maximum(-rel, 0.0))
            return jnp.where(rel > 0, fwd, jnp.where(rel < 0, bwd, 2.0))

        cst_ref[0] = decay_mask(ldf0, ldb0)
        cst_ref[1] = decay_mask(ldf1, ldb1)
        cst_ref[2] = jnp.exp(ldf * (CHUNK - 1.0 - rowf))
        cst_ref[3] = jnp.exp(ldb * rowf)
        cst_ref[4] = jnp.exp(ldf * (rowf + 1.0))
        cst_ref[5] = jnp.exp(ldb * (CHUNK - rowf))
        cst_ref[6] = jnp.exp(ldf * float(CHUNK))
        cst_ref[7] = jnp.exp(ldb * float(CHUNK))

    def rows_of(c):
        return pl.ds(pl.multiple_of(c * CHUNK, CHUNK), CHUNK)

    def load_qk(ref, c, scale):
        x = ref[rows_of(c), :].astype(F32)
        return x if scale == 1.0 else x * scale

    nchunks = nb * nc
    group = min(nchunks, RET_GROUP)

    def phase_a(i, carry):
        cs = [i * group + g for g in range(group)]
        ks = [load_qk(k_ref, c, HEAD_DIM ** -0.5) for c in cs]
        kzs = [jnp.concatenate([(k * cst_ref[2]).astype(BF16), (k * cst_ref[3]).astype(BF16)], axis=1)
               for k in ks]
        for c, kz in zip(cs, kzs):
            uu_ref[c] = _dot_tn(kz, v_ref[rows_of(c), :])
        return carry

    lax.fori_loop(0, nchunks // group, phase_a, 0)

    zeros = jnp.zeros((HEAD_DIM, HEAD_DIM), F32)

    def embed(direction):
        top = jnp.concatenate([s0_ref[0, 0, direction, 0], zeros], axis=1)
        bot = jnp.concatenate([zeros, s0_ref[0, 0, direction, 1]], axis=1)
        return jnp.concatenate([top, bot], axis=0)

    for bi in range(nb):
        sf = embed(0) if latent else jnp.zeros(sq, F32)
        sb = embed(1) if latent else jnp.zeros(sq, F32)
        for c in range(bi * nc, (bi + 1) * nc):
            u = uu_ref[c, 0:CHUNK, :]
            uu_ref[c, 0:CHUNK, :] = sf
            sf = cst_ref[6] * sf + jnp.where(blockdiag, u, 0.0)
        for c in reversed(range(bi * nc, (bi + 1) * nc)):
            u = uu_ref[c, CHUNK:2 * CHUNK, :]
            uu_ref[c, CHUNK:2 * CHUNK, :] = sb
            sb = cst_ref[7] * sb + jnp.where(blockdiag, u, 0.0)
        if not latent:
            for d, st in ((0, sf), (1, sb)):
                sfin_ref[bi, d, 0] = st[0:HEAD_DIM, 0:HEAD_DIM]
                sfin_ref[bi, d, 1] = st[HEAD_DIM:, HEAD_DIM:]

    def phase_c(i, carry):
        cs = [i * group + g for g in range(group)]
        n = range(group)
        q = [load_qk(q_ref, c, 1.0) for c in cs]
        k = [load_qk(k_ref, c, HEAD_DIM ** -0.5) for c in cs]
        v = [v_ref[rows_of(c), :] for c in cs]
        qb = [x.astype(BF16) for x in q]
        s0 = [_dot_nt(qb[g], jnp.where(lo, k[g], 0.0).astype(BF16)) for g in n]
        s1 = [_dot_nt(qb[g], jnp.where(lo, 0.0, k[g]).astype(BF16)) for g in n]
        qx = [jnp.concatenate([(x * cst_ref[4]).astype(BF16), (x * cst_ref[5]).astype(BF16)], axis=1) for x in q]
        cross = [_dot(qx[g], uu_ref[cs[g]].astype(BF16)) for g in n]
        i0 = [_dot((s0[g] * cst_ref[0]).astype(BF16), v[g]) for g in n]
        i1 = [_dot((s1[g] * cst_ref[1]).astype(BF16), v[g]) for g in n]
        o = [jnp.where(lo, i0[g], i1[g]) + cross[g] for g in n]
        mu = [_seg_mean_mxu(x, bmat, 2) for x in o]
        d = [o[g] - mu[g] for g in n]
        var = [_seg_mean_mxu(x * x, bmat, 1) for x in d]
        for g in n:
            y = d[g] * lax.rsqrt(var[g] + EPS) * gn_ref[...]
            o_ref[rows_of(cs[g]), :] = (y * rg_ref[rows_of(cs[g]), :].astype(F32)).astype(BF16)
        return carry

    lax.fori_loop(0, nchunks // group, phase_c, 0)


def _retention(z, dec, gn, layer, seq, latent, s0=None):
    t = z.shape[0]
    b = t // seq
    nc = seq // CHUNK
    npair = H_RET // 2
    nb = 1 if latent else RET_CTX_BATCH
    rows = nb * seq

    def zcol(base):
        return pl.BlockSpec((rows, LANES), lambda hp, bi: (bi, base // LANES + hp))

    in_specs = [
        pl.BlockSpec(memory_space=pltpu.SMEM),
        zcol(COL_RQ), zcol(COL_RK), zcol(COL_RV), zcol(COL_RG),
        pl.BlockSpec((None, 1, LANES), lambda hp, bi: (layer, 0, hp)),
    ]
    args = [dec, z, z, z, z, gn]
    out_shape = [jax.ShapeDtypeStruct((t, RET_W), BF16)]
    out_specs = [pl.BlockSpec((rows, LANES), lambda hp, bi: (bi, hp))]
    if latent:
        in_specs.append(pl.BlockSpec((1, 1, 2, 2, HEAD_DIM, HEAD_DIM), lambda hp, bi: (bi, layer, 0, hp, 0, 0)))
        args.append(s0)
    else:
        out_shape.append(jax.ShapeDtypeStruct((b, 2, H_RET, HEAD_DIM, HEAD_DIM), F32))
        out_specs.append(pl.BlockSpec((nb, 2, 2, HEAD_DIM, HEAD_DIM), lambda hp, bi: (bi, 0, hp, 0, 0)))
    res = pl.pallas_call(
        functools.partial(_ret_kernel, nb=nb, nc=nc, latent=latent, layer=layer),
        grid=(npair, b // nb),
        in_specs=in_specs,
        out_specs=out_specs,
        out_shape=out_shape,
        scratch_shapes=[pltpu.VMEM((8, CHUNK, LANES), F32), pltpu.VMEM((nb * nc, 2 * CHUNK, LANES), F32)],
        compiler_params=_cparams(("parallel", "arbitrary")),
        name="retention_latent" if latent else "retention_ctx",
    )(*args)
    return res if not latent else res[0]


def _att_kernel(*refs, tq, seq, latent):
    if latent:
        (q_ref, ak_ref, av_ref, ck_ref, cv_ref, gq_ref, gk_ref, cosq_ref, sinq_ref, cosk_ref, sink_ref,
         o_ref, k0_ref, k1_ref, vv_ref, *sp_refs) = refs
        off = PAST_LEN
    else:
        (q_ref, ak_ref, av_ref, gq_ref, gk_ref,
         o_ref, knew_ref, vnew_ref, k0_ref, k1_ref, vv_ref, *sp_refs) = refs
        off = 0

    @pl.when(pl.program_id(1) == 0)
    def _():
        lo = _lo_mask((seq, LANES))
        ak = ak_ref[...].astype(F32)
        kn = ak * lax.rsqrt(_seg_mean(ak * ak, lo) + EPS) * gk_ref[...]
        av = av_ref[...]
        if latent:
            kn = _rope(kn, cosk_ref[...], sink_ref[...])
            lo_c = _lo_mask((PAST_LEN, LANES))
            ck = ck_ref[0, 0]
            k0_ref[:, 0:off] = jnp.where(lo_c, ck, 0.0).T.astype(BF16)
            k1_ref[:, 0:off] = jnp.where(lo_c, 0.0, ck).T.astype(BF16)
            vv_ref[0:off, :] = cv_ref[0, 0].astype(BF16)
        else:
            knew_ref[...] = kn
            vnew_ref[...] = av.astype(F32)
        k0_ref[:, off:off + seq] = jnp.where(lo, kn, 0.0).T.astype(BF16)
        k1_ref[:, off:off + seq] = jnp.where(lo, 0.0, kn).T.astype(BF16)
        vv_ref[off:off + seq, :] = av

    sk = off + seq
    nkt = sk // MXU_N
    lo = _lo_mask((CHUNK, LANES))
    krefs = (k0_ref, k1_ref)
    units = [(sub, r, g) for sub in range(tq // CHUNK) for r in range(ATT_Q_W // LANES) for g in range(H_KV)]
    qb, m_lane, m_row, l_lane, o_acc, done = {}, {}, {}, {}, {}, {}

    def prep(sub, r):
        rows = slice(sub * CHUNK, (sub + 1) * CHUNK)
        x = q_ref[rows, r * LANES:(r + 1) * LANES].astype(F32)
        xn = x * lax.rsqrt(_seg_mean(x * x, lo) + EPS) * gq_ref[...]
        if latent:
            xn = _rope(xn, cosq_ref[rows, :], sinq_ref[rows, :])
        qb[(sub, r)] = (xn * (HEAD_DIM ** -0.5 * LOG2_E)).astype(BF16)

    def score_tile(i, kt):
        sub, r, g = units[i]
        cols = slice(kt * MXU_N, (kt + 1) * MXU_N)
        s_t = _dot(qb[(sub, r)], krefs[g][:, cols])
        sp_refs[i % ATT_BUFS][:, cols] = s_t
        t = jnp.maximum(s_t[:, :LANES], s_t[:, LANES:])
        m_lane[i] = t if kt == 0 else jnp.maximum(m_lane[i], t)

    def soft_tile(i, kt):
        cols = slice(kt * MXU_N, (kt + 1) * MXU_N)
        sbuf = sp_refs[i % ATT_BUFS]
        p0 = jnp.exp2(sbuf[:, kt * MXU_N:kt * MXU_N + LANES] - m_row[i])
        p1 = jnp.exp2(sbuf[:, kt * MXU_N + LANES:(kt + 1) * MXU_N] - m_row[i])
        pv = _dot(jnp.concatenate([p0, p1], axis=1).astype(BF16), vv_ref[cols, :])
        if kt == 0:
            l_lane[i], o_acc[i] = p0 + p1, pv
        else:
            l_lane[i], o_acc[i] = l_lane[i] + (p0 + p1), o_acc[i] + pv

    def finish(i):
        sub, r, g = units[i]
        done[(sub, r, g)] = o_acc.pop(i) / jnp.sum(l_lane.pop(i), axis=-1, keepdims=True)
        if g == H_KV - 1:
            rows = slice(sub * CHUNK, (sub + 1) * CHUNK)
            o = jnp.where(lo, done.pop((sub, r, 0)), done.pop((sub, r, 1)))
            o_ref[rows, r * LANES:(r + 1) * LANES] = o.astype(BF16)

    prep(units[0][0], units[0][1])
    for s in range(len(units) + ATT_LAG):
        nxt = s + H_KV
        if s % H_KV == 0 and nxt < len(units):
            prep(units[nxt][0], units[nxt][1])
        for kt in range(nkt):
            if s < len(units):
                score_tile(s, kt)
            if s >= ATT_LAG:
                soft_tile(s - ATT_LAG, kt)
        if s < len(units):
            m_row[s] = jnp.broadcast_to(jnp.max(m_lane.pop(s), axis=-1, keepdims=True), (CHUNK, LANES))
        if s >= ATT_LAG:
            finish(s - ATT_LAG)


def _attention(z, gq, gk, layer, seq, latent, cos=None, sin=None, cache_k=None, cache_v=None):
    t = z.shape[0]
    b = t // seq
    tq = min(seq, ATT_TQ)
    nq = seq // tq
    sk = seq + (PAST_LEN if latent else 0)
    kv_spec = lambda base: pl.BlockSpec((seq, LANES), lambda bi, qi: (bi, base // LANES))
    vec_spec = _layer_spec(layer, (1, LANES))
    in_specs = [
        pl.BlockSpec((tq, ATT_Q_W), lambda bi, qi: (bi * nq + qi, COL_AQ // ATT_Q_W)),
        kv_spec(COL_AK), kv_spec(COL_AV),
    ]
    args = [z, z, z]
    if latent:
        cache_spec = pl.BlockSpec((1, 1, PAST_LEN, LANES), lambda bi, qi: (bi, layer, 0, 0))
        in_specs += [cache_spec, cache_spec]
        args += [cache_k, cache_v]
    in_specs += [vec_spec, vec_spec]
    args += [gq, gk]
    out_shape = [jax.ShapeDtypeStruct((t, ATT_Q_W), BF16)]
    out_specs = [pl.BlockSpec((tq, ATT_Q_W), lambda bi, qi: (bi * nq + qi, 0))]
    if latent:
        in_specs += [
            pl.BlockSpec((tq, LANES), lambda bi, qi: (qi, 0)),
            pl.BlockSpec((tq, LANES), lambda bi, qi: (qi, 0)),
            pl.BlockSpec((seq, LANES), lambda bi, qi: (0, 0)),
            pl.BlockSpec((seq, LANES), lambda bi, qi: (0, 0)),
        ]
        args += [cos, sin, cos, sin]
    else:
        new_spec = pl.BlockSpec((seq, LANES), lambda bi, qi: (bi, 0))
        out_shape += [jax.ShapeDtypeStruct((t, LANES), F32)] * 2
        out_specs += [new_spec, new_spec]
    res = pl.pallas_call(
        functools.partial(_att_kernel, tq=tq, seq=seq, latent=latent),
        grid=(b, nq),
        in_specs=in_specs,
        out_specs=out_specs,
        out_shape=out_shape,
        scratch_shapes=([pltpu.VMEM((LANES, sk), BF16)] * 2 + [pltpu.VMEM((sk, LANES), BF16)]
                        + [pltpu.VMEM((CHUNK, sk), F32)] * ATT_BUFS),
        compiler_params=_cparams(("parallel", "arbitrary")),
        name="attention_latent" if latent else "attention_ctx",
    )(*args)
    return res if not latent else res[0]


def _merge_kernel(x_ref, gr_ref, ga_ref, gc_ref, cu_ref, cv_ref, ro_ref, ao_ref,
                  wr_ref, wa_ref, wc_ref, wo_ref, cmn_ref, ws_ref, bs_ref, g1_ref,
                  gn2_ref, sc2_ref, sh2_ref, o_ref, h_ref, *, tm):
    u = cu_ref[...].astype(F32)
    v = cv_ref[...].astype(F32)
    vn = v * lax.rsqrt(jnp.mean(v * v, axis=-1, keepdims=True) + EPS) * cmn_ref[...]
    vb = vn.astype(BF16)
    rows = []
    for n in range(tm // CHUNK):
        cols = []
        for g in range(CM_GROUPS):
            vg = vb[n * CHUNK:(n + 1) * CHUNK, g * LANES:(g + 1) * LANES]
            cols.append(_dot(ws_ref[g], vg) + bs_ref[:, g:g + 1])
        rows.append(jnp.concatenate(cols, axis=1))
    cm = (u * jnp.concatenate(rows, axis=0)).astype(BF16)
    merged = (gr_ref[...].astype(F32) * _dot(ro_ref[...], wr_ref[...])
              + ga_ref[...].astype(F32) * _dot(ao_ref[...], wa_ref[...])
              + gc_ref[...].astype(F32) * _dot(cm, wc_ref[...]))
    mb = merged.astype(BF16)
    for r0 in range(0, tm, tm // 2):
        rows = slice(r0, r0 + tm // 2)
        xn = x_ref[rows, :] + g1_ref[0] * _dot(mb[rows, :], wo_ref[...])
        o_ref[rows, :] = xn
        h_ref[rows, :] = _norm_mod(xn, gn2_ref[...], sc2_ref[0], sh2_ref[0]).astype(BF16)


def _merge(x, z, ret_o, att_o, wr, wa, wc, wo, cmn, ws, bs_t, mod, gn2, layer, seq, latent):
    t = x.shape[0]
    tm = 512
    mrow = _mod_row_fn(tm, seq, latent)
    row_spec = lambda w, cb: pl.BlockSpec((tm, w), lambda i: (i, cb))
    of_layer = lambda shape: _layer_spec(layer, shape)
    return pl.pallas_call(
        functools.partial(_merge_kernel, tm=tm),
        grid=(t // tm,),
        in_specs=[
            row_spec(D_MODEL, 0),
            row_spec(D_MODEL, COL_GR // D_MODEL), row_spec(D_MODEL, COL_GA // D_MODEL),
            row_spec(D_MODEL, COL_GC // D_MODEL),
            row_spec(CM_W, COL_CU // CM_W), row_spec(CM_W, COL_CV // CM_W),
            row_spec(RET_W, 0), row_spec(ATT_Q_W, 0),
            of_layer((RET_W, D_MODEL)), of_layer((ATT_Q_W, D_MODEL)), of_layer((CM_W, D_MODEL)),
            of_layer((D_MODEL, D_MODEL)),
            of_layer((1, CM_W)), of_layer((CM_GROUPS, CHUNK, CHUNK)), of_layer((CHUNK, CM_GROUPS)),
            _mod_spec(layer, MOD_G1, mrow), of_layer((1, D_MODEL)),
            _mod_spec(layer, MOD_SC2, mrow), _mod_spec(layer, MOD_SH2, mrow),
        ],
        out_specs=[row_spec(D_MODEL, 0), row_spec(D_MODEL, 0)],
        out_shape=[jax.ShapeDtypeStruct((t, D_MODEL), F32), jax.ShapeDtypeStruct((t, D_MODEL), BF16)],
        compiler_params=_cparams(("parallel",)),
        name="merge",
    )(x, z, z, z, z, z, ret_o, att_o, wr, wa, wc, wo, cmn, ws, bs_t, mod, gn2, mod, mod)


def _ffn_kernel(x_ref, h_ref, w1_ref, w2_ref, g2_ref, gf_ref, o_ref, acc_ref, *, final):
    k = pl.program_id(1)
    a = jnp.square(jnp.maximum(_dot(h_ref[...], w1_ref[...]), 0.0)).astype(BF16)

    @pl.when(k == 0)
    def _():
        acc_ref[...] = _dot(a, w2_ref[...])

    last = pl.num_programs(1) - 1

    @pl.when(jnp.logical_and(k > 0, k < last))
    def _():
        acc_ref[...] += _dot(a, w2_ref[...])

    @pl.when(k == last)
    def _():
        tm = a.shape[0]
        for r0 in range(0, tm, tm // 2):
            rows = slice(r0, r0 + tm // 2)
            xn = x_ref[rows, :] + g2_ref[0] * (acc_ref[rows, :] + _dot(a[rows, :], w2_ref[...]))
            if final:
                xn = xn * lax.rsqrt(jnp.mean(xn * xn, axis=-1, keepdims=True) + EPS) * gf_ref[...]
            o_ref[rows, :] = xn


def _ffn(x, h, w1, w2, mod, gfin, layer, seq, latent, final):
    t = x.shape[0]
    tm, tf = 1024, 1024
    mrow = _mod_row_fn(tm, seq, latent)
    return pl.pallas_call(
        functools.partial(_ffn_kernel, final=final),
        grid=(t // tm, D_FF // tf),
        in_specs=[
            pl.BlockSpec((tm, D_MODEL), lambda i, k: (i, 0)),
            pl.BlockSpec((tm, D_MODEL), lambda i, k: (i, 0)),
            pl.BlockSpec((None, D_MODEL, tf), lambda i, k: (layer, 0, k)),
            pl.BlockSpec((None, tf, D_MODEL), lambda i, k: (layer, k, 0)),
            _mod_spec(layer, MOD_G2, mrow),
            pl.BlockSpec((1, D_MODEL), lambda i, k: (0, 0)),
        ],
        out_specs=pl.BlockSpec((tm, D_MODEL), lambda i, k: (i, 0)),
        out_shape=jax.ShapeDtypeStruct((t, D_MODEL), F32),
        scratch_shapes=[pltpu.VMEM((tm, D_MODEL), F32)],
        compiler_params=_cparams(("parallel", "arbitrary")),
        name="ffn_final" if final else "ffn",
    )(x, h, w1, w2, mod, gfin)


def _rope_tables(length):
    rows = length // GRID_W
    row = jnp.repeat(jnp.arange(rows, dtype=F32), GRID_W)
    col = jnp.tile(jnp.arange(GRID_W, dtype=F32), rows)
    half = HEAD_DIM // 2
    inv = 1.0 / (ROPE_BASE ** (jnp.arange(0, half, 2, dtype=F32) / half))
    ang = jnp.concatenate([row[:, None] * inv[None, :], col[:, None] * inv[None, :]], axis=-1)
    cos, sin = jnp.cos(ang), jnp.sin(ang)
    cos2 = jnp.repeat(cos, 2, axis=-1)
    sin2 = jnp.stack([-sin, sin], axis=-1).reshape(length, HEAD_DIM)
    return jnp.tile(cos2, (1, 2)), jnp.tile(sin2, (1, 2))


def _permute_w_in(w_in):
    ret, aq, kv, cm, gates = (w_in[:, :, a:e] for a, e in
                              ((0, 2048), (2048, 2560), (2560, 2816), (2816, 3840), (3840, IN_W)))
    aq = aq.reshape(DEPTH, D_MODEL, H_Q, HEAD_DIM)[:, :, np.array(Q_HEAD_ORDER)].reshape(DEPTH, D_MODEL, ATT_Q_W)
    return jnp.concatenate([gates, ret, aq, cm, kv], axis=-1).astype(BF16)


def kernel(x_prompt, x_sample, cache_k, cache_v, state_ret, c, c_ctx, w_mod, b_mod, g_norm1, g_norm2, w_in, g_q, g_k, ret_decay, ret_gn, cm_norm, cm_ws, cm_bs, w_br_ret, w_br_att, w_br_cm, w_out, w_ff1, w_ff2, g_final):
    batch, dec_batch = x_prompt.shape[0], x_sample.shape[0]

    cond = jnp.concatenate(
        [c_ctx[None, :], c, jnp.zeros((N_MOD_ROWS - 1 - dec_batch, D_MODEL), F32)], axis=0)
    mod = _modulation(cond, w_mod, b_mod)
    mod = mod.reshape(DEPTH, N_MOD_ROWS, 6, D_MODEL).transpose(0, 2, 1, 3).reshape(DEPTH, 6, N_MOD_ROWS, 1, D_MODEL)

    w_in_p = _permute_w_in(w_in)
    w_br_ret_b = w_br_ret.astype(BF16)
    w_br_att_b = w_br_att.reshape(DEPTH, H_Q, HEAD_DIM, D_MODEL)[:, np.array(Q_HEAD_ORDER)].reshape(
        DEPTH, ATT_Q_W, D_MODEL).astype(BF16)
    w_br_cm_b = w_br_cm.astype(BF16)
    w_out_b = w_out.astype(BF16)
    w_ff1_b = w_ff1.astype(BF16)
    w_ff2_b = w_ff2.astype(BF16)
    cm_ws_b = cm_ws.astype(BF16)
    cm_bs_t = jnp.swapaxes(cm_bs, 1, 2)
    g_q2 = jnp.tile(g_q, (1, 2)).reshape(DEPTH, 1, LANES)
    g_k2 = jnp.tile(g_k, (1, 2)).reshape(DEPTH, 1, LANES)
    cos, sin = _rope_tables(DEC_SEQ)
    cache_k2 = cache_k.reshape(dec_batch, DEPTH, PAST_LEN, LANES)
    cache_v2 = cache_v.reshape(dec_batch, DEPTH, PAST_LEN, LANES)
    g_fin = g_final.reshape(1, D_MODEL)
    gn1 = g_norm1.reshape(DEPTH, 1, D_MODEL)
    gn2 = g_norm2.reshape(DEPTH, 1, D_MODEL)
    ret_gn2 = ret_gn.reshape(DEPTH, 1, RET_W)
    cm_norm2 = cm_norm.reshape(DEPTH, 1, CM_W)

    xp = x_prompt.reshape(batch * SEQ, D_MODEL)
    xs = x_sample.reshape(dec_batch * DEC_SEQ, D_MODEL)
    ks_list, vs_list, ss_list = [], [], []
    for l in range(DEPTH):
        final = l == DEPTH - 1
        for latent in (False, True):
            x = xs if latent else xp
            seq = DEC_SEQ if latent else SEQ
            z = _in_proj(x, gn1, mod, w_in_p, l, seq, latent, cos, sin)
            if latent:
                ret_o = _retention(z, ret_decay, ret_gn2, l, seq, True, state_ret)
                att_o = _attention(z, g_q2, g_k2, l, seq, True, cos, sin, cache_k2, cache_v2)
            else:
                ret_o, s_fin = _retention(z, ret_decay, ret_gn2, l, seq, False)
                att_o, k_new, v_new = _attention(z, g_q2, g_k2, l, seq, False)
                ks_list.append(k_new.reshape(batch, SEQ, H_KV, HEAD_DIM))
                vs_list.append(v_new.reshape(batch, SEQ, H_KV, HEAD_DIM))
                ss_list.append(s_fin)
            x, h2 = _merge(x, z, ret_o, att_o, w_br_ret_b, w_br_att_b, w_br_cm_b, w_out_b,
                           cm_norm2, cm_ws_b, cm_bs_t, mod, gn2, l, seq, latent)
            x = _ffn(x, h2, w_ff1_b, w_ff2_b, mod, g_fin, l, seq, latent, final)
            if latent:
                xs = x
            else:
                xp = x
    y_prompt = xp.reshape(batch, SEQ, D_MODEL)
    y_sample = xs.reshape(dec_batch, DEC_SEQ, D_MODEL)
    return (y_prompt, y_sample, jnp.stack(ks_list, axis=1), jnp.stack(vs_list, axis=1),
            jnp.stack(ss_list, axis=1))
```
